```python
import jax, jax.numpy as jnp
from jax import lax
import numpy as np

D_MODEL = 1024
BATCH = 2
SEQ = 8192
DEPTH = 2

CHUNK = 64
N_MIXERS = 2
N_HEADS = 16
HEAD_DIM = D_MODEL // N_HEADS
D_FF = 4 * D_MODEL
CONV_WIDTH = 31
Q_BLOCK = 128
RMS_EPS = 1e-6
LN_EPS = 1e-5
MASK_VALUE = -1e30
N_CONV_LAYERS = (DEPTH + 1) // 2
N_ATTN_LAYERS = DEPTH // 2

kernel_name = "hybrid_conformer_conv_fox_sqrelu_sandwich"


def rmsnorm(x, g):
    xf = x.astype(jnp.float32)
    y = xf * lax.rsqrt(jnp.mean(jnp.square(xf), axis=-1, keepdims=True) + RMS_EPS)
    return (y * g.astype(jnp.float32)).astype(x.dtype)


def layernorm(x, g, b):
    xf = x.astype(jnp.float32)
    mu = jnp.mean(xf, axis=-1, keepdims=True)
    var = jnp.mean(jnp.square(xf - mu), axis=-1, keepdims=True)
    y = (xf - mu) * lax.rsqrt(var + LN_EPS)
    return (y * g.astype(jnp.float32) + b.astype(jnp.float32)).astype(x.dtype)


def conformer_conv(h, pw1_w, pw1_b, dw_w, dw_b, ln_g, ln_b, pw2_w, pw2_b):
    a = h @ pw1_w + pw1_b
    u, gate = jnp.split(a, 2, axis=-1)
    u = u * jax.nn.sigmoid(gate)
    y = lax.conv_general_dilated(
        u, dw_w[:, None, :],
        window_strides=(1,),
        padding=((CONV_WIDTH - 1, 0),),
        dimension_numbers=("NWC", "WIO", "NWC"),
        feature_group_count=D_MODEL) + dw_b
    y = layernorm(y, ln_g, ln_b)
    y = jax.nn.silu(y)
    return y @ pw2_w + pw2_b


def forgetting_attention(h, w_in, b_f, w_o):
    B, S, _ = h.shape
    proj = h @ w_in
    q = proj[..., :D_MODEL]
    k = proj[..., D_MODEL:2 * D_MODEL]
    v = proj[..., 2 * D_MODEL:3 * D_MODEL]
    f_logit = proj[..., 3 * D_MODEL:].astype(jnp.float32) + b_f.astype(jnp.float32)
    log_f = jax.nn.log_sigmoid(f_logit)
    F = jnp.cumsum(log_f, axis=1).transpose(0, 2, 1)

    def heads(t):
        return t.reshape(B, S, N_HEADS, HEAD_DIM).transpose(0, 2, 1, 3)
    q, k, v = heads(q), heads(k), heads(v)

    n_blocks = S // Q_BLOCK
    qb = q.reshape(B, N_HEADS, n_blocks, Q_BLOCK, HEAD_DIM).transpose(2, 0, 1, 3, 4)
    Fq = F.reshape(B, N_HEADS, n_blocks, Q_BLOCK).transpose(2, 0, 1, 3)
    kpos = jnp.arange(S)
    scale = HEAD_DIM ** -0.5

    def attend(args):
        q_blk, f_blk, blk = args
        qpos = blk * Q_BLOCK + jnp.arange(Q_BLOCK)
        s = jnp.einsum("bhqd,bhkd->bhqk", q_blk, k,
                       preferred_element_type=jnp.float32) * scale
        s = s + (f_blk[..., None] - F[:, :, None, :])
        s = jnp.where(kpos[None, :] <= qpos[:, None], s, MASK_VALUE)
        p = jax.nn.softmax(s, axis=-1)
        return jnp.einsum("bhqk,bhkd->bhqd", p.astype(v.dtype), v)

    o = lax.map(attend, (qb, Fq, jnp.arange(n_blocks)))
    o = o.transpose(1, 0, 3, 2, 4).reshape(B, S, D_MODEL)
    return o @ w_o


def sqrelu_mlp(h, w_up, w_down):
    return jnp.square(jax.nn.relu(h @ w_up)) @ w_down


def setup_inputs(seed: int = 0) -> dict:
    key = jax.random.key(seed)
    ks = jax.random.split(key, 20)
    f32 = jnp.float32
    D, H = D_MODEL, N_HEADS
    nrm = lambda k, shape, s: jax.random.normal(k, shape, f32) * s
    return {
        "x": jax.random.normal(ks[0], (BATCH, SEQ, D), f32),
        "g_mix_pre": 1.0 + nrm(ks[1], (DEPTH, D), 0.02),
        "g_mix_post": 1.0 + nrm(ks[2], (DEPTH, D), 0.02),
        "g_ffn_pre": 1.0 + nrm(ks[3], (DEPTH, D), 0.02),
        "g_ffn_post": 1.0 + nrm(ks[4], (DEPTH, D), 0.02),
        "conv_pw1_w": nrm(ks[5], (N_CONV_LAYERS, D, 2 * D), D ** -0.5),
        "conv_pw1_b": nrm(ks[6], (N_CONV_LAYERS, 2 * D), 0.02),
        "conv_dw_w": nrm(ks[7], (N_CONV_LAYERS, CONV_WIDTH, D), CONV_WIDTH ** -0.5),
        "conv_dw_b": nrm(ks[8], (N_CONV_LAYERS, D), 0.02),
        "conv_ln_g": 1.0 + nrm(ks[9], (N_CONV_LAYERS, D), 0.02),
        "conv_ln_b": nrm(ks[10], (N_CONV_LAYERS, D), 0.02),
        "conv_pw2_w": nrm(ks[11], (N_CONV_LAYERS, D, D), D ** -0.5),
        "conv_pw2_b": nrm(ks[12], (N_CONV_LAYERS, D), 0.02),
        "attn_w_in": nrm(ks[13], (N_ATTN_LAYERS, D, 3 * D + H), D ** -0.5),
        "attn_b_f": 1.0 + nrm(ks[14], (N_ATTN_LAYERS, H), 0.5),
        "attn_w_o": nrm(ks[15], (N_ATTN_LAYERS, D, D), D ** -0.5),
        "mlp_w_up": nrm(ks[16], (DEPTH, D, D_FF), D ** -0.5),
        "mlp_w_down": nrm(ks[17], (DEPTH, D_FF, D), D_FF ** -0.5),
    }


def reference(x, g_mix_pre, g_mix_post, g_ffn_pre, g_ffn_post,
              conv_pw1_w, conv_pw1_b, conv_dw_w, conv_dw_b, conv_ln_g, conv_ln_b,
              conv_pw2_w, conv_pw2_b, attn_w_in, attn_b_f, attn_w_o,
              mlp_w_up, mlp_w_down):
    for i in range(DEPTH):
        j = i // N_MIXERS
        h = rmsnorm(x, g_mix_pre[i])
        if i % N_MIXERS == 0:
            m = conformer_conv(h, conv_pw1_w[j], conv_pw1_b[j], conv_dw_w[j], conv_dw_b[j],
                               conv_ln_g[j], conv_ln_b[j], conv_pw2_w[j], conv_pw2_b[j])
        else:
            m = forgetting_attention(h, attn_w_in[j], attn_b_f[j], attn_w_o[j])
        x = x + rmsnorm(m, g_mix_post[i])
        h = rmsnorm(x, g_ffn_pre[i])
        x = x + rmsnorm(sqrelu_mlp(h, mlp_w_up[i], mlp_w_down[i]), g_ffn_post[i])
    return x
```

```python
import functools

import jax
import jax.numpy as jnp
from jax import lax
from jax.experimental import pallas as pl
from jax.experimental.pallas import tpu as pltpu

N_HEADS = 16
HEAD_DIM = 64
CONV_WIDTH = 31
RMS_EPS = 1e-6
LN_EPS = 1e-5
MASK_VALUE = -1e30

LANES = 128
CONV_HALO = 32
VMEM_LIMIT_BYTES = 56 * 1024 * 1024

F32 = jnp.float32
BF16 = jnp.bfloat16


def _rms(x, g):
    return x * lax.rsqrt(jnp.mean(x * x, axis=-1, keepdims=True) + RMS_EPS) * g


def _const_spec(shape):
    nd = len(shape)
    return pl.BlockSpec(shape, lambda *_: (0,) * nd, pipeline_mode=pl.Buffered(1))


def _conv_mixer_kernel(x_ref, gpre_ref, w1_ref, b1_ref, dww_ref, dwb_ref, lng_ref, lnb_ref,
                       w2_ref, b2_ref, gpost_ref, o_ref, ubuf_ref, *, tm, d):
    x = x_ref[0]
    h = _rms(x, gpre_ref[...])
    a = jnp.dot(h.astype(BF16), w1_ref[...], preferred_element_type=F32) + b1_ref[...]
    u = a[:, :d] * jax.nn.sigmoid(a[:, d:])

    @pl.when(pl.program_id(1) == 0)
    def _():
        ubuf_ref[0:CONV_HALO, :] = jnp.zeros((CONV_HALO, d), F32)

    ubuf_ref[CONV_HALO:CONV_HALO + tm, :] = u
    y = jnp.broadcast_to(dwb_ref[...], (tm, d))
    for k in range(CONV_WIDTH):
        y = y + ubuf_ref[pl.ds(CONV_HALO - (CONV_WIDTH - 1) + k, tm), :] * dww_ref[k:k + 1, :]
    ubuf_ref[0:CONV_HALO, :] = ubuf_ref[tm:tm + CONV_HALO, :]

    mu = jnp.mean(y, axis=-1, keepdims=True)
    yc = y - mu
    var = jnp.mean(yc * yc, axis=-1, keepdims=True)
    yn = yc * lax.rsqrt(var + LN_EPS) * lng_ref[...] + lnb_ref[...]
    act = yn * jax.nn.sigmoid(yn)
    m = jnp.dot(act.astype(BF16), w2_ref[...], preferred_element_type=F32) + b2_ref[...]
    o_ref[0] = x + _rms(m, gpost_ref[...])


def _conv_mixer(x, gpre, w1, b1, dww, dwb, lng, lnb, w2, b2, gpost, *, tm=512):
    b, s, d = x.shape
    row = lambda v: v.reshape(1, -1).astype(F32)
    kern = functools.partial(_conv_mixer_kernel, tm=tm, d=d)
    return pl.pallas_call(
        kern,
        grid=(b, s // tm),
        in_specs=[
            pl.BlockSpec((1, tm, d), lambda i, j: (i, j, 0)),
            _const_spec((1, d)), _const_spec((d, 2 * d)), _const_spec((1, 2 * d)),
            _const_spec((CONV_WIDTH, d)), _const_spec((1, d)), _const_spec((1, d)), _const_spec((1, d)),
            _const_spec((d, d)), _const_spec((1, d)), _const_spec((1, d)),
        ],
        out_specs=pl.BlockSpec((1, tm, d), lambda i, j: (i, j, 0)),
        out_shape=jax.ShapeDtypeStruct(x.shape, F32),
        scratch_shapes=[pltpu.VMEM((CONV_HALO + tm, d), F32)],
        compiler_params=pltpu.CompilerParams(
            dimension_semantics=("arbitrary", "arbitrary"), vmem_limit_bytes=VMEM_LIMIT_BYTES),
        name="conv_mixer",
    )(x, row(gpre), w1.astype(BF16), row(b1), dww.astype(F32), row(dwb), row(lng), row(lnb),
      w2.astype(BF16), row(b2), row(gpost))


def _mlp_kernel(x_ref, gpre_ref, wup_ref, wdown_ref, gpost_ref, o_ref):
    x = x_ref[...]
    h = _rms(x, gpre_ref[...])
    up = jnp.dot(h.astype(BF16), wup_ref[...], preferred_element_type=F32)
    r = jnp.maximum(up, 0.0)
    act = (r * r).astype(BF16)
    down = jnp.dot(act, wdown_ref[...], preferred_element_type=F32)
    o_ref[...] = x + _rms(down, gpost_ref[...])


def _mlp(x2d, gpre, wup, wdown, gpost, *, tm=512):
    t, d = x2d.shape
    ff = wup.shape[1]
    row = lambda v: v.reshape(1, -1).astype(F32)
    return pl.pallas_call(
        _mlp_kernel,
        grid=(t // tm,),
        in_specs=[
            pl.BlockSpec((tm, d), lambda i: (i, 0)),
            _const_spec((1, d)), _const_spec((d, ff)), _const_spec((ff, d)), _const_spec((1, d)),
        ],
        out_specs=pl.BlockSpec((tm, d), lambda i: (i, 0)),
        out_shape=jax.ShapeDtypeStruct(x2d.shape, F32),
        compiler_params=pltpu.CompilerParams(
            dimension_semantics=("arbitrary",), vmem_limit_bytes=VMEM_LIMIT_BYTES),
        name="sqrelu_mlp",
    )(x2d, row(gpre), wup.astype(BF16), wdown.astype(BF16), row(gpost))


def _split3_bf16(x):
    hi = x.astype(BF16)
    r1 = x - hi.astype(F32)
    mid = r1.astype(BF16)
    lo = (r1 - mid.astype(F32)).astype(BF16)
    return hi, mid, lo


def _qkv_kernel(x_ref, gpre_ref, wqkv_ref, wf_ref, bf_ref, q_ref, k_ref, v_ref, ft_ref, carry_ref, *, tm, d):
    x = x_ref[0]
    hb = _rms(x, gpre_ref[...]).astype(BF16)
    proj = jnp.dot(hb, wqkv_ref[...], preferred_element_type=F32)
    q_ref[0] = (proj[:, :d] * (HEAD_DIM ** -0.5)).astype(BF16)
    k_ref[0] = proj[:, d:2 * d].astype(BF16)
    v_ref[0] = proj[:, 2 * d:].astype(BF16)

    f_logit = jnp.dot(hb, wf_ref[...], preferred_element_type=F32) + bf_ref[...]
    log_f = jax.nn.log_sigmoid(f_logit)

    @pl.when(pl.program_id(1) == 0)
    def _():
        carry_ref[...] = jnp.zeros_like(carry_ref)

    rows = lax.broadcasted_iota(jnp.int32, (tm, tm), 0)
    cols = lax.broadcasted_iota(jnp.int32, (tm, tm), 1)
    tri = (rows >= cols).astype(BF16)
    hi, mid, lo = _split3_bf16(log_f)
    csum = (jnp.dot(tri, hi, preferred_element_type=F32) + jnp.dot(tri, mid, preferred_element_type=F32)
            + jnp.dot(tri, lo, preferred_element_type=F32))
    f_cum = csum + carry_ref[...]
    carry_ref[...] = f_cum[tm - 1:tm, :]
    ft_ref[0] = f_cum.T[:N_HEADS, :]


def _qkv(x, gpre, w_in, b_f, *, tm=512):
    b, s, d = x.shape
    wqkv = w_in[:, :3 * d].astype(BF16)
    wf = jnp.zeros((d, LANES), F32).at[:, :N_HEADS].set(w_in[:, 3 * d:]).astype(BF16)
    bf = jnp.zeros((1, LANES), F32).at[0, :N_HEADS].set(b_f.astype(F32))
    kern = functools.partial(_qkv_kernel, tm=tm, d=d)
    act_spec = pl.BlockSpec((1, tm, d), lambda i, j: (i, j, 0))
    return pl.pallas_call(
        kern,
        grid=(b, s // tm),
        in_specs=[act_spec, _const_spec((1, d)), _const_spec((d, 3 * d)), _const_spec((d, LANES)),
                  _const_spec((1, LANES))],
        out_specs=[act_spec, act_spec, act_spec, pl.BlockSpec((1, N_HEADS, tm), lambda i, j: (i, 0, j))],
        out_shape=[jax.ShapeDtypeStruct((b, s, d), BF16)] * 3 + [jax.ShapeDtypeStruct((b, N_HEADS, s), F32)],
        scratch_shapes=[pltpu.VMEM((1, LANES), F32)],
        compiler_params=pltpu.CompilerParams(
            dimension_semantics=("arbitrary", "arbitrary"), vmem_limit_bytes=VMEM_LIMIT_BYTES),
        name="attn_qkv",
    )(x, gpre.reshape(1, -1).astype(F32), wqkv, wf, bf)


def _attn_kernel(q_ref, k_ref, v_ref, f_ref, o_ref, *, s, tq):
    tk = tq
    lane = lax.broadcasted_iota(jnp.int32, (1, LANES), 1)
    head_lanes = [lane < HEAD_DIM, lane >= HEAD_DIM]
    row_ids = lax.broadcasted_iota(jnp.int32, (tq, tk), 0)
    col_ids = lax.broadcasted_iota(jnp.int32, (tq, tk), 1)
    causal = col_ids <= row_ids

    def q_block(qi, _):
        q0 = pl.multiple_of(qi * tq, tq)
        qp = q_ref[0, pl.ds(q0, tq), :]
        outs = []
        for hh in range(2):
            def kv_step(k0, carry, masked):
                m, l, acc = carry
                kp = k_ref[0, pl.ds(k0, tk), :]
                kh = jnp.where(head_lanes[hh], kp, jnp.zeros_like(kp))
                sc = lax.dot_general(qp, kh, (((1,), (1,)), ((), ())), preferred_element_type=F32)
                sc = sc - f_ref[0, 0, hh:hh + 1, pl.ds(k0, tk)]
                if masked:
                    sc = jnp.where(causal, sc, MASK_VALUE)
                m_new = jnp.maximum(m, jnp.max(sc, axis=1, keepdims=True))
                alpha = jnp.exp(m - m_new)
                p = jnp.exp(sc - m_new)
                l_new = alpha * l + jnp.sum(p, axis=1, keepdims=True)
                pv = jnp.dot(p.astype(BF16), v_ref[0, pl.ds(k0, tk), :], preferred_element_type=F32)
                return m_new, l_new, alpha * acc + pv

            init = (jnp.full((tq, 1), MASK_VALUE, F32), jnp.zeros((tq, 1), F32), jnp.zeros((tq, LANES), F32))
            carry = lax.fori_loop(
                0, qi, lambda kj, c: kv_step(pl.multiple_of(kj * tk, tk), c, False), init)
            m, l, acc = kv_step(q0, carry, True)
            outs.append(acc / l)
        o_ref[0, pl.ds(q0, tq), :] = jnp.where(head_lanes[0], outs[0], outs[1]).astype(o_ref.dtype)
        return 0

    lax.fori_loop(0, s // tq, q_block, 0)


def _attention(q, k, v, ft, *, tq=256):
    b, s, d = q.shape
    n_pairs = d // LANES
    heads_per_group = LANES // HEAD_DIM
    f4 = ft.reshape(b, n_pairs, heads_per_group, s)
    kern = functools.partial(_attn_kernel, s=s, tq=tq)
    col_spec = pl.BlockSpec((1, s, LANES), lambda i, j: (i, 0, j))
    return pl.pallas_call(
        kern,
        grid=(b, n_pairs),
        in_specs=[col_spec, col_spec, col_spec,
                  pl.BlockSpec((1, 1, heads_per_group, s), lambda i, j: (i, j, 0, 0))],
        out_specs=col_spec,
        out_shape=jax.ShapeDtypeStruct((b, s, d), BF16),
        compiler_params=pltpu.CompilerParams(
            dimension_semantics=("arbitrary", "arbitrary"), vmem_limit_bytes=VMEM_LIMIT_BYTES),
        name="fox_attention",
    )(q, k, v, f4)


def _out_proj_kernel(x_ref, o_ref, wo_ref, gpost_ref, y_ref):
    m = jnp.dot(o_ref[...], wo_ref[...], preferred_element_type=F32)
    y_ref[...] = x_ref[...] + _rms(m, gpost_ref[...])


def _out_proj(x2d, o2d, wo, gpost, *, tm=512):
    t, d = x2d.shape
    spec = pl.BlockSpec((tm, d), lambda i: (i, 0))
    return pl.pallas_call(
        _out_proj_kernel,
        grid=(t // tm,),
        in_specs=[spec, spec, _const_spec((d, d)), _const_spec((1, d))],
        out_specs=spec,
        out_shape=jax.ShapeDtypeStruct(x2d.shape, F32),
        compiler_params=pltpu.CompilerParams(
            dimension_semantics=("arbitrary",), vmem_limit_bytes=VMEM_LIMIT_BYTES),
        name="attn_out_proj",
    )(x2d, o2d, wo.astype(BF16), gpost.reshape(1, -1).astype(F32))


def kernel(x, g_mix_pre, g_mix_post, g_ffn_pre, g_ffn_post, conv_pw1_w, conv_pw1_b, conv_dw_w, conv_dw_b,
           conv_ln_g, conv_ln_b, conv_pw2_w, conv_pw2_b, attn_w_in, attn_b_f, attn_w_o, mlp_w_up, mlp_w_down):
    b, s, d = x.shape
    depth = g_mix_pre.shape[0]
    for i in range(depth):
        j = i // 2
        if i % 2 == 0:
            x = _conv_mixer(x, g_mix_pre[i], conv_pw1_w[j], conv_pw1_b[j], conv_dw_w[j], conv_dw_b[j],
                            conv_ln_g[j], conv_ln_b[j], conv_pw2_w[j], conv_pw2_b[j], g_mix_post[i])
        else:
            q, k, v, ft = _qkv(x, g_mix_pre[i], attn_w_in[j], attn_b_f[j])
            o = _attention(q, k, v, ft)
            x = _out_proj(x.reshape(b * s, d), o.reshape(b * s, d), attn_w_o[j], g_mix_post[i]).reshape(b, s, d)
        x = _mlp(x.reshape(b * s, d), g_ffn_pre[i], mlp_w_up[i], mlp_w_down[i], g_ffn_post[i]).reshape(b, s, d)
    return x
```

```python
import functools

import jax
import jax.numpy as jnp
from jax import lax
from jax.experimental import pallas as pl
from jax.experimental.pallas import tpu as pltpu

N_HEADS = 16
HEAD_DIM = 64
CONV_WIDTH = 31
RMS_EPS = 1e-6
LN_EPS = 1e-5
MASK_VALUE = -1e30

LANES = 128
CONV_HALO = 32
VMEM_LIMIT_BYTES = 56 * 1024 * 1024
EXP_UNDERFLOW_LOGIT = 110.0
NORM_SLACK = 1.01

F32 = jnp.float32
BF16 = jnp.bfloat16


def _rms(x, g):
    return x * lax.rsqrt(jnp.mean(x * x, axis=-1, keepdims=True) + RMS_EPS) * g


def _const_spec(shape):
    nd = len(shape)
    return pl.BlockSpec(shape, lambda *_: (0,) * nd, pipeline_mode=pl.Buffered(1))


def _conv_mixer_kernel(x_ref, gpre_ref, w1_ref, b1_ref, dww_ref, dwb_ref, lng_ref, lnb_ref,
                       w2_ref, b2_ref, gpost_ref, o_ref, ubuf_ref, *, tm, d):
    x = x_ref[0]
    h = _rms(x, gpre_ref[...])
    a = jnp.dot(h.astype(BF16), w1_ref[...], preferred_element_type=F32) + b1_ref[...]
    u = a[:, :d] * jax.nn.sigmoid(a[:, d:])

    @pl.when(pl.program_id(1) == 0)
    def _():
        ubuf_ref[0:CONV_HALO, :] = jnp.zeros((CONV_HALO, d), F32)

    ubuf_ref[CONV_HALO:CONV_HALO + tm, :] = u
    y = jnp.broadcast_to(dwb_ref[...], (tm, d))
    for k in range(CONV_WIDTH):
        y = y + ubuf_ref[pl.ds(CONV_HALO - (CONV_WIDTH - 1) + k, tm), :] * dww_ref[k:k + 1, :]
    ubuf_ref[0:CONV_HALO, :] = ubuf_ref[tm:tm + CONV_HALO, :]

    mu = jnp.mean(y, axis=-1, keepdims=True)
    yc = y - mu
    var = jnp.mean(yc * yc, axis=-1, keepdims=True)
    yn = yc * lax.rsqrt(var + LN_EPS) * lng_ref[...] + lnb_ref[...]
    act = yn * jax.nn.sigmoid(yn)
    m = jnp.dot(act.astype(BF16), w2_ref[...], preferred_element_type=F32) + b2_ref[...]
    o_ref[0] = x + _rms(m, gpost_ref[...])


def _conv_mixer(x, gpre, w1, b1, dww, dwb, lng, lnb, w2, b2, gpost, *, tm=512):
    b, s, d = x.shape
    row = lambda v: v.reshape(1, -1).astype(F32)
    kern = functools.partial(_conv_mixer_kernel, tm=tm, d=d)
    return pl.pallas_call(
        kern,
        grid=(b, s // tm),
        in_specs=[
            pl.BlockSpec((1, tm, d), lambda i, j: (i, j, 0)),
            _const_spec((1, d)), _const_spec((d, 2 * d)), _const_spec((1, 2 * d)),
            _const_spec((CONV_WIDTH, d)), _const_spec((1, d)), _const_spec((1, d)), _const_spec((1, d)),
            _const_spec((d, d)), _const_spec((1, d)), _const_spec((1, d)),
        ],
        out_specs=pl.BlockSpec((1, tm, d), lambda i, j: (i, j, 0)),
        out_shape=jax.ShapeDtypeStruct(x.shape, F32),
        scratch_shapes=[pltpu.VMEM((CONV_HALO + tm, d), F32)],
        compiler_params=pltpu.CompilerParams(
            dimension_semantics=("arbitrary", "arbitrary"), vmem_limit_bytes=VMEM_LIMIT_BYTES),
        name="conv_mixer",
    )(x, row(gpre), w1.astype(BF16), row(b1), dww.astype(F32), row(dwb), row(lng), row(lnb),
      w2.astype(BF16), row(b2), row(gpost))


def _mlp_kernel(x_ref, gpre_ref, wup_ref, wdown_ref, gpost_ref, o_ref):
    x = x_ref[...]
    h = _rms(x, gpre_ref[...])
    up = jnp.dot(h.astype(BF16), wup_ref[...], preferred_element_type=F32)
    r = jnp.maximum(up, 0.0)
    act = (r * r).astype(BF16)
    down = jnp.dot(act, wdown_ref[...], preferred_element_type=F32)
    o_ref[...] = x + _rms(down, gpost_ref[...])


def _mlp(x2d, gpre, wup, wdown, gpost, *, tm=512):
    t, d = x2d.shape
    ff = wup.shape[1]
    row = lambda v: v.reshape(1, -1).astype(F32)
    return pl.pallas_call(
        _mlp_kernel,
        grid=(t // tm,),
        in_specs=[
            pl.BlockSpec((tm, d), lambda i: (i, 0)),
            _const_spec((1, d)), _const_spec((d, ff)), _const_spec((ff, d)), _const_spec((1, d)),
        ],
        out_specs=pl.BlockSpec((tm, d), lambda i: (i, 0)),
        out_shape=jax.ShapeDtypeStruct(x2d.shape, F32),
        compiler_params=pltpu.CompilerParams(
            dimension_semantics=("arbitrary",), vmem_limit_bytes=VMEM_LIMIT_BYTES),
        name="sqrelu_mlp",
    )(x2d, row(gpre), wup.astype(BF16), wdown.astype(BF16), row(gpost))


def _split3_bf16(x):
    hi = x.astype(BF16)
    r1 = x - hi.astype(F32)
    mid = r1.astype(BF16)
    lo = (r1 - mid.astype(F32)).astype(BF16)
    return hi, mid, lo


def _head_sq_norm_bound(xb, ind):
    xf = xb.astype(F32)
    nsq = jnp.dot((xf * xf).astype(BF16), ind, preferred_element_type=F32)
    return nsq * NORM_SLACK


def _qkv_kernel(x_ref, gpre_ref, wqkv_ref, wf_ref, bf_ref, ind_ref, q_ref, k_ref, v_ref, st_ref, carry_ref, *, tm, d):
    x = x_ref[0]
    hb = _rms(x, gpre_ref[...]).astype(BF16)
    proj = jnp.dot(hb, wqkv_ref[...], preferred_element_type=F32)
    qb = (proj[:, :d] * (HEAD_DIM ** -0.5)).astype(BF16)
    kb = proj[:, d:2 * d].astype(BF16)
    q_ref[0] = qb
    k_ref[0] = kb
    v_ref[0] = proj[:, 2 * d:].astype(BF16)
    st_ref[0, 1] = _head_sq_norm_bound(qb, ind_ref[...]).T[:N_HEADS, :]
    st_ref[0, 2] = _head_sq_norm_bound(kb, ind_ref[...]).T[:N_HEADS, :]

    f_logit = jnp.dot(hb, wf_ref[...], preferred_element_type=F32) + bf_ref[...]
    log_f = jax.nn.log_sigmoid(f_logit)

    @pl.when(pl.program_id(1) == 0)
    def _():
        carry_ref[...] = jnp.zeros_like(carry_ref)

    rows = lax.broadcasted_iota(jnp.int32, (tm, tm), 0)
    cols = lax.broadcasted_iota(jnp.int32, (tm, tm), 1)
    tri = (rows >= cols).astype(BF16)
    hi, mid, lo = _split3_bf16(log_f)
    csum = (jnp.dot(tri, hi, preferred_element_type=F32) + jnp.dot(tri, mid, preferred_element_type=F32)
            + jnp.dot(tri, lo, preferred_element_type=F32))
    f_cum = csum + carry_ref[...]
    carry_ref[...] = f_cum[tm - 1:tm, :]
    st_ref[0, 0] = f_cum.T[:N_HEADS, :]


def _qkv(x, gpre, w_in, b_f, *, tm=512):
    b, s, d = x.shape
    wqkv = w_in[:, :3 * d].astype(BF16)
    wf = jnp.zeros((d, LANES), F32).at[:, :N_HEADS].set(w_in[:, 3 * d:]).astype(BF16)
    bf = jnp.zeros((1, LANES), F32).at[0, :N_HEADS].set(b_f.astype(F32))
    ind = (jnp.arange(d)[:, None] // HEAD_DIM == jnp.arange(LANES)[None, :]).astype(BF16)
    kern = functools.partial(_qkv_kernel, tm=tm, d=d)
    act_spec = pl.BlockSpec((1, tm, d), lambda i, j: (i, j, 0))
    return pl.pallas_call(
        kern,
        grid=(b, s // tm),
        in_specs=[act_spec, _const_spec((1, d)), _const_spec((d, 3 * d)), _const_spec((d, LANES)),
                  _const_spec((1, LANES)), _const_spec((d, LANES))],
        out_specs=[act_spec, act_spec, act_spec, pl.BlockSpec((1, 3, N_HEADS, tm), lambda i, j: (i, 0, 0, j))],
        out_shape=[jax.ShapeDtypeStruct((b, s, d), BF16)] * 3 + [jax.ShapeDtypeStruct((b, 3, N_HEADS, s), F32)],
        scratch_shapes=[pltpu.VMEM((1, LANES), F32)],
        compiler_params=pltpu.CompilerParams(
            dimension_semantics=("arbitrary", "arbitrary"), vmem_limit_bytes=VMEM_LIMIT_BYTES),
        name="attn_qkv",
    )(x, gpre.reshape(1, -1).astype(F32), wqkv, wf, bf, ind)


def _attn_kernel(q_ref, k_ref, v_ref, st_ref, o_ref, g_ref, *, s, tq):
    tk = tq
    n_heads = LANES // HEAD_DIM
    lane = lax.broadcasted_iota(jnp.int32, (1, LANES), 1)
    head_lanes = [(lane >= hh * HEAD_DIM) & (lane < (hh + 1) * HEAD_DIM) for hh in range(n_heads)]
    row_ids = lax.broadcasted_iota(jnp.int32, (tq, tk), 0)
    col_ids = lax.broadcasted_iota(jnp.int32, (tq, tk), 1)
    causal = col_ids <= row_ids
    pos = lax.broadcasted_iota(jnp.int32, (1, s), 1)

    nk = st_ref[0, 2, 0]
    kmax = jnp.max(nk, axis=1, keepdims=True)
    g_ref[...] = jnp.sqrt(st_ref[0, 1, 0]) * (jnp.sqrt(kmax) + jnp.sqrt(nk)) + st_ref[0, 0, 0]

    def kv_step(qp, k0, carry, masked):
        kp = k_ref[0, pl.ds(k0, tk), :]
        vp = v_ref[0, pl.ds(k0, tk), :]
        new = []
        for hh in range(n_heads):
            m, l, acc = carry[hh]
            kh = jnp.where(head_lanes[hh], kp, jnp.zeros_like(kp))
            sc = lax.dot_general(qp, kh, (((1,), (1,)), ((), ())), preferred_element_type=F32)
            sc = sc - st_ref[0, 0, 0, hh:hh + 1, pl.ds(k0, tk)]
            if masked:
                sc = jnp.where(causal, sc, MASK_VALUE)
            m_new = jnp.maximum(m, jnp.max(sc, axis=1, keepdims=True))
            alpha = jnp.exp(m - m_new)
            p = jnp.exp(sc - m_new)
            l_new = alpha * l + jnp.sum(p, axis=1, keepdims=True)
            pv = jnp.dot(p.astype(BF16), vp, preferred_element_type=F32)
            new.append((m_new, l_new, alpha * acc + pv))
        return tuple(new)

    def q_block(qi, _):
        q0 = pl.multiple_of(qi * tq, tq)
        qp = q_ref[0, pl.ds(q0, tq), :]

        gmax = jnp.max(g_ref[:, pl.ds(q0, tq)], axis=1, keepdims=True)
        live = (gmax - st_ref[0, 0, 0] >= -EXP_UNDERFLOW_LOGIT) & (pos < q0)
        n_live = jnp.max(jnp.sum(live.astype(F32), axis=1, keepdims=True), axis=0, keepdims=True)
        n_back = jnp.ceil(n_live * (1.0 / tk)).astype(jnp.int32)[0, 0]

        init = tuple((jnp.full((tq, 1), MASK_VALUE, F32), jnp.zeros((tq, 1), F32), jnp.zeros((tq, LANES), F32))
                     for _ in range(n_heads))
        carry = lax.fori_loop(
            qi - n_back, qi, lambda kj, c: kv_step(qp, pl.multiple_of(kj * tk, tk), c, False), init)
        carry = kv_step(qp, q0, carry, True)
        out = jnp.zeros((tq, LANES), F32)
        for hh in range(n_heads):
            m, l, acc = carry[hh]
            out = jnp.where(head_lanes[hh], acc / l, out)
        o_ref[0, pl.ds(q0, tq), :] = out.astype(o_ref.dtype)
        return 0

    lax.fori_loop(0, s // tq, q_block, 0)


def _attention(q, k, v, stats, *, tq=512):
    b, s, d = q.shape
    n_groups = d // LANES
    heads_per_group = LANES // HEAD_DIM
    st5 = stats.reshape(b, 3, n_groups, heads_per_group, s)
    kern = functools.partial(_attn_kernel, s=s, tq=tq)
    col_spec = pl.BlockSpec((1, s, LANES), lambda i, j: (i, 0, j))
    return pl.pallas_call(
        kern,
        grid=(b, n_groups),
        in_specs=[col_spec, col_spec, col_spec,
                  pl.BlockSpec((1, 3, 1, heads_per_group, s), lambda i, j: (i, 0, j, 0, 0))],
        out_specs=col_spec,
        out_shape=jax.ShapeDtypeStruct((b, s, d), BF16),
        scratch_shapes=[pltpu.VMEM((heads_per_group, s), F32)],
        compiler_params=pltpu.CompilerParams(
            dimension_semantics=("arbitrary", "arbitrary"), vmem_limit_bytes=VMEM_LIMIT_BYTES),
        name="fox_attention",
    )(q, k, v, st5)


def _out_proj_kernel(x_ref, o_ref, wo_ref, gpost_ref, y_ref):
    m = jnp.dot(o_ref[...], wo_ref[...], preferred_element_type=F32)
    y_ref[...] = x_ref[...] + _rms(m, gpost_ref[...])


def _out_proj(x2d, o2d, wo, gpost, *, tm=512):
    t, d = x2d.shape
    spec = pl.BlockSpec((tm, d), lambda i: (i, 0))
    return pl.pallas_call(
        _out_proj_kernel,
        grid=(t // tm,),
        in_specs=[spec, spec, _const_spec((d, d)), _const_spec((1, d))],
        out_specs=spec,
        out_shape=jax.ShapeDtypeStruct(x2d.shape, F32),
        compiler_params=pltpu.CompilerParams(
            dimension_semantics=("arbitrary",), vmem_limit_bytes=VMEM_LIMIT_BYTES),
        name="attn_out_proj",
    )(x2d, o2d, wo.astype(BF16), gpost.reshape(1, -1).astype(F32))


def kernel(x, g_mix_pre, g_mix_post, g_ffn_pre, g_ffn_post, conv_pw1_w, conv_pw1_b, conv_dw_w, conv_dw_b,
           conv_ln_g, conv_ln_b, conv_pw2_w, conv_pw2_b, attn_w_in, attn_b_f, attn_w_o, mlp_w_up, mlp_w_down):
    b, s, d = x.shape
    depth = g_mix_pre.shape[0]
    for i in range(depth):
        j = i // 2
        if i % 2 == 0:
            x = _conv_mixer(x, g_mix_pre[i], conv_pw1_w[j], conv_pw1_b[j], conv_dw_w[j], conv_dw_b[j],
                            conv_ln_g[j], conv_ln_b[j], conv_pw2_w[j], conv_pw2_b[j], g_mix_post[i])
        else:
            q, k, v, stats = _qkv(x, g_mix_pre[i], attn_w_in[j], attn_b_f[j])
            o = _attention(q, k, v, stats)
            x = _out_proj(x.reshape(b * s, d), o.reshape(b * s, d), attn_w_o[j], g_mix_post[i]).reshape(b, s, d)
        x = _mlp(x.reshape(b * s, d), g_ffn_pre[i], mlp_w_up[i], mlp_w_down[i], g_ffn_post[i]).reshape(b, s, d)
    return x
```

```python
import functools

import jax
import jax.numpy as jnp
from jax import lax
from jax.experimental import pallas as pl
from jax.experimental.pallas import tpu as pltpu

N_HEADS = 16
HEAD_DIM = 64
CONV_WIDTH = 31
RMS_EPS = 1e-6
LN_EPS = 1e-5
MASK_VALUE = -1e30

LANES = 128
SUBLANES = 8
CONV_ROWS = 64
CONV_HALO = 32
VMEM_LIMIT_BYTES = 56 * 1024 * 1024
EXP2_UNDERFLOW = 160.0
LOG2E = 1.4426950408889634
NORM_SLACK = 1.01

F32 = jnp.float32
BF16 = jnp.bfloat16


def _rms(x, g):
    return x * lax.rsqrt(jnp.mean(x * x, axis=-1, keepdims=True) + RMS_EPS) * g


def _const_spec(shape):
    nd = len(shape)
    return pl.BlockSpec(shape, lambda *_: (0,) * nd, pipeline_mode=pl.Buffered(1))


def _conv_mixer_kernel(x_ref, gpre_ref, w1_ref, b1_ref, dww_ref, dwb_ref, lng_ref, lnb_ref,
                       w2_ref, b2_ref, gpost_ref, o_ref, ubuf_ref, ybuf_ref, *, tm, d):
    x = x_ref[0]
    h = _rms(x, gpre_ref[...])
    a = jnp.dot(h.astype(BF16), w1_ref[...], preferred_element_type=F32) + b1_ref[...]
    u = a[:, :d] * jax.nn.sigmoid(a[:, d:])

    @pl.when(pl.program_id(1) == 0)
    def _():
        ubuf_ref[0:CONV_HALO, :] = jnp.zeros((CONV_HALO, d), F32)
        ubuf_ref[CONV_HALO + tm:CONV_HALO + tm + SUBLANES, :] = jnp.zeros((SUBLANES, d), F32)

    ubuf_ref[CONV_HALO:CONV_HALO + tm, :] = u

    def conv_chunk(i, _):
        r0 = pl.multiple_of(i * CONV_ROWS, CONV_ROWS)

        def lane_chunk(c, _):
            c0 = pl.multiple_of(c * LANES, LANES)
            acc = jnp.broadcast_to(dwb_ref[:, pl.ds(c0, LANES)], (CONV_ROWS, LANES))
            for r in range(SUBLANES):
                z = None
                for o in range(r if r >= 2 else r + SUBLANES, CONV_WIDTH + 2, SUBLANES):
                    rows = pl.ds(pl.multiple_of(r0 + (o - r), SUBLANES), CONV_ROWS + SUBLANES)
                    term = ubuf_ref[rows, pl.ds(c0, LANES)] * dww_ref[o - 2:o - 1, pl.ds(c0, LANES)]
                    z = term if z is None else z + term
                acc = acc + z[r:r + CONV_ROWS]
            ybuf_ref[pl.ds(r0, CONV_ROWS), pl.ds(c0, LANES)] = acc
            return 0

        return lax.fori_loop(0, d // LANES, lane_chunk, 0)

    lax.fori_loop(0, tm // CONV_ROWS, conv_chunk, 0)
    ubuf_ref[0:CONV_HALO, :] = ubuf_ref[tm:tm + CONV_HALO, :]

    y = ybuf_ref[...]
    mu = jnp.mean(y, axis=-1, keepdims=True)
    yc = y - mu
    var = jnp.mean(yc * yc, axis=-1, keepdims=True)
    yn = yc * lax.rsqrt(var + LN_EPS) * lng_ref[...] + lnb_ref[...]
    act = yn * jax.nn.sigmoid(yn)
    m = jnp.dot(act.astype(BF16), w2_ref[...], preferred_element_type=F32) + b2_ref[...]
    o_ref[0] = x + _rms(m, gpost_ref[...])


def _conv_mixer(x, gpre, w1, b1, dww, dwb, lng, lnb, w2, b2, gpost, *, tm=512):
    b, s, d = x.shape
    row = lambda v: v.reshape(1, -1).astype(F32)
    kern = functools.partial(_conv_mixer_kernel, tm=tm, d=d)
    return pl.pallas_call(
        kern,
        grid=(b, s // tm),
        in_specs=[
            pl.BlockSpec((1, tm, d), lambda i, j: (i, j, 0)),
            _const_spec((1, d)), _const_spec((d, 2 * d)), _const_spec((1, 2 * d)),
            _const_spec((CONV_WIDTH, d)), _const_spec((1, d)), _const_spec((1, d)), _const_spec((1, d)),
            _const_spec((d, d)), _const_spec((1, d)), _const_spec((1, d)),
        ],
        out_specs=pl.BlockSpec((1, tm, d), lambda i, j: (i, j, 0)),
        out_shape=jax.ShapeDtypeStruct(x.shape, F32),
        scratch_shapes=[pltpu.VMEM((CONV_HALO + tm + SUBLANES, d), F32), pltpu.VMEM((tm, d), F32)],
        compiler_params=pltpu.CompilerParams(
            dimension_semantics=("arbitrary", "arbitrary"), vmem_limit_bytes=VMEM_LIMIT_BYTES),
        name="conv_mixer",
    )(x, row(gpre), w1.astype(BF16), row(b1), dww.astype(F32), row(dwb), row(lng), row(lnb),
      w2.astype(BF16), row(b2), row(gpost))


def _mlp_kernel(x_ref, gpre_ref, wup_ref, wdown_ref, gpost_ref, o_ref):
    x = x_ref[...]
    h = _rms(x, gpre_ref[...])
    up = jnp.dot(h.astype(BF16), wup_ref[...], preferred_element_type=F32)
    r = jnp.maximum(up, 0.0)
    act = (r * r).astype(BF16)
    down = jnp.dot(act, wdown_ref[...], preferred_element_type=F32)
    o_ref[...] = x + _rms(down, gpost_ref[...])


def _mlp(x2d, gpre, wup, wdown, gpost, *, tm=512):
    t, d = x2d.shape
    ff = wup.shape[1]
    row = lambda v: v.reshape(1, -1).astype(F32)
    return pl.pallas_call(
        _mlp_kernel,
        grid=(t // tm,),
        in_specs=[
            pl.BlockSpec((tm, d), lambda i: (i, 0)),
            _const_spec((1, d)), _const_spec((d, ff)), _const_spec((ff, d)), _const_spec((1, d)),
        ],
        out_specs=pl.BlockSpec((tm, d), lambda i: (i, 0)),
        out_shape=jax.ShapeDtypeStruct(x2d.shape, F32),
        compiler_params=pltpu.CompilerParams(
            dimension_semantics=("arbitrary",), vmem_limit_bytes=VMEM_LIMIT_BYTES),
        name="sqrelu_mlp",
    )(x2d, row(gpre), wup.astype(BF16), wdown.astype(BF16), row(gpost))


def _split3_bf16(x):
    hi = x.astype(BF16)
    r1 = x - hi.astype(F32)
    mid = r1.astype(BF16)
    lo = (r1 - mid.astype(F32)).astype(BF16)
    return hi, mid, lo


def _head_sq_norm_bound(xb, ind):
    xf = xb.astype(F32)
    nsq = jnp.dot((xf * xf).astype(BF16), ind, preferred_element_type=F32)
    return nsq * NORM_SLACK


def _qkv_kernel(x_ref, gpre_ref, wqkv_ref, wf_ref, bf_ref, ind_ref, sel_ref, q_ref, k_ref, ka_ref, v_ref, st_ref,
                carry_ref, *, tm, d):
    x = x_ref[0]
    hb = _rms(x, gpre_ref[...]).astype(BF16)
    proj = jnp.dot(hb, wqkv_ref[...], preferred_element_type=F32)
    qb = (proj[:, :d] * (HEAD_DIM ** -0.5 * LOG2E)).astype(BF16)
    kb = proj[:, d:2 * d].astype(BF16)
    q_ref[0] = qb
    k_ref[0] = kb
    v_ref[0] = proj[:, 2 * d:].astype(BF16)
    st_ref[0, 1] = _head_sq_norm_bound(qb, ind_ref[...]).T[:N_HEADS, :]
    st_ref[0, 2] = _head_sq_norm_bound(kb, ind_ref[...]).T[:N_HEADS, :]

    f_logit = jnp.dot(hb, wf_ref[...], preferred_element_type=F32) + bf_ref[...]
    log_f = jax.nn.log_sigmoid(f_logit)

    @pl.when(pl.program_id(1) == 0)
    def _():
        carry_ref[...] = jnp.zeros_like(carry_ref)

    rows = lax.broadcasted_iota(jnp.int32, (tm, tm), 0)
    cols = lax.broadcasted_iota(jnp.int32, (tm, tm), 1)
    tri = (rows >= cols).astype(BF16)
    hi, mid, lo = _split3_bf16(log_f)
    csum = (jnp.dot(tri, hi, preferred_element_type=F32) + jnp.dot(tri, mid, preferred_element_type=F32)
            + jnp.dot(tri, lo, preferred_element_type=F32))
    f_cum = csum + carry_ref[...]
    carry_ref[...] = f_cum[tm - 1:tm, :]
    f2 = f_cum * LOG2E
    st_ref[0, 0] = f2.T[:N_HEADS, :]
    f_hi, f_mid, f_lo = _split3_bf16(f2)
    ka = (jnp.dot(f_hi, sel_ref[0], preferred_element_type=F32) + jnp.dot(f_mid, sel_ref[1], preferred_element_type=F32)
          + jnp.dot(f_lo, sel_ref[2], preferred_element_type=F32))
    ka_ref[0] = ka.astype(BF16)


def _qkv(x, gpre, w_in, b_f, *, tm=512):
    b, s, d = x.shape
    wqkv = w_in[:, :3 * d].astype(BF16)
    wf = jnp.zeros((d, LANES), F32).at[:, :N_HEADS].set(w_in[:, 3 * d:]).astype(BF16)
    bf = jnp.zeros((1, LANES), F32).at[0, :N_HEADS].set(b_f.astype(F32))
    ind = (jnp.arange(d)[:, None] // HEAD_DIM == jnp.arange(LANES)[None, :]).astype(BF16)
    heads = jnp.arange(LANES)[:, None]
    dest = (heads // 2) * LANES + (1 - heads % 2) * HEAD_DIM
    cols = jnp.arange(d)[None, :]
    sel = jnp.stack([-((cols == dest + t) & (heads < N_HEADS)).astype(F32) for t in range(3)]).astype(BF16)
    kern = functools.partial(_qkv_kernel, tm=tm, d=d)
    act_spec = pl.BlockSpec((1, tm, d), lambda i, j: (i, j, 0))
    return pl.pallas_call(
        kern,
        grid=(b, s // tm),
        in_specs=[act_spec, _const_spec((1, d)), _const_spec((d, 3 * d)), _const_spec((d, LANES)),
                  _const_spec((1, LANES)), _const_spec((d, LANES)), _const_spec((3, LANES, d))],
        out_specs=[act_spec, act_spec, act_spec, act_spec,
                   pl.BlockSpec((1, 3, N_HEADS, tm), lambda i, j: (i, 0, 0, j))],
        out_shape=[jax.ShapeDtypeStruct((b, s, d), BF16)] * 4 + [jax.ShapeDtypeStruct((b, 3, N_HEADS, s), F32)],
        scratch_shapes=[pltpu.VMEM((1, LANES), F32)],
        compiler_params=pltpu.CompilerParams(
            dimension_semantics=("arbitrary", "arbitrary"), vmem_limit_bytes=VMEM_LIMIT_BYTES),
        name="attn_qkv",
    )(x, gpre.reshape(1, -1).astype(F32), wqkv, wf, bf, ind, sel)


def _attn_kernel(q_ref, k_ref, ka_ref, v_ref, st_ref, o_ref, g_ref, *, s, tq):
    tk = tq
    lane = lax.broadcasted_iota(jnp.int32, (1, LANES), 1)
    head_lanes = [lane < HEAD_DIM, lane >= HEAD_DIM]
    bias_ones = [((lane >= HEAD_DIM) & (lane < HEAD_DIM + 3)).astype(BF16), (lane < 3).astype(BF16)]
    row_ids = lax.broadcasted_iota(jnp.int32, (tq, tk), 0)
    col_ids = lax.broadcasted_iota(jnp.int32, (tq, tk), 1)
    causal = col_ids <= row_ids
    pos = lax.broadcasted_iota(jnp.int32, (1, s), 1)

    nk = st_ref[0, 2, 0]
    kmax = jnp.max(nk, axis=1, keepdims=True)
    g_ref[...] = jnp.sqrt(st_ref[0, 1, 0]) * (jnp.sqrt(kmax) + jnp.sqrt(nk)) + st_ref[0, 0, 0]

    def kv_step(qh, k0, carry, masked):
        kp = k_ref[0, pl.ds(k0, tk), :]
        ap = ka_ref[0, pl.ds(k0, tk), :]
        vp = v_ref[0, pl.ds(k0, tk), :]
        new = []
        for hh in range(2):
            m, acc = carry[hh]
            kh = jnp.where(head_lanes[hh], kp, ap)
            sc = lax.dot_general(qh[hh], kh, (((1,), (1,)), ((), ())), preferred_element_type=F32)
            if masked:
                sc = jnp.where(causal, sc, MASK_VALUE)
            m_new = jnp.maximum(m, jnp.max(sc, axis=1, keepdims=True))
            alpha = jnp.exp2(m - m_new)
            p = jnp.exp2(sc - m_new).astype(BF16)
            vh = jnp.where(head_lanes[hh], vp, jnp.ones_like(vp))
            new.append((m_new, alpha * acc + jnp.dot(p, vh, preferred_element_type=F32)))
        return tuple(new)

    def q_block(qi, _):
        q0 = pl.multiple_of(qi * tq, tq)
        qp = q_ref[0, pl.ds(q0, tq), :]
        qh = [jnp.where(head_lanes[hh], qp, bias_ones[hh]) for hh in range(2)]

        gmax = jnp.max(g_ref[:, pl.ds(q0, tq)], axis=1, keepdims=True)
        live = (gmax - st_ref[0, 0, 0] >= -EXP2_UNDERFLOW) & (pos < q0)
        n_live = jnp.max(jnp.sum(live.astype(F32), axis=1, keepdims=True), axis=0, keepdims=True)
        n_back = jnp.ceil(n_live * (1.0 / tk)).astype(jnp.int32)[0, 0]

        init = tuple((jnp.full((tq, 1), MASK_VALUE, F32), jnp.zeros((tq, LANES), F32)) for _ in range(2))
        carry = lax.fori_loop(
            qi - n_back, qi, lambda kj, c: kv_step(qh, pl.multiple_of(kj * tk, tk), c, False), init)
        carry = kv_step(qh, q0, carry, True)
        acc0, acc1 = carry[0][1], carry[1][1]
        out = jnp.where(head_lanes[0], acc0 / pltpu.roll(acc0, HEAD_DIM, 1), acc1 / pltpu.roll(acc1, HEAD_DIM, 1))
        o_ref[0, pl.ds(q0, tq), :] = out.astype(o_ref.dtype)
        return 0

    lax.fori_loop(0, s // tq, q_block, 0)


def _attention(q, k, kaug, v, stats, *, tq=512):
    b, s, d = q.shape
    n_groups = d // LANES
    heads_per_group = LANES // HEAD_DIM
    assert heads_per_group == 2
    st5 = stats.reshape(b, 3, n_groups, heads_per_group, s)
    kern = functools.partial(_attn_kernel, s=s, tq=tq)
    col_spec = pl.BlockSpec((1, s, LANES), lambda i, j: (i, 0, j))
    return pl.pallas_call(
        kern,
        grid=(b, n_groups),
        in_specs=[col_spec, col_spec, col_spec, col_spec,
                  pl.BlockSpec((1, 3, 1, heads_per_group, s), lambda i, j: (i, 0, j, 0, 0))],
        out_specs=col_spec,
        out_shape=jax.ShapeDtypeStruct((b, s, d), BF16),
        scratch_shapes=[pltpu.VMEM((heads_per_group, s), F32)],
        compiler_params=pltpu.CompilerParams(
            dimension_semantics=("arbitrary", "arbitrary"), vmem_limit_bytes=VMEM_LIMIT_BYTES),
        name="fox_attention",
    )(q, k, kaug, v, st5)


def _out_proj_kernel(x_ref, o_ref, wo_ref, gpost_ref, y_ref):
    m = jnp.dot(o_ref[...], wo_ref[...], preferred_element_type=F32)
    y_ref[...] = x_ref[...] + _rms(m, gpost_ref[...])


def _out_proj(x2d, o2d, wo, gpost, *, tm=512):
    t, d = x2d.shape
    spec = pl.BlockSpec((tm, d), lambda i: (i, 0))
    return pl.pallas_call(
        _out_proj_kernel,
        grid=(t // tm,),
        in_specs=[spec, spec, _const_spec((d, d)), _const_spec((1, d))],
        out_specs=spec,
        out_shape=jax.ShapeDtypeStruct(x2d.shape, F32),
        compiler_params=pltpu.CompilerParams(
            dimension_semantics=("arbitrary",), vmem_limit_bytes=VMEM_LIMIT_BYTES),
        name="attn_out_proj",
    )(x2d, o2d, wo.astype(BF16), gpost.reshape(1, -1).astype(F32))


def kernel(x, g_mix_pre, g_mix_post, g_ffn_pre, g_ffn_post, conv_pw1_w, conv_pw1_b, conv_dw_w, conv_dw_b,
           conv_ln_g, conv_ln_b, conv_pw2_w, conv_pw2_b, attn_w_in, attn_b_f, attn_w_o, mlp_w_up, mlp_w_down):
    b, s, d = x.shape
    depth = g_mix_pre.shape[0]
    for i in range(depth):
        j = i // 2
        if i % 2 == 0:
            x = _conv_mixer(x, g_mix_pre[i], conv_pw1_w[j], conv_pw1_b[j], conv_dw_w[j], conv_dw_b[j],
                            conv_ln_g[j], conv_ln_b[j], conv_pw2_w[j], conv_pw2_b[j], g_mix_post[i])
        else:
            q, k, kaug, v, stats = _qkv(x, g_mix_pre[i], attn_w_in[j], attn_b_f[j])
            o = _attention(q, k, kaug, v, stats)
            x = _out_proj(x.reshape(b * s, d), o.reshape(b * s, d), attn_w_o[j], g_mix_post[i]).reshape(b, s, d)
        x = _mlp(x.reshape(b * s, d), g_ffn_pre[i], mlp_w_up[i], mlp_w_down[i], g_ffn_post[i]).reshape(b, s, d)
    return x
```

```python
import functools

import jax
import jax.numpy as jnp
from jax import lax
from jax.experimental import pallas as pl
from jax.experimental.pallas import tpu as pltpu

N_HEADS = 16
HEAD_DIM = 64
CONV_WIDTH = 31
RMS_EPS = 1e-6
LN_EPS = 1e-5
MASK_VALUE = -1e30

LANES = 128
SUBLANES = 8
CONV_ROWS = 64
CONV_HALO = 32
VMEM_LIMIT_BYTES = 56 * 1024 * 1024
EXP2_UNDERFLOW = 160.0
LOG2E = 1.4426950408889634
NORM_SLACK = 1.01

F32 = jnp.float32
BF16 = jnp.bfloat16


def _rms(x, g):
    return x * lax.rsqrt(jnp.mean(x * x, axis=-1, keepdims=True) + RMS_EPS) * g


def _const_spec(shape):
    nd = len(shape)
    return pl.BlockSpec(shape, lambda *_: (0,) * nd, pipeline_mode=pl.Buffered(1))


def _conv_mixer_kernel(x_ref, gpre_ref, w1_ref, b1_ref, dww_ref, dwb_ref, lng_ref, lnb_ref,
                       w2_ref, b2_ref, gpost_ref, o_ref, ubuf_ref, ybuf_ref, *, tm, d):
    x = x_ref[0]
    h = _rms(x, gpre_ref[...])
    a = jnp.dot(h.astype(BF16), w1_ref[...], preferred_element_type=F32) + b1_ref[...]
    u = a[:, :d] * jax.nn.sigmoid(a[:, d:])

    @pl.when(pl.program_id(1) == 0)
    def _():
        ubuf_ref[0:CONV_HALO, :] = jnp.zeros((CONV_HALO, d), F32)
        ubuf_ref[CONV_HALO + tm:CONV_HALO + tm + SUBLANES, :] = jnp.zeros((SUBLANES, d), F32)

    ubuf_ref[CONV_HALO:CONV_HALO + tm, :] = u

    def conv_chunk(i, _):
        r0 = pl.multiple_of(i * CONV_ROWS, CONV_ROWS)

        def lane_chunk(c, _):
            c0 = pl.multiple_of(c * LANES, LANES)
            acc = jnp.broadcast_to(dwb_ref[:, pl.ds(c0, LANES)], (CONV_ROWS, LANES))
            for r in range(SUBLANES):
                z = None
                for o in range(r if r >= 2 else r + SUBLANES, CONV_WIDTH + 2, SUBLANES):
                    rows = pl.ds(pl.multiple_of(r0 + (o - r), SUBLANES), CONV_ROWS + SUBLANES)
                    term = ubuf_ref[rows, pl.ds(c0, LANES)] * dww_ref[o - 2:o - 1, pl.ds(c0, LANES)]
                    z = term if z is None else z + term
                acc = acc + z[r:r + CONV_ROWS]
            ybuf_ref[pl.ds(r0, CONV_ROWS), pl.ds(c0, LANES)] = acc
            return 0

        return lax.fori_loop(0, d // LANES, lane_chunk, 0)

    lax.fori_loop(0, tm // CONV_ROWS, conv_chunk, 0)
    ubuf_ref[0:CONV_HALO, :] = ubuf_ref[tm:tm + CONV_HALO, :]

    y = ybuf_ref[...]
    mu = jnp.mean(y, axis=-1, keepdims=True)
    yc = y - mu
    var = jnp.mean(yc * yc, axis=-1, keepdims=True)
    yn = yc * lax.rsqrt(var + LN_EPS) * lng_ref[...] + lnb_ref[...]
    act = yn * jax.nn.sigmoid(yn)
    m = jnp.dot(act.astype(BF16), w2_ref[...], preferred_element_type=F32) + b2_ref[...]
    o_ref[0] = x + _rms(m, gpost_ref[...])


def _conv_mixer(x, gpre, w1, b1, dww, dwb, lng, lnb, w2, b2, gpost, *, tm=512):
    b, s, d = x.shape
    row = lambda v: v.reshape(1, -1).astype(F32)
    kern = functools.partial(_conv_mixer_kernel, tm=tm, d=d)
    return pl.pallas_call(
        kern,
        grid=(b, s // tm),
        in_specs=[
            pl.BlockSpec((1, tm, d), lambda i, j: (i, j, 0)),
            _const_spec((1, d)), _const_spec((d, 2 * d)), _const_spec((1, 2 * d)),
            _const_spec((CONV_WIDTH, d)), _const_spec((1, d)), _const_spec((1, d)), _const_spec((1, d)),
            _const_spec((d, d)), _const_spec((1, d)), _const_spec((1, d)),
        ],
        out_specs=pl.BlockSpec((1, tm, d), lambda i, j: (i, j, 0)),
        out_shape=jax.ShapeDtypeStruct(x.shape, F32),
        scratch_shapes=[pltpu.VMEM((CONV_HALO + tm + SUBLANES, d), F32), pltpu.VMEM((tm, d), F32)],
        compiler_params=pltpu.CompilerParams(
            dimension_semantics=("arbitrary", "arbitrary"), vmem_limit_bytes=VMEM_LIMIT_BYTES),
        name="conv_mixer",
    )(x, row(gpre), w1.astype(BF16), row(b1), dww.astype(F32), row(dwb), row(lng), row(lnb),
      w2.astype(BF16), row(b2), row(gpost))


def _mlp_body(x, gpre_ref, wup_ref, wdown_ref, gpost_ref, o_ref):
    h = _rms(x, gpre_ref[...])
    up = jnp.dot(h.astype(BF16), wup_ref[...], preferred_element_type=F32)
    r = jnp.maximum(up, 0.0)
    act = (r * r).astype(BF16)
    down = jnp.dot(act, wdown_ref[...], preferred_element_type=F32)
    o_ref[...] = x + _rms(down, gpost_ref[...])


def _mlp_kernel(x_ref, gpre_ref, wup_ref, wdown_ref, gpost_ref, o_ref):
    _mlp_body(x_ref[...], gpre_ref, wup_ref, wdown_ref, gpost_ref, o_ref)


def _attn_out_mlp_kernel(x_ref, a_ref, wo_ref, gmix_ref, gpre_ref, wup_ref, wdown_ref, gpost_ref, o_ref):
    m = jnp.dot(a_ref[...], wo_ref[...], preferred_element_type=F32)
    _mlp_body(x_ref[...] + _rms(m, gmix_ref[...]), gpre_ref, wup_ref, wdown_ref, gpost_ref, o_ref)


def _mlp(x2d, gpre, wup, wdown, gpost, attn=None, *, tm=512):
    t, d = x2d.shape
    ff = wup.shape[1]
    row = lambda v: v.reshape(1, -1).astype(F32)
    act_spec = pl.BlockSpec((tm, d), lambda i: (i, 0))
    mlp_specs = [_const_spec((1, d)), _const_spec((d, ff)), _const_spec((ff, d)), _const_spec((1, d))]
    mlp_args = (row(gpre), wup.astype(BF16), wdown.astype(BF16), row(gpost))
    if attn is None:
        kern, in_specs, args = _mlp_kernel, [act_spec] + mlp_specs, (x2d,) + mlp_args
    else:
        a2d, wo, gmix = attn
        kern = _attn_out_mlp_kernel
        in_specs = [act_spec, act_spec, _const_spec((d, d)), _const_spec((1, d))] + mlp_specs
        args = (x2d, a2d, wo.astype(BF16), row(gmix)) + mlp_args
    return pl.pallas_call(
        kern,
        grid=(t // tm,),
        in_specs=in_specs,
        out_specs=act_spec,
        out_shape=jax.ShapeDtypeStruct(x2d.shape, F32),
        compiler_params=pltpu.CompilerParams(
            dimension_semantics=("arbitrary",), vmem_limit_bytes=VMEM_LIMIT_BYTES),
        name="sqrelu_mlp" if attn is None else "attn_out_mlp",
    )(*args)


def _split3_bf16(x):
    hi = x.astype(BF16)
    r1 = x - hi.astype(F32)
    mid = r1.astype(BF16)
    lo = (r1 - mid.astype(F32)).astype(BF16)
    return hi, mid, lo


def _head_sq_norm_bound(xb, ind):
    xf = xb.astype(F32)
    nsq = jnp.dot((xf * xf).astype(BF16), ind, preferred_element_type=F32)
    return nsq * NORM_SLACK


def _qkv_kernel(x_ref, gpre_ref, wqkv_ref, wf_ref, bf_ref, ind_ref, sel_ref, q_ref, k_ref, ka_ref, v_ref, st_ref,
                carry_ref, *, tm, d):
    x = x_ref[0]
    hb = _rms(x, gpre_ref[...]).astype(BF16)
    proj = jnp.dot(hb, wqkv_ref[...], preferred_element_type=F32)
    qb = (proj[:, :d] * (HEAD_DIM ** -0.5 * LOG2E)).astype(BF16)
    kb = proj[:, d:2 * d].astype(BF16)
    q_ref[0] = qb
    k_ref[0] = kb
    v_ref[0] = proj[:, 2 * d:].astype(BF16)
    st_ref[0, 1] = _head_sq_norm_bound(qb, ind_ref[...]).T[:N_HEADS, :]
    st_ref[0, 2] = _head_sq_norm_bound(kb, ind_ref[...]).T[:N_HEADS, :]

    f_logit = jnp.dot(hb, wf_ref[...], preferred_element_type=F32) + bf_ref[...]
    log_f = jax.nn.log_sigmoid(f_logit)

    @pl.when(pl.program_id(1) == 0)
    def _():
        carry_ref[...] = jnp.zeros_like(carry_ref)

    rows = lax.broadcasted_iota(jnp.int32, (tm, tm), 0)
    cols = lax.broadcasted_iota(jnp.int32, (tm, tm), 1)
    tri = (rows >= cols).astype(BF16)
    hi, mid, lo = _split3_bf16(log_f)
    csum = (jnp.dot(tri, hi, preferred_element_type=F32) + jnp.dot(tri, mid, preferred_element_type=F32)
            + jnp.dot(tri, lo, preferred_element_type=F32))
    f_cum = csum + carry_ref[...]
    carry_ref[...] = f_cum[tm - 1:tm, :]
    f2 = f_cum * LOG2E
    st_ref[0, 0] = f2.T[:N_HEADS, :]
    f_hi, f_mid, f_lo = _split3_bf16(f2)
    ka = (jnp.dot(f_hi, sel_ref[0], preferred_element_type=F32) + jnp.dot(f_mid, sel_ref[1], preferred_element_type=F32)
          + jnp.dot(f_lo, sel_ref[2], preferred_element_type=F32))
    ka_ref[0] = ka.astype(BF16)


def _qkv(x, gpre, w_in, b_f, *, tm=512):
    b, s, d = x.shape
    wqkv = w_in[:, :3 * d].astype(BF16)
    wf = jnp.zeros((d, LANES), F32).at[:, :N_HEADS].set(w_in[:, 3 * d:]).astype(BF16)
    bf = jnp.zeros((1, LANES), F32).at[0, :N_HEADS].set(b_f.astype(F32))
    ind = (jnp.arange(d)[:, None] // HEAD_DIM == jnp.arange(LANES)[None, :]).astype(BF16)
    heads = jnp.arange(LANES)[:, None]
    dest = (heads // 2) * LANES + (1 - heads % 2) * HEAD_DIM
    cols = jnp.arange(d)[None, :]
    sel = jnp.stack([-((cols == dest + t) & (heads < N_HEADS)).astype(F32) for t in range(3)]).astype(BF16)
    kern = functools.partial(_qkv_kernel, tm=tm, d=d)
    act_spec = pl.BlockSpec((1, tm, d), lambda i, j: (i, j, 0))
    return pl.pallas_call(
        kern,
        grid=(b, s // tm),
        in_specs=[act_spec, _const_spec((1, d)), _const_spec((d, 3 * d)), _const_spec((d, LANES)),
                  _const_spec((1, LANES)), _const_spec((d, LANES)), _const_spec((3, LANES, d))],
        out_specs=[act_spec, act_spec, act_spec, act_spec,
                   pl.BlockSpec((1, 3, N_HEADS, tm), lambda i, j: (i, 0, 0, j))],
        out_shape=[jax.ShapeDtypeStruct((b, s, d), BF16)] * 4 + [jax.ShapeDtypeStruct((b, 3, N_HEADS, s), F32)],
        scratch_shapes=[pltpu.VMEM((1, LANES), F32)],
        compiler_params=pltpu.CompilerParams(
            dimension_semantics=("arbitrary", "arbitrary"), vmem_limit_bytes=VMEM_LIMIT_BYTES),
        name="attn_qkv",
    )(x, gpre.reshape(1, -1).astype(F32), wqkv, wf, bf, ind, sel)


def _attn_kernel(q_ref, k_ref, ka_ref, v_ref, st_ref, o_ref, g_ref, s_ref, *, s, tq):
    tk = tq
    lane = lax.broadcasted_iota(jnp.int32, (1, LANES), 1)
    head_lanes = [lane < HEAD_DIM, lane >= HEAD_DIM]
    bias_ones = [((lane >= HEAD_DIM) & (lane < HEAD_DIM + 3)).astype(BF16), (lane < 3).astype(BF16)]
    row_ids = lax.broadcasted_iota(jnp.int32, (tq, tk), 0)
    col_ids = lax.broadcasted_iota(jnp.int32, (tq, tk), 1)
    causal = col_ids <= row_ids
    pos = lax.broadcasted_iota(jnp.int32, (1, s), 1)

    nk = st_ref[0, 2, 0]
    kmax = jnp.max(nk, axis=1, keepdims=True)
    g_ref[...] = jnp.sqrt(st_ref[0, 1, 0]) * (jnp.sqrt(kmax) + jnp.sqrt(nk)) + st_ref[0, 0, 0]

    def prev_block(q0):
        return q0 - tk if isinstance(q0, int) else pl.multiple_of(q0 - tk, tk)

    def load_q(q0):
        qp = q_ref[0, pl.ds(q0, tq), :]
        return [jnp.where(head_lanes[hh], qp, bias_ones[hh]) for hh in range(2)]

    def scores(qh, k0, masked):
        kp = k_ref[0, pl.ds(k0, tk), :]
        ap = ka_ref[0, pl.ds(k0, tk), :]
        out = []
        for hh in range(2):
            kh = jnp.where(head_lanes[hh], kp, ap)
            sc = lax.dot_general(qh[hh], kh, (((1,), (1,)), ((), ())), preferred_element_type=F32)
            out.append(jnp.where(causal, sc, MASK_VALUE) if masked else sc)
        return out

    def softmax_pv(scs, k0s, carry):
        vps = [v_ref[0, pl.ds(k0, tk), :] for k0 in k0s]
        new = []
        for hh in range(2):
            m, acc = carry[hh]
            m_new = m
            for sc in scs[hh]:
                m_new = jnp.maximum(m_new, jnp.max(sc, axis=1, keepdims=True))
            acc = jnp.exp2(m - m_new) * acc
            for sc, vp in zip(scs[hh], vps):
                p = jnp.exp2(sc - m_new).astype(BF16)
                vh = jnp.where(head_lanes[hh], vp, jnp.ones_like(vp))
                acc = acc + jnp.dot(p, vh, preferred_element_type=F32)
            new.append((m_new, acc))
        return tuple(new)

    def init_carry():
        return tuple((jnp.full((tq, 1), MASK_VALUE, F32), jnp.zeros((tq, LANES), F32)) for _ in range(2))

    def far_blocks(qi, q0, qh):
        gmax = jnp.max(g_ref[:, pl.ds(q0, tq)], axis=1, keepdims=True)
        live = (gmax - st_ref[0, 0, 0] >= -EXP2_UNDERFLOW) & (pos < q0)
        n_live = jnp.max(jnp.sum(live.astype(F32), axis=1, keepdims=True), axis=0, keepdims=True)
        n_back = jnp.ceil(n_live * (1.0 / tk)).astype(jnp.int32)[0, 0]

        def step(kj, carry):
            k0 = pl.multiple_of(kj * tk, tk)
            return softmax_pv([[sc] for sc in scores(qh, k0, False)], [k0], carry)

        return lax.fori_loop(qi - n_back, qi - 1, step, init_carry())

    def tail_scores(slot, q0, qh):
        for bi, (k0, masked) in enumerate([(prev_block(q0), False), (q0, True)]):
            for hh, sc in enumerate(scores(qh, k0, masked)):
                s_ref[slot, hh, bi] = sc

    def tail_finish(slot, q0, carry):
        scs = [[s_ref[slot, hh, 0], s_ref[slot, hh, 1]] for hh in range(2)]
        finish(q0, softmax_pv(scs, [prev_block(q0), q0], carry))

    def finish(q0, carry):
        acc0, acc1 = carry[0][1], carry[1][1]
        out = jnp.where(head_lanes[0], acc0 / pltpu.roll(acc0, HEAD_DIM, 1), acc1 / pltpu.roll(acc1, HEAD_DIM, 1))
        o_ref[0, pl.ds(q0, tq), :] = out.astype(o_ref.dtype)

    finish(0, softmax_pv([[sc] for sc in scores(load_q(0), 0, True)], [0], init_carry()))

    n_q = s // tq
    last = n_q - 1
    tail_scores(0, tq, load_q(tq))

    def pair(i, _):
        qa = 2 * i + 1
        qb = qa + 1
        qn = jnp.minimum(qa + 2, last)
        a0, b0, n0 = (pl.multiple_of(x * tq, tq) for x in (qa, qb, qn))
        qha, qhb, qhn = load_q(a0), load_q(b0), load_q(n0)
        carry_a = far_blocks(qa, a0, qha)
        carry_b = far_blocks(qb, b0, qhb)
        tail_scores(1, b0, qhb)
        tail_finish(0, a0, carry_a)
        tail_scores(0, n0, qhn)
        tail_finish(1, b0, carry_b)
        return 0

    lax.fori_loop(0, last // 2, pair, 0)
    if last % 2 == 1:
        q0 = last * tq
        tail_finish(0, q0, far_blocks(last, q0, load_q(q0)))


def _attention(q, k, kaug, v, stats, *, tq=512):
    b, s, d = q.shape
    n_groups = d // LANES
    heads_per_group = LANES // HEAD_DIM
    assert heads_per_group == 2
    st5 = stats.reshape(b, 3, n_groups, heads_per_group, s)
    kern = functools.partial(_attn_kernel, s=s, tq=tq)
    col_spec = pl.BlockSpec((1, s, LANES), lambda i, j: (i, 0, j))
    return pl.pallas_call(
        kern,
        grid=(b, n_groups),
        in_specs=[col_spec, col_spec, col_spec, col_spec,
                  pl.BlockSpec((1, 3, 1, heads_per_group, s), lambda i, j: (i, 0, j, 0, 0))],
        out_specs=col_spec,
        out_shape=jax.ShapeDtypeStruct((b, s, d), BF16),
        scratch_shapes=[pltpu.VMEM((heads_per_group, s), F32),
                        pltpu.VMEM((2, heads_per_group, 2, tq, tq), F32)],
        compiler_params=pltpu.CompilerParams(
            dimension_semantics=("arbitrary", "arbitrary"), vmem_limit_bytes=VMEM_LIMIT_BYTES),
        name="fox_attention",
    )(q, k, kaug, v, st5)


def kernel(x, g_mix_pre, g_mix_post, g_ffn_pre, g_ffn_post, conv_pw1_w, conv_pw1_b, conv_dw_w, conv_dw_b,
           conv_ln_g, conv_ln_b, conv_pw2_w, conv_pw2_b, attn_w_in, attn_b_f, attn_w_o, mlp_w_up, mlp_w_down):
    b, s, d = x.shape
    depth = g_mix_pre.shape[0]
    for i in range(depth):
        j = i // 2
        attn = None
        if i % 2 == 0:
            x = _conv_mixer(x, g_mix_pre[i], conv_pw1_w[j], conv_pw1_b[j], conv_dw_w[j], conv_dw_b[j],
                            conv_ln_g[j], conv_ln_b[j], conv_pw2_w[j], conv_pw2_b[j], g_mix_post[i])
        else:
            q, k, kaug, v, stats = _qkv(x, g_mix_pre[i], attn_w_in[j], attn_b_f[j])
            attn = (_attention(q, k, kaug, v, stats).reshape(b * s, d), attn_w_o[j], g_mix_post[i])
        x = _mlp(x.reshape(b * s, d), g_ffn_pre[i], mlp_w_up[i], mlp_w_down[i], g_ffn_post[i], attn).reshape(b, s, d)
    return x
```

```python
import functools

import jax
import jax.numpy as jnp
from jax import lax
from jax.experimental import pallas as pl
from jax.experimental.pallas import tpu as pltpu

N_HEADS = 16
HEAD_DIM = 64
CONV_WIDTH = 31
RMS_EPS = 1e-6
LN_EPS = 1e-5
MASK_VALUE = -1e30

LANES = 128
SUBLANES = 8
CONV_ROWS = 64
CONV_HALO = 32
VMEM_LIMIT_BYTES = 56 * 1024 * 1024
EXP2_UNDERFLOW = 160.0
LOG2E = 1.4426950408889634
NORM_SLACK = 1.01

F32 = jnp.float32
BF16 = jnp.bfloat16


def _rms(x, g):
    return x * lax.rsqrt(jnp.mean(x * x, axis=-1, keepdims=True) + RMS_EPS) * g


def _const_spec(shape):
    nd = len(shape)
    return pl.BlockSpec(shape, lambda *_: (0,) * nd, pipeline_mode=pl.Buffered(1))


def _conv_mixer_kernel(x_ref, gpre_ref, w1_ref, b1_ref, dww_ref, dwb_ref, lng_ref, lnb_ref,
                       w2_ref, b2_ref, gpost_ref, o_ref, ubuf_ref, ybuf_ref, *, tm, d):
    x = x_ref[0]
    h = _rms(x, gpre_ref[...])
    a = jnp.dot(h.astype(BF16), w1_ref[...], preferred_element_type=F32) + b1_ref[...]
    u = a[:, :d] * jax.nn.sigmoid(a[:, d:])

    @pl.when(pl.program_id(1) == 0)
    def _():
        ubuf_ref[0:CONV_HALO, :] = jnp.zeros((CONV_HALO, d), F32)
        ubuf_ref[CONV_HALO + tm:CONV_HALO + tm + SUBLANES, :] = jnp.zeros((SUBLANES, d), F32)

    ubuf_ref[CONV_HALO:CONV_HALO + tm, :] = u

    def conv_chunk(i, _):
        r0 = pl.multiple_of(i * CONV_ROWS, CONV_ROWS)

        def lane_chunk(c, _):
            c0 = pl.multiple_of(c * LANES, LANES)
            acc = jnp.broadcast_to(dwb_ref[:, pl.ds(c0, LANES)], (CONV_ROWS, LANES))
            for r in range(SUBLANES):
                z = None
                for o in range(r if r >= 2 else r + SUBLANES, CONV_WIDTH + 2, SUBLANES):
                    rows = pl.ds(pl.multiple_of(r0 + (o - r), SUBLANES), CONV_ROWS + SUBLANES)
                    term = ubuf_ref[rows, pl.ds(c0, LANES)] * dww_ref[o - 2:o - 1, pl.ds(c0, LANES)]
                    z = term if z is None else z + term
                acc = acc + z[r:r + CONV_ROWS]
            ybuf_ref[pl.ds(r0, CONV_ROWS), pl.ds(c0, LANES)] = acc
            return 0

        return lax.fori_loop(0, d // LANES, lane_chunk, 0)

    lax.fori_loop(0, tm // CONV_ROWS, conv_chunk, 0)
    ubuf_ref[0:CONV_HALO, :] = ubuf_ref[tm:tm + CONV_HALO, :]

    y = ybuf_ref[...]
    mu = jnp.mean(y, axis=-1, keepdims=True)
    yc = y - mu
    var = jnp.mean(yc * yc, axis=-1, keepdims=True)
    yn = yc * lax.rsqrt(var + LN_EPS) * lng_ref[...] + lnb_ref[...]
    act = yn * jax.nn.sigmoid(yn)
    m = jnp.dot(act.astype(BF16), w2_ref[...], preferred_element_type=F32) + b2_ref[...]
    o_ref[0] = x + _rms(m, gpost_ref[...])


def _conv_mixer(x, gpre, w1, b1, dww, dwb, lng, lnb, w2, b2, gpost, *, tm=512):
    b, s, d = x.shape
    row = lambda v: v.reshape(1, -1).astype(F32)
    kern = functools.partial(_conv_mixer_kernel, tm=tm, d=d)
    return pl.pallas_call(
        kern,
        grid=(b, s // tm),
        in_specs=[
            pl.BlockSpec((1, tm, d), lambda i, j: (i, j, 0)),
            _const_spec((1, d)), _const_spec((d, 2 * d)), _const_spec((1, 2 * d)),
            _const_spec((CONV_WIDTH, d)), _const_spec((1, d)), _const_spec((1, d)), _const_spec((1, d)),
            _const_spec((d, d)), _const_spec((1, d)), _const_spec((1, d)),
        ],
        out_specs=pl.BlockSpec((1, tm, d), lambda i, j: (i, j, 0)),
        out_shape=jax.ShapeDtypeStruct(x.shape, F32),
        scratch_shapes=[pltpu.VMEM((CONV_HALO + tm + SUBLANES, d), F32), pltpu.VMEM((tm, d), F32)],
        compiler_params=pltpu.CompilerParams(
            dimension_semantics=("arbitrary", "arbitrary"), vmem_limit_bytes=VMEM_LIMIT_BYTES),
        name="conv_mixer",
    )(x, row(gpre), w1.astype(BF16), row(b1), dww.astype(F32), row(dwb), row(lng), row(lnb),
      w2.astype(BF16), row(b2), row(gpost))


def _mlp_body(x, gpre_ref, wup_ref, wdown_ref, gpost_ref, o_ref):
    h = _rms(x, gpre_ref[...])
    up = jnp.dot(h.astype(BF16), wup_ref[...], preferred_element_type=F32)
    r = jnp.maximum(up, 0.0)
    act = (r * r).astype(BF16)
    down = jnp.dot(act, wdown_ref[...], preferred_element_type=F32)
    o_ref[...] = x + _rms(down, gpost_ref[...])


def _mlp_kernel(x_ref, gpre_ref, wup_ref, wdown_ref, gpost_ref, o_ref):
    _mlp_body(x_ref[...], gpre_ref, wup_ref, wdown_ref, gpost_ref, o_ref)


def _attn_out_mlp_kernel(x_ref, a_ref, wo_ref, gmix_ref, gpre_ref, wup_ref, wdown_ref, gpost_ref, o_ref):
    m = jnp.dot(a_ref[...], wo_ref[...], preferred_element_type=F32)
    _mlp_body(x_ref[...] + _rms(m, gmix_ref[...]), gpre_ref, wup_ref, wdown_ref, gpost_ref, o_ref)


def _mlp(x2d, gpre, wup, wdown, gpost, attn=None, *, tm=512):
    t, d = x2d.shape
    ff = wup.shape[1]
    row = lambda v: v.reshape(1, -1).astype(F32)
    act_spec = pl.BlockSpec((tm, d), lambda i: (i, 0))
    mlp_specs = [_const_spec((1, d)), _const_spec((d, ff)), _const_spec((ff, d)), _const_spec((1, d))]
    mlp_args = (row(gpre), wup.astype(BF16), wdown.astype(BF16), row(gpost))
    if attn is None:
        kern, in_specs, args = _mlp_kernel, [act_spec] + mlp_specs, (x2d,) + mlp_args
    else:
        a2d, wo, gmix = attn
        kern = _attn_out_mlp_kernel
        in_specs = [act_spec, act_spec, _const_spec((d, d)), _const_spec((1, d))] + mlp_specs
        args = (x2d, a2d, wo.astype(BF16), row(gmix)) + mlp_args
    return pl.pallas_call(
        kern,
        grid=(t // tm,),
        in_specs=in_specs,
        out_specs=act_spec,
        out_shape=jax.ShapeDtypeStruct(x2d.shape, F32),
        compiler_params=pltpu.CompilerParams(
            dimension_semantics=("arbitrary",), vmem_limit_bytes=VMEM_LIMIT_BYTES),
        name="sqrelu_mlp" if attn is None else "attn_out_mlp",
    )(*args)


def _split3_bf16(x):
    hi = x.astype(BF16)
    r1 = x - hi.astype(F32)
    mid = r1.astype(BF16)
    lo = (r1 - mid.astype(F32)).astype(BF16)
    return hi, mid, lo


def _head_sq_norm_bound(xb, ind):
    xf = xb.astype(F32)
    nsq = jnp.dot((xf * xf).astype(BF16), ind, preferred_element_type=F32)
    return nsq * NORM_SLACK


def _qkv_kernel(x_ref, gpre_ref, wqkv_ref, wf_ref, bf_ref, ind_ref, sel_ref, q_ref, k_ref, ka_ref, v_ref, st_ref,
                carry_ref, *, tm, d):
    x = x_ref[0]
    hb = _rms(x, gpre_ref[...]).astype(BF16)
    proj = jnp.dot(hb, wqkv_ref[...], preferred_element_type=F32)
    qb = (proj[:, :d] * (HEAD_DIM ** -0.5 * LOG2E)).astype(BF16)
    kb = proj[:, d:2 * d].astype(BF16)
    q_ref[0] = qb
    k_ref[0] = kb
    v_ref[0] = proj[:, 2 * d:].astype(BF16)
    st_ref[0, 1] = _head_sq_norm_bound(qb, ind_ref[...]).T[:N_HEADS, :]
    st_ref[0, 2] = _head_sq_norm_bound(kb, ind_ref[...]).T[:N_HEADS, :]

    f_logit = jnp.dot(hb, wf_ref[...], preferred_element_type=F32) + bf_ref[...]
    log_f = jax.nn.log_sigmoid(f_logit)

    @pl.when(pl.program_id(1) == 0)
    def _():
        carry_ref[...] = jnp.zeros_like(carry_ref)

    rows = lax.broadcasted_iota(jnp.int32, (tm, tm), 0)
    cols = lax.broadcasted_iota(jnp.int32, (tm, tm), 1)
    tri = (rows >= cols).astype(BF16)
    hi, mid, lo = _split3_bf16(log_f)
    csum = (jnp.dot(tri, hi, preferred_element_type=F32) + jnp.dot(tri, mid, preferred_element_type=F32)
            + jnp.dot(tri, lo, preferred_element_type=F32))
    f_cum = csum + carry_ref[...]
    carry_ref[...] = f_cum[tm - 1:tm, :]
    f2 = f_cum * LOG2E
    st_ref[0, 0] = f2.T[:N_HEADS, :]
    f_hi, f_mid, f_lo = (t.astype(F32) for t in _split3_bf16(f2))
    lane = lax.broadcasted_iota(jnp.int32, (1, LANES), 1)
    packed = jnp.where(lane < N_HEADS, f_hi,
                       jnp.where(lane < 2 * N_HEADS, pltpu.roll(f_mid, N_HEADS, 1), pltpu.roll(f_lo, 2 * N_HEADS, 1)))
    ka_ref[0] = jnp.dot(packed.astype(BF16), sel_ref[...], preferred_element_type=F32).astype(BF16)


def _qkv(x, gpre, w_in, b_f, *, tm=512):
    b, s, d = x.shape
    wqkv = w_in[:, :3 * d].astype(BF16)
    wf = jnp.zeros((d, LANES), F32).at[:, :N_HEADS].set(w_in[:, 3 * d:]).astype(BF16)
    bf = jnp.zeros((1, LANES), F32).at[0, :N_HEADS].set(b_f.astype(F32))
    ind = (jnp.arange(d)[:, None] // HEAD_DIM == jnp.arange(LANES)[None, :]).astype(BF16)
    rows = jnp.arange(LANES)[:, None]
    head, term = rows % N_HEADS, rows // N_HEADS
    dest = (head // 2) * LANES + (1 - head % 2) * HEAD_DIM
    sel = -((jnp.arange(d)[None, :] == dest + term) & (term < 3)).astype(BF16)
    kern = functools.partial(_qkv_kernel, tm=tm, d=d)
    act_spec = pl.BlockSpec((1, tm, d), lambda i, j: (i, j, 0))
    return pl.pallas_call(
        kern,
        grid=(b, s // tm),
        in_specs=[act_spec, _const_spec((1, d)), _const_spec((d, 3 * d)), _const_spec((d, LANES)),
                  _const_spec((1, LANES)), _const_spec((d, LANES)), _const_spec((LANES, d))],
        out_specs=[act_spec, act_spec, act_spec, act_spec,
                   pl.BlockSpec((1, 3, N_HEADS, tm), lambda i, j: (i, 0, 0, j))],
        out_shape=[jax.ShapeDtypeStruct((b, s, d), BF16)] * 4 + [jax.ShapeDtypeStruct((b, 3, N_HEADS, s), F32)],
        scratch_shapes=[pltpu.VMEM((1, LANES), F32)],
        compiler_params=pltpu.CompilerParams(
            dimension_semantics=("arbitrary", "arbitrary"), vmem_limit_bytes=VMEM_LIMIT_BYTES),
        name="attn_qkv",
    )(x, gpre.reshape(1, -1).astype(F32), wqkv, wf, bf, ind, sel)


def _attn_kernel(q_ref, k_ref, ka_ref, v_ref, st_ref, o_ref, g_ref, s_ref, nback_ref, *, s, tq):
    tk = tq
    lane = lax.broadcasted_iota(jnp.int32, (1, LANES), 1)
    head_lanes = [lane < HEAD_DIM, lane >= HEAD_DIM]
    bias_ones = [((lane >= HEAD_DIM) & (lane < HEAD_DIM + 3)).astype(BF16), (lane < 3).astype(BF16)]
    row_ids = lax.broadcasted_iota(jnp.int32, (tq, tk), 0)
    col_ids = lax.broadcasted_iota(jnp.int32, (tq, tk), 1)
    causal = col_ids <= row_ids
    pos = lax.broadcasted_iota(jnp.int32, (1, s), 1)

    nk = st_ref[0, 2, 0]
    kmax = jnp.max(nk, axis=1, keepdims=True)
    g_ref[...] = jnp.sqrt(st_ref[0, 1, 0]) * (jnp.sqrt(kmax) + jnp.sqrt(nk)) + st_ref[0, 0, 0]

    def prev_block(q0):
        return q0 - tk if isinstance(q0, int) else pl.multiple_of(q0 - tk, tk)

    def load_q(q0):
        qp = q_ref[0, pl.ds(q0, tq), :]
        return [jnp.where(head_lanes[hh], qp, bias_ones[hh]) for hh in range(2)]

    def scores(qh, k0, masked):
        kp = k_ref[0, pl.ds(k0, tk), :]
        ap = ka_ref[0, pl.ds(k0, tk), :]
        out = []
        for hh in range(2):
            kh = jnp.where(head_lanes[hh], kp, ap)
            sc = lax.dot_general(qh[hh], kh, (((1,), (1,)), ((), ())), preferred_element_type=F32)
            out.append(jnp.where(causal, sc, MASK_VALUE) if masked else sc)
        return out

    def softmax_pv(scs, k0s, carry):
        vps = [v_ref[0, pl.ds(k0, tk), :] for k0 in k0s]
        new = []
        for hh in range(2):
            m, acc = carry[hh]
            m_new = m
            for sc in scs[hh]:
                m_new = jnp.maximum(m_new, jnp.max(sc, axis=1, keepdims=True))
            acc = jnp.exp2(m - m_new) * acc
            for sc, vp in zip(scs[hh], vps):
                p = jnp.exp2(sc - m_new).astype(BF16)
                vh = jnp.where(head_lanes[hh], vp, jnp.ones_like(vp))
                acc = acc + jnp.dot(p, vh, preferred_element_type=F32)
            new.append((m_new, acc))
        return tuple(new)

    def init_carry():
        return tuple((jnp.full((tq, 1), MASK_VALUE, F32), jnp.zeros((tq, LANES), F32)) for _ in range(2))

    for qi in range(1, s // tq):
        gmax = jnp.max(g_ref[:, qi * tq:(qi + 1) * tq], axis=1, keepdims=True)
        live = (gmax - st_ref[0, 0, 0] >= -EXP2_UNDERFLOW) & (pos < qi * tq)
        n_live = jnp.max(jnp.sum(live.astype(F32), axis=1, keepdims=True), axis=0, keepdims=True)
        nback_ref[qi] = jnp.ceil(n_live * (1.0 / tk)).astype(jnp.int32)[0, 0]

    def far_blocks(qi, qh):
        def step(kj, carry):
            k0 = pl.multiple_of(kj * tk, tk)
            return softmax_pv([[sc] for sc in scores(qh, k0, False)], [k0], carry)

        return lax.fori_loop(qi - nback_ref[qi], qi - 1, step, init_carry())

    def tail_scores(slot, q0, qh):
        for bi, (k0, masked) in enumerate([(prev_block(q0), False), (q0, True)]):
            for hh, sc in enumerate(scores(qh, k0, masked)):
                s_ref[slot, hh, bi] = sc

    def tail_finish(slot, q0, carry):
        scs = [[s_ref[slot, hh, 0], s_ref[slot, hh, 1]] for hh in range(2)]
        finish(q0, softmax_pv(scs, [prev_block(q0), q0], carry))

    def finish(q0, carry):
        acc0, acc1 = carry[0][1], carry[1][1]
        out = jnp.where(head_lanes[0], acc0 / pltpu.roll(acc0, HEAD_DIM, 1), acc1 / pltpu.roll(acc1, HEAD_DIM, 1))
        o_ref[0, pl.ds(q0, tq), :] = out.astype(o_ref.dtype)

    finish(0, softmax_pv([[sc] for sc in scores(load_q(0), 0, True)], [0], init_carry()))

    n_q = s // tq
    last = n_q - 1
    tail_scores(0, tq, load_q(tq))

    def pair(i, _):
        qa = 2 * i + 1
        qb = qa + 1
        qn = jnp.minimum(qa + 2, last)
        a0, b0, n0 = (pl.multiple_of(x * tq, tq) for x in (qa, qb, qn))
        qha, qhb, qhn = load_q(a0), load_q(b0), load_q(n0)
        carry_a = far_blocks(qa, qha)
        carry_b = far_blocks(qb, qhb)
        tail_scores(1, b0, qhb)
        tail_finish(0, a0, carry_a)
        tail_scores(0, n0, qhn)
        tail_finish(1, b0, carry_b)
        return 0

    lax.fori_loop(0, last // 2, pair, 0)
    if last % 2 == 1:
        q0 = last * tq
        tail_finish(0, q0, far_blocks(last, load_q(q0)))


def _attention(q, k, kaug, v, stats, *, tq=512):
    b, s, d = q.shape
    n_groups = d // LANES
    heads_per_group = LANES // HEAD_DIM
    assert heads_per_group == 2
    st5 = stats.reshape(b, 3, n_groups, heads_per_group, s)
    kern = functools.partial(_attn_kernel, s=s, tq=tq)
    col_spec = pl.BlockSpec((1, s, LANES), lambda i, j: (i, 0, j))
    return pl.pallas_call(
        kern,
        grid=(b, n_groups),
        in_specs=[col_spec, col_spec, col_spec, col_spec,
                  pl.BlockSpec((1, 3, 1, heads_per_group, s), lambda i, j: (i, 0, j, 0, 0))],
        out_specs=col_spec,
        out_shape=jax.ShapeDtypeStruct((b, s, d), BF16),
        scratch_shapes=[pltpu.VMEM((heads_per_group, s), F32),
                        pltpu.VMEM((2, heads_per_group, 2, tq, tq), F32),
                        pltpu.SMEM((s // tq,), jnp.int32)],
        compiler_params=pltpu.CompilerParams(
            dimension_semantics=("arbitrary", "arbitrary"), vmem_limit_bytes=VMEM_LIMIT_BYTES),
        name="fox_attention",
    )(q, k, kaug, v, st5)


def kernel(x, g_mix_pre, g_mix_post, g_ffn_pre, g_ffn_post, conv_pw1_w, conv_pw1_b, conv_dw_w, conv_dw_b,
           conv_ln_g, conv_ln_b, conv_pw2_w, conv_pw2_b, attn_w_in, attn_b_f, attn_w_o, mlp_w_up, mlp_w_down):
    b, s, d = x.shape
    depth = g_mix_pre.shape[0]
    for i in range(depth):
        j = i // 2
        attn = None
        if i % 2 == 0:
            x = _conv_mixer(x, g_mix_pre[i], conv_pw1_w[j], conv_pw1_b[j], conv_dw_w[j], conv_dw_b[j],
                            conv_ln_g[j], conv_ln_b[j], conv_pw2_w[j], conv_pw2_b[j], g_mix_post[i])
        else:
            q, k, kaug, v, stats = _qkv(x, g_mix_pre[i], attn_w_in[j], attn_b_f[j])
            attn = (_attention(q, k, kaug, v, stats).reshape(b * s, d), attn_w_o[j], g_mix_post[i])
        x = _mlp(x.reshape(b * s, d), g_ffn_pre[i], mlp_w_up[i], mlp_w_down[i], g_ffn_post[i], attn).reshape(b, s, d)
    return x
```

```python
import functools

import jax
import jax.numpy as jnp
from jax import lax
from jax.experimental import pallas as pl
from jax.experimental.pallas import tpu as pltpu

N_HEADS = 16
HEAD_DIM = 64
CONV_WIDTH = 31
RMS_EPS = 1e-6
LN_EPS = 1e-5
MASK_VALUE = -1e30

LANES = 128
SUBLANES = 8
CONV_ROWS = 64
CONV_COLS = 256
CONV_HALO = 32
VMEM_LIMIT_BYTES = 56 * 1024 * 1024
EXP2_UNDERFLOW = 160.0
LOG2E = 1.4426950408889634
NORM_SLACK = 1.01

F32 = jnp.float32
BF16 = jnp.bfloat16


def _rms(x, g):
    return x * lax.rsqrt(jnp.mean(x * x, axis=-1, keepdims=True) + RMS_EPS) * g


def _const_spec(shape):
    nd = len(shape)
    return pl.BlockSpec(shape, lambda *_: (0,) * nd, pipeline_mode=pl.Buffered(1))


def _conv_mixer_kernel(x_ref, gpre_ref, w1_ref, b1_ref, dww_ref, dwb_ref, lng_ref, lnb_ref,
                       w2_ref, b2_ref, gpost_ref, o_ref, ubuf_ref, ybuf_ref, *, tm, d):
    x = x_ref[0]
    hb = _rms(x, gpre_ref[...]).astype(BF16)

    @pl.when(pl.program_id(1) == 0)
    def _():
        ubuf_ref[0:CONV_HALO, :] = jnp.zeros((CONV_HALO, d), F32)
        ubuf_ref[CONV_HALO + tm:CONV_HALO + tm + SUBLANES, :] = jnp.zeros((SUBLANES, d), F32)

    def conv_chunk(r0, c0):
        acc = jnp.broadcast_to(dwb_ref[:, c0:c0 + LANES], (CONV_ROWS, LANES))
        for r in range(SUBLANES):
            z = None
            for o in range(r if r >= 2 else r + SUBLANES, CONV_WIDTH + 2, SUBLANES):
                a8 = r0 + o - r
                term = ubuf_ref[a8:a8 + CONV_ROWS + SUBLANES, c0:c0 + LANES] * dww_ref[o - 2:o - 1, c0:c0 + LANES]
                z = term if z is None else z + term
            acc = acc + z[r:r + CONV_ROWS]
        ybuf_ref[r0:r0 + CONV_ROWS, c0:c0 + LANES] = acc

    for g0 in range(0, d, CONV_COLS):
        cols = slice(g0, g0 + CONV_COLS)
        gate_cols = slice(d + g0, d + g0 + CONV_COLS)
        a_u = jnp.dot(hb, w1_ref[:, cols], preferred_element_type=F32) + b1_ref[:, cols]
        a_g = jnp.dot(hb, w1_ref[:, gate_cols], preferred_element_type=F32) + b1_ref[:, gate_cols]
        ubuf_ref[CONV_HALO:CONV_HALO + tm, cols] = a_u * jax.nn.sigmoid(a_g)
        for c0 in range(g0, g0 + CONV_COLS, LANES):
            for r0 in range(0, tm, CONV_ROWS):
                conv_chunk(r0, c0)
        ubuf_ref[0:CONV_HALO, cols] = ubuf_ref[tm:tm + CONV_HALO, cols]

    y = ybuf_ref[...]
    mu = jnp.mean(y, axis=-1, keepdims=True)
    yc = y - mu
    var = jnp.mean(yc * yc, axis=-1, keepdims=True)
    yn = yc * lax.rsqrt(var + LN_EPS) * lng_ref[...] + lnb_ref[...]
    act = yn * jax.nn.sigmoid(yn)
    m = jnp.dot(act.astype(BF16), w2_ref[...], preferred_element_type=F32) + b2_ref[...]
    o_ref[0] = x + _rms(m, gpost_ref[...])


def _conv_mixer(x, gpre, w1, b1, dww, dwb, lng, lnb, w2, b2, gpost, *, tm=512):
    b, s, d = x.shape
    row = lambda v: v.reshape(1, -1).astype(F32)
    kern = functools.partial(_conv_mixer_kernel, tm=tm, d=d)
    return pl.pallas_call(
        kern,
        grid=(b, s // tm),
        in_specs=[
            pl.BlockSpec((1, tm, d), lambda i, j: (i, j, 0)),
            _const_spec((1, d)), _const_spec((d, 2 * d)), _const_spec((1, 2 * d)),
            _const_spec((CONV_WIDTH, d)), _const_spec((1, d)), _const_spec((1, d)), _const_spec((1, d)),
            _const_spec((d, d)), _const_spec((1, d)), _const_spec((1, d)),
        ],
        out_specs=pl.BlockSpec((1, tm, d), lambda i, j: (i, j, 0)),
        out_shape=jax.ShapeDtypeStruct(x.shape, F32),
        scratch_shapes=[pltpu.VMEM((CONV_HALO + tm + SUBLANES, d), F32), pltpu.VMEM((tm, d), F32)],
        compiler_params=pltpu.CompilerParams(
            dimension_semantics=("arbitrary", "arbitrary"), vmem_limit_bytes=VMEM_LIMIT_BYTES),
        name="conv_mixer",
    )(x, row(gpre), w1.astype(BF16), row(b1), dww.astype(F32), row(dwb), row(lng), row(lnb),
      w2.astype(BF16), row(b2), row(gpost))


def _mlp_body(x, gpre_ref, wup_ref, wdown_ref, gpost_ref, o_ref):
    h = _rms(x, gpre_ref[...])
    up = jnp.dot(h.astype(BF16), wup_ref[...], preferred_element_type=F32)
    r = jnp.maximum(up, 0.0)
    act = (r * r).astype(BF16)
    down = jnp.dot(act, wdown_ref[...], preferred_element_type=F32)
    o_ref[...] = x + _rms(down, gpost_ref[...])


def _mlp_kernel(x_ref, gpre_ref, wup_ref, wdown_ref, gpost_ref, o_ref):
    _mlp_body(x_ref[...], gpre_ref, wup_ref, wdown_ref, gpost_ref, o_ref)


def _attn_out_mlp_kernel(x_ref, a_ref, wo_ref, gmix_ref, gpre_ref, wup_ref, wdown_ref, gpost_ref, o_ref):
    m = jnp.dot(a_ref[...], wo_ref[...], preferred_element_type=F32)
    _mlp_body(x_ref[...] + _rms(m, gmix_ref[...]), gpre_ref, wup_ref, wdown_ref, gpost_ref, o_ref)


def _mlp(x2d, gpre, wup, wdown, gpost, attn=None, *, tm=512):
    t, d = x2d.shape
    ff = wup.shape[1]
    row = lambda v: v.reshape(1, -1).astype(F32)
    act_spec = pl.BlockSpec((tm, d), lambda i: (i, 0))
    mlp_specs = [_const_spec((1, d)), _const_spec((d, ff)), _const_spec((ff, d)), _const_spec((1, d))]
    mlp_args = (row(gpre), wup.astype(BF16), wdown.astype(BF16), row(gpost))
    if attn is None:
        kern, in_specs, args = _mlp_kernel, [act_spec] + mlp_specs, (x2d,) + mlp_args
    else:
        a2d, wo, gmix = attn
        kern = _attn_out_mlp_kernel
        in_specs = [act_spec, act_spec, _const_spec((d, d)), _const_spec((1, d))] + mlp_specs
        args = (x2d, a2d, wo.astype(BF16), row(gmix)) + mlp_args
    return pl.pallas_call(
        kern,
        grid=(t // tm,),
        in_specs=in_specs,
        out_specs=act_spec,
        out_shape=jax.ShapeDtypeStruct(x2d.shape, F32),
        compiler_params=pltpu.CompilerParams(
            dimension_semantics=("arbitrary",), vmem_limit_bytes=VMEM_LIMIT_BYTES),
        name="sqrelu_mlp" if attn is None else "attn_out_mlp",
    )(*args)


def _split3_bf16(x):
    hi = x.astype(BF16)
    r1 = x - hi.astype(F32)
    mid = r1.astype(BF16)
    lo = (r1 - mid.astype(F32)).astype(BF16)
    return hi, mid, lo


def _head_sq_norm_bound(xb, ind):
    xf = xb.astype(F32)
    nsq = jnp.dot((xf * xf).astype(BF16), ind, preferred_element_type=F32)
    return nsq * NORM_SLACK


def _qkv_kernel(x_ref, gpre_ref, wqkv_ref, wf_ref, bf_ref, ind_ref, sel_ref, q_ref, k_ref, ka_ref, v_ref, st_ref,
                carry_ref, *, tm, d):
    x = x_ref[0]
    hb = _rms(x, gpre_ref[...]).astype(BF16)
    proj = jnp.dot(hb, wqkv_ref[...], preferred_element_type=F32)
    qb = (proj[:, :d] * (HEAD_DIM ** -0.5 * LOG2E)).astype(BF16)
    kb = proj[:, d:2 * d].astype(BF16)
    q_ref[0] = qb
    k_ref[0] = kb
    v_ref[0] = proj[:, 2 * d:].astype(BF16)
    st_ref[0, 1] = _head_sq_norm_bound(qb, ind_ref[...]).T[:N_HEADS, :]
    st_ref[0, 2] = _head_sq_norm_bound(kb, ind_ref[...]).T[:N_HEADS, :]

    f_logit = jnp.dot(hb, wf_ref[...], preferred_element_type=F32) + bf_ref[...]
    log_f = jax.nn.log_sigmoid(f_logit)

    @pl.when(pl.program_id(1) == 0)
    def _():
        carry_ref[...] = jnp.zeros_like(carry_ref)

    rows = lax.broadcasted_iota(jnp.int32, (tm, tm), 0)
    cols = lax.broadcasted_iota(jnp.int32, (tm, tm), 1)
    tri = (rows >= cols).astype(BF16)
    hi, mid, lo = _split3_bf16(log_f)
    csum = (jnp.dot(tri, hi, preferred_element_type=F32) + jnp.dot(tri, mid, preferred_element_type=F32)
            + jnp.dot(tri, lo, preferred_element_type=F32))
    f_cum = csum + carry_ref[...]
    carry_ref[...] = f_cum[tm - 1:tm, :]
    f2 = f_cum * LOG2E
    st_ref[0, 0] = f2.T[:N_HEADS, :]
    f_hi, f_mid, f_lo = (t.astype(F32) for t in _split3_bf16(f2))
    lane = lax.broadcasted_iota(jnp.int32, (1, LANES), 1)
    packed = jnp.where(lane < N_HEADS, f_hi,
                       jnp.where(lane < 2 * N_HEADS, pltpu.roll(f_mid, N_HEADS, 1), pltpu.roll(f_lo, 2 * N_HEADS, 1)))
    ka_ref[0] = jnp.dot(packed.astype(BF16), sel_ref[...], preferred_element_type=F32).astype(BF16)


def _qkv(x, gpre, w_in, b_f, *, tm=512):
    b, s, d = x.shape
    wqkv = w_in[:, :3 * d].astype(BF16)
    wf = jnp.zeros((d, LANES), F32).at[:, :N_HEADS].set(w_in[:, 3 * d:]).astype(BF16)
    bf = jnp.zeros((1, LANES), F32).at[0, :N_HEADS].set(b_f.astype(F32))
    ind = (jnp.arange(d)[:, None] // HEAD_DIM == jnp.arange(LANES)[None, :]).astype(BF16)
    rows = jnp.arange(LANES)[:, None]
    head, term = rows % N_HEADS, rows // N_HEADS
    dest = (head // 2) * LANES + (1 - head % 2) * HEAD_DIM
    sel = -((jnp.arange(d)[None, :] == dest + term) & (term < 3)).astype(BF16)
    kern = functools.partial(_qkv_kernel, tm=tm, d=d)
    act_spec = pl.BlockSpec((1, tm, d), lambda i, j: (i, j, 0))
    return pl.pallas_call(
        kern,
        grid=(b, s // tm),
        in_specs=[act_spec, _const_spec((1, d)), _const_spec((d, 3 * d)), _const_spec((d, LANES)),
                  _const_spec((1, LANES)), _const_spec((d, LANES)), _const_spec((LANES, d))],
        out_specs=[act_spec, act_spec, act_spec, act_spec,
                   pl.BlockSpec((1, 3, N_HEADS, tm), lambda i, j: (i, 0, 0, j))],
        out_shape=[jax.ShapeDtypeStruct((b, s, d), BF16)] * 4 + [jax.ShapeDtypeStruct((b, 3, N_HEADS, s), F32)],
        scratch_shapes=[pltpu.VMEM((1, LANES), F32)],
        compiler_params=pltpu.CompilerParams(
            dimension_semantics=("arbitrary", "arbitrary"), vmem_limit_bytes=VMEM_LIMIT_BYTES),
        name="attn_qkv",
    )(x, gpre.reshape(1, -1).astype(F32), wqkv, wf, bf, ind, sel)


def _attn_kernel(q_ref, k_ref, ka_ref, v_ref, st_ref, o_ref, g_ref, s_ref, nback_ref, *, s, tq):
    tk = tq
    lane = lax.broadcasted_iota(jnp.int32, (1, LANES), 1)
    head_lanes = [lane < HEAD_DIM, lane >= HEAD_DIM]
    bias_ones = [((lane >= HEAD_DIM) & (lane < HEAD_DIM + 3)).astype(BF16), (lane < 3).astype(BF16)]
    row_ids = lax.broadcasted_iota(jnp.int32, (tq, tk), 0)
    col_ids = lax.broadcasted_iota(jnp.int32, (tq, tk), 1)
    causal = col_ids <= row_ids
    pos = lax.broadcasted_iota(jnp.int32, (1, s), 1)

    nk = st_ref[0, 2, 0]
    kmax = jnp.max(nk, axis=1, keepdims=True)
    g_ref[...] = jnp.sqrt(st_ref[0, 1, 0]) * (jnp.sqrt(kmax) + jnp.sqrt(nk)) + st_ref[0, 0, 0]

    def prev_block(q0):
        return q0 - tk if isinstance(q0, int) else pl.multiple_of(q0 - tk, tk)

    def load_q(q0):
        qp = q_ref[0, pl.ds(q0, tq), :]
        return [jnp.where(head_lanes[hh], qp, bias_ones[hh]) for hh in range(2)]

    def scores(qh, k0, masked):
        kp = k_ref[0, pl.ds(k0, tk), :]
        ap = ka_ref[0, pl.ds(k0, tk), :]
        out = []
        for hh in range(2):
            kh = jnp.where(head_lanes[hh], kp, ap)
            sc = lax.dot_general(qh[hh], kh, (((1,), (1,)), ((), ())), preferred_element_type=F32)
            out.append(jnp.where(causal, sc, MASK_VALUE) if masked else sc)
        return out

    def softmax_pv(scs, k0s, carry):
        vps = [v_ref[0, pl.ds(k0, tk), :] for k0 in k0s]
        new = []
        for hh in range(2):
            m, acc = carry[hh]
            m_new = m
            for sc in scs[hh]:
                m_new = jnp.maximum(m_new, jnp.max(sc, axis=1, keepdims=True))
            acc = jnp.exp2(m - m_new) * acc
            for sc, vp in zip(scs[hh], vps):
                p = jnp.exp2(sc - m_new).astype(BF16)
                vh = jnp.where(head_lanes[hh], vp, jnp.ones_like(vp))
                acc = acc + jnp.dot(p, vh, preferred_element_type=F32)
            new.append((m_new, acc))
        return tuple(new)

    def init_carry():
        return tuple((jnp.full((tq, 1), MASK_VALUE, F32), jnp.zeros((tq, LANES), F32)) for _ in range(2))

    for qi in range(1, s // tq):
        gmax = jnp.max(g_ref[:, qi * tq:(qi + 1) * tq], axis=1, keepdims=True)
        live = (gmax - st_ref[0, 0, 0] >= -EXP2_UNDERFLOW) & (pos < qi * tq)
        n_live = jnp.max(jnp.sum(live.astype(F32), axis=1, keepdims=True), axis=0, keepdims=True)
        nback_ref[qi] = jnp.ceil(n_live * (1.0 / tk)).astype(jnp.int32)[0, 0]

    def far_blocks(qi, qh):
        def step(kj, carry):
            k0 = pl.multiple_of(kj * tk, tk)
            return softmax_pv([[sc] for sc in scores(qh, k0, False)], [k0], carry)

        return lax.fori_loop(qi - nback_ref[qi], qi - 1, step, init_carry())

    def tail_scores(slot, q0, qh):
        for bi, (k0, masked) in enumerate([(prev_block(q0), False), (q0, True)]):
            for hh, sc in enumerate(scores(qh, k0, masked)):
                s_ref[slot, hh, bi] = sc

    def tail_finish(slot, q0, carry):
        scs = [[s_ref[slot, hh, 0], s_ref[slot, hh, 1]] for hh in range(2)]
        finish(q0, softmax_pv(scs, [prev_block(q0), q0], carry))

    def finish(q0, carry):
        acc0, acc1 = carry[0][1], carry[1][1]
        out = jnp.where(head_lanes[0], acc0 / pltpu.roll(acc0, HEAD_DIM, 1), acc1 / pltpu.roll(acc1, HEAD_DIM, 1))
        o_ref[0, pl.ds(q0, tq), :] = out.astype(o_ref.dtype)

    finish(0, softmax_pv([[sc] for sc in scores(load_q(0), 0, True)], [0], init_carry()))

    n_q = s // tq
    last = n_q - 1
    tail_scores(0, tq, load_q(tq))

    def pair(i, _):
        qa = 2 * i + 1
        qb = qa + 1
        qn = jnp.minimum(qa + 2, last)
        a0, b0, n0 = (pl.multiple_of(x * tq, tq) for x in (qa, qb, qn))
        qha, qhb, qhn = load_q(a0), load_q(b0), load_q(n0)
        carry_a = far_blocks(qa, qha)
        carry_b = far_blocks(qb, qhb)
        tail_scores(1, b0, qhb)
        tail_finish(0, a0, carry_a)
        tail_scores(0, n0, qhn)
        tail_finish(1, b0, carry_b)
        return 0

    lax.fori_loop(0, last // 2, pair, 0)
    if last % 2 == 1:
        q0 = last * tq
        tail_finish(0, q0, far_blocks(last, load_q(q0)))


def _attention(q, k, kaug, v, stats, *, tq=512):
    b, s, d = q.shape
    n_groups = d // LANES
    heads_per_group = LANES // HEAD_DIM
    assert heads_per_group == 2
    st5 = stats.reshape(b, 3, n_groups, heads_per_group, s)
    kern = functools.partial(_attn_kernel, s=s, tq=tq)
    col_spec = pl.BlockSpec((1, s, LANES), lambda i, j: (i, 0, j))
    return pl.pallas_call(
        kern,
        grid=(b, n_groups),
        in_specs=[col_spec, col_spec, col_spec, col_spec,
                  pl.BlockSpec((1, 3, 1, heads_per_group, s), lambda i, j: (i, 0, j, 0, 0))],
        out_specs=col_spec,
        out_shape=jax.ShapeDtypeStruct((b, s, d), BF16),
        scratch_shapes=[pltpu.VMEM((heads_per_group, s), F32),
                        pltpu.VMEM((2, heads_per_group, 2, tq, tq), F32),
                        pltpu.SMEM((s // tq,), jnp.int32)],
        compiler_params=pltpu.CompilerParams(
            dimension_semantics=("arbitrary", "arbitrary"), vmem_limit_bytes=VMEM_LIMIT_BYTES),
        name="fox_attention",
    )(q, k, kaug, v, st5)


def kernel(x, g_mix_pre, g_mix_post, g_ffn_pre, g_ffn_post, conv_pw1_w, conv_pw1_b, conv_dw_w, conv_dw_b,
           conv_ln_g, conv_ln_b, conv_pw2_w, conv_pw2_b, attn_w_in, attn_b_f, attn_w_o, mlp_w_up, mlp_w_down):
    b, s, d = x.shape
    depth = g_mix_pre.shape[0]
    for i in range(depth):
        j = i // 2
        attn = None
        if i % 2 == 0:
            x = _conv_mixer(x, g_mix_pre[i], conv_pw1_w[j], conv_pw1_b[j], conv_dw_w[j], conv_dw_b[j],
                            conv_ln_g[j], conv_ln_b[j], conv_pw2_w[j], conv_pw2_b[j], g_mix_post[i])
        else:
            q, k, kaug, v, stats = _qkv(x, g_mix_pre[i], attn_w_in[j], attn_b_f[j])
            attn = (_attention(q, k, kaug, v, stats).reshape(b * s, d), attn_w_o[j], g_mix_post[i])
        x = _mlp(x.reshape(b * s, d), g_ffn_pre[i], mlp_w_up[i], mlp_w_down[i], g_ffn_post[i], attn).reshape(b, s, d)
    return x
```

```python
import functools

import jax
import jax.numpy as jnp
from jax import lax
from jax.experimental import pallas as pl
from jax.experimental.pallas import tpu as pltpu

N_HEADS = 16
HEAD_DIM = 64
CONV_WIDTH = 31
RMS_EPS = 1e-6
LN_EPS = 1e-5
MASK_VALUE = -1e30

LANES = 128
SUBLANES = 8
CONV_ROWS = 64
CONV_COLS = 256
FF_CHUNK = 1024
CONV_HALO = 32
VMEM_LIMIT_BYTES = 56 * 1024 * 1024
EXP2_UNDERFLOW = 160.0
LOG2E = 1.4426950408889634
NORM_SLACK = 1.01

F32 = jnp.float32
BF16 = jnp.bfloat16


def _rms(x, g):
    return x * lax.rsqrt(jnp.mean(x * x, axis=-1, keepdims=True) + RMS_EPS) * g


def _const_spec(shape):
    nd = len(shape)
    return pl.BlockSpec(shape, lambda *_: (0,) * nd, pipeline_mode=pl.Buffered(1))


def _conv_mixer_kernel(x_ref, gpre_ref, w1_ref, b1_ref, dww_ref, dwb_ref, lng_ref, lnb_ref,
                       w2_ref, b2_ref, gpost_ref, o_ref, ubuf_ref, ybuf_ref, *, tm, d):
    x = x_ref[0]
    hb = _rms(x, gpre_ref[...]).astype(BF16)

    @pl.when(pl.program_id(1) == 0)
    def _():
        ubuf_ref[0:CONV_HALO, :] = jnp.zeros((CONV_HALO, d), F32)
        ubuf_ref[CONV_HALO + tm:CONV_HALO + tm + SUBLANES, :] = jnp.zeros((SUBLANES, d), F32)

    def conv_chunk(r0, c0):
        acc = jnp.broadcast_to(dwb_ref[:, c0:c0 + LANES], (CONV_ROWS, LANES))
        for r in range(SUBLANES):
            z = None
            for o in range(r if r >= 2 else r + SUBLANES, CONV_WIDTH + 2, SUBLANES):
                a8 = r0 + o - r
                term = ubuf_ref[a8:a8 + CONV_ROWS + SUBLANES, c0:c0 + LANES] * dww_ref[o - 2:o - 1, c0:c0 + LANES]
                z = term if z is None else z + term
            acc = acc + z[r:r + CONV_ROWS]
        ybuf_ref[r0:r0 + CONV_ROWS, c0:c0 + LANES] = acc

    for g0 in range(0, d, CONV_COLS):
        cols = slice(g0, g0 + CONV_COLS)
        gate_cols = slice(d + g0, d + g0 + CONV_COLS)
        a_u = jnp.dot(hb, w1_ref[:, cols], preferred_element_type=F32) + b1_ref[:, cols]
        a_g = jnp.dot(hb, w1_ref[:, gate_cols], preferred_element_type=F32) + b1_ref[:, gate_cols]
        ubuf_ref[CONV_HALO:CONV_HALO + tm, cols] = a_u * jax.nn.sigmoid(a_g)
        for c0 in range(g0, g0 + CONV_COLS, LANES):
            for r0 in range(0, tm, CONV_ROWS):
                conv_chunk(r0, c0)
        ubuf_ref[0:CONV_HALO, cols] = ubuf_ref[tm:tm + CONV_HALO, cols]

    y = ybuf_ref[...]
    mu = jnp.mean(y, axis=-1, keepdims=True)
    yc = y - mu
    var = jnp.mean(yc * yc, axis=-1, keepdims=True)
    yn = yc * lax.rsqrt(var + LN_EPS) * lng_ref[...] + lnb_ref[...]
    act = yn * jax.nn.sigmoid(yn)
    m = jnp.dot(act.astype(BF16), w2_ref[...], preferred_element_type=F32) + b2_ref[...]
    o_ref[0] = x + _rms(m, gpost_ref[...])


def _conv_mixer(x, gpre, w1, b1, dww, dwb, lng, lnb, w2, b2, gpost, *, tm=512):
    b, s, d = x.shape
    row = lambda v: v.reshape(1, -1).astype(F32)
    kern = functools.partial(_conv_mixer_kernel, tm=tm, d=d)
    return pl.pallas_call(
        kern,
        grid=(b, s // tm),
        in_specs=[
            pl.BlockSpec((1, tm, d), lambda i, j: (i, j, 0)),
            _const_spec((1, d)), _const_spec((d, 2 * d)), _const_spec((1, 2 * d)),
            _const_spec((CONV_WIDTH, d)), _const_spec((1, d)), _const_spec((1, d)), _const_spec((1, d)),
            _const_spec((d, d)), _const_spec((1, d)), _const_spec((1, d)),
        ],
        out_specs=pl.BlockSpec((1, tm, d), lambda i, j: (i, j, 0)),
        out_shape=jax.ShapeDtypeStruct(x.shape, F32),
        scratch_shapes=[pltpu.VMEM((CONV_HALO + tm + SUBLANES, d), F32), pltpu.VMEM((tm, d), F32)],
        compiler_params=pltpu.CompilerParams(
            dimension_semantics=("arbitrary", "arbitrary"), vmem_limit_bytes=VMEM_LIMIT_BYTES),
        name="conv_mixer",
    )(x, row(gpre), w1.astype(BF16), row(b1), dww.astype(F32), row(dwb), row(lng), row(lnb),
      w2.astype(BF16), row(b2), row(gpost))


def _mlp_body(x, gpre_ref, wup_ref, wdown_ref, gpost_ref, o_ref):
    hb = _rms(x, gpre_ref[...]).astype(BF16)
    down = None
    for c0 in range(0, wup_ref.shape[1], FF_CHUNK):
        up = jnp.dot(hb, wup_ref[:, c0:c0 + FF_CHUNK].astype(BF16), preferred_element_type=F32)
        r = jnp.maximum(up, 0.0)
        part = jnp.dot((r * r).astype(BF16), wdown_ref[c0:c0 + FF_CHUNK, :].astype(BF16), preferred_element_type=F32)
        down = part if down is None else down + part
    o_ref[...] = x + _rms(down, gpost_ref[...])


def _mlp_kernel(x_ref, gpre_ref, wup_ref, wdown_ref, gpost_ref, o_ref):
    _mlp_body(x_ref[...], gpre_ref, wup_ref, wdown_ref, gpost_ref, o_ref)


def _attn_out_mlp_kernel(x_ref, a_ref, wo_ref, gmix_ref, gpre_ref, wup_ref, wdown_ref, gpost_ref, o_ref):
    m = jnp.dot(a_ref[...], wo_ref[...], preferred_element_type=F32)
    _mlp_body(x_ref[...] + _rms(m, gmix_ref[...]), gpre_ref, wup_ref, wdown_ref, gpost_ref, o_ref)


def _mlp(x2d, gpre, wup, wdown, gpost, attn=None, *, tm=512):
    t, d = x2d.shape
    ff = wup.shape[1]
    row = lambda v: v.reshape(1, -1).astype(F32)
    act_spec = pl.BlockSpec((tm, d), lambda i: (i, 0))
    mlp_specs = [_const_spec((1, d)), _const_spec((d, ff)), _const_spec((ff, d)), _const_spec((1, d))]
    mlp_args = (row(gpre), wup.astype(F32), wdown.astype(F32), row(gpost))
    if attn is None:
        kern, in_specs, args = _mlp_kernel, [act_spec] + mlp_specs, (x2d,) + mlp_args
    else:
        a2d, wo, gmix = attn
        kern = _attn_out_mlp_kernel
        in_specs = [act_spec, act_spec, _const_spec((d, d)), _const_spec((1, d))] + mlp_specs
        args = (x2d, a2d, wo.astype(BF16), row(gmix)) + mlp_args
    return pl.pallas_call(
        kern,
        grid=(t // tm,),
        in_specs=in_specs,
        out_specs=act_spec,
        out_shape=jax.ShapeDtypeStruct(x2d.shape, F32),
        compiler_params=pltpu.CompilerParams(
            dimension_semantics=("arbitrary",), vmem_limit_bytes=VMEM_LIMIT_BYTES),
        name="sqrelu_mlp" if attn is None else "attn_out_mlp",
    )(*args)


def _split3_bf16(x):
    hi = x.astype(BF16)
    r1 = x - hi.astype(F32)
    mid = r1.astype(BF16)
    lo = (r1 - mid.astype(F32)).astype(BF16)
    return hi, mid, lo


def _head_sq_norm_bound(xb, ind):
    xf = xb.astype(F32)
    nsq = jnp.dot((xf * xf).astype(BF16), ind, preferred_element_type=F32)
    return nsq * NORM_SLACK


def _qkv_kernel(x_ref, gpre_ref, win_ref, wf_ref, bf_ref, ind_ref, sel_ref, q_ref, k_ref, ka_ref, v_ref, st_ref,
                carry_ref, *, tm, d):
    x = x_ref[0]
    hb = _rms(x, gpre_ref[...]).astype(BF16)
    proj = jnp.dot(hb, win_ref[:, :3 * d].astype(BF16), preferred_element_type=F32)
    qb = (proj[:, :d] * (HEAD_DIM ** -0.5 * LOG2E)).astype(BF16)
    kb = proj[:, d:2 * d].astype(BF16)
    q_ref[0] = qb
    k_ref[0] = kb
    v_ref[0] = proj[:, 2 * d:].astype(BF16)
    st_ref[0, 1] = _head_sq_norm_bound(qb, ind_ref[...]).T[:N_HEADS, :]
    st_ref[0, 2] = _head_sq_norm_bound(kb, ind_ref[...]).T[:N_HEADS, :]

    f_logit = jnp.dot(hb, wf_ref[...], preferred_element_type=F32) + bf_ref[...]
    log_f = jax.nn.log_sigmoid(f_logit)

    @pl.when(pl.program_id(1) == 0)
    def _():
        carry_ref[...] = jnp.zeros_like(carry_ref)

    rows = lax.broadcasted_iota(jnp.int32, (tm, tm), 0)
    cols = lax.broadcasted_iota(jnp.int32, (tm, tm), 1)
    tri = (rows >= cols).astype(BF16)
    hi, mid, lo = _split3_bf16(log_f)
    csum = (jnp.dot(tri, hi, preferred_element_type=F32) + jnp.dot(tri, mid, preferred_element_type=F32)
            + jnp.dot(tri, lo, preferred_element_type=F32))
    f_cum = csum + carry_ref[...]
    carry_ref[...] = f_cum[tm - 1:tm, :]
    f2 = f_cum * LOG2E
    st_ref[0, 0] = f2.T[:N_HEADS, :]
    f_hi, f_mid, f_lo = (t.astype(F32) for t in _split3_bf16(f2))
    lane = lax.broadcasted_iota(jnp.int32, (1, LANES), 1)
    packed = jnp.where(lane < N_HEADS, f_hi,
                       jnp.where(lane < 2 * N_HEADS, pltpu.roll(f_mid, N_HEADS, 1), pltpu.roll(f_lo, 2 * N_HEADS, 1)))
    ka_ref[0] = jnp.dot(packed.astype(BF16), sel_ref[...], preferred_element_type=F32).astype(BF16)


def _qkv(x, gpre, w_in, b_f, *, tm=512):
    b, s, d = x.shape
    wf = jnp.zeros((d, LANES), F32).at[:, :N_HEADS].set(w_in[:, 3 * d:]).astype(BF16)
    bf = jnp.zeros((1, LANES), F32).at[0, :N_HEADS].set(b_f.astype(F32))
    ind = (jnp.arange(d)[:, None] // HEAD_DIM == jnp.arange(LANES)[None, :]).astype(BF16)
    rows = jnp.arange(LANES)[:, None]
    head, term = rows % N_HEADS, rows // N_HEADS
    dest = (head // 2) * LANES + (1 - head % 2) * HEAD_DIM
    sel = -((jnp.arange(d)[None, :] == dest + term) & (term < 3)).astype(BF16)
    kern = functools.partial(_qkv_kernel, tm=tm, d=d)
    act_spec = pl.BlockSpec((1, tm, d), lambda i, j: (i, j, 0))
    return pl.pallas_call(
        kern,
        grid=(b, s // tm),
        in_specs=[act_spec, _const_spec((1, d)), _const_spec(w_in.shape), _const_spec((d, LANES)),
                  _const_spec((1, LANES)), _const_spec((d, LANES)), _const_spec((LANES, d))],
        out_specs=[act_spec, act_spec, act_spec, act_spec,
                   pl.BlockSpec((1, 3, N_HEADS, tm), lambda i, j: (i, 0, 0, j))],
        out_shape=[jax.ShapeDtypeStruct((b, s, d), BF16)] * 4 + [jax.ShapeDtypeStruct((b, 3, N_HEADS, s), F32)],
        scratch_shapes=[pltpu.VMEM((1, LANES), F32)],
        compiler_params=pltpu.CompilerParams(
            dimension_semantics=("arbitrary", "arbitrary"), vmem_limit_bytes=VMEM_LIMIT_BYTES),
        name="attn_qkv",
    )(x, gpre.reshape(1, -1).astype(F32), w_in.astype(F32), wf, bf, ind, sel)


def _attn_kernel(q_ref, k_ref, ka_ref, v_ref, st_ref, o_ref, g_ref, s_ref, nback_ref, *, s, tq):
    tk = tq
    lane = lax.broadcasted_iota(jnp.int32, (1, LANES), 1)
    head_lanes = [lane < HEAD_DIM, lane >= HEAD_DIM]
    bias_ones = [((lane >= HEAD_DIM) & (lane < HEAD_DIM + 3)).astype(BF16), (lane < 3).astype(BF16)]
    row_ids = lax.broadcasted_iota(jnp.int32, (tq, tk), 0)
    col_ids = lax.broadcasted_iota(jnp.int32, (tq, tk), 1)
    causal = col_ids <= row_ids
    pos = lax.broadcasted_iota(jnp.int32, (1, s), 1)

    nk = st_ref[0, 2, 0]
    kmax = jnp.max(nk, axis=1, keepdims=True)
    g_ref[...] = jnp.sqrt(st_ref[0, 1, 0]) * (jnp.sqrt(kmax) + jnp.sqrt(nk)) + st_ref[0, 0, 0]

    def prev_block(q0):
        return q0 - tk if isinstance(q0, int) else pl.multiple_of(q0 - tk, tk)

    def load_q(q0):
        qp = q_ref[0, pl.ds(q0, tq), :]
        return [jnp.where(head_lanes[hh], qp, bias_ones[hh]) for hh in range(2)]

    def scores(qh, k0, masked):
        kp = k_ref[0, pl.ds(k0, tk), :]
        ap = ka_ref[0, pl.ds(k0, tk), :]
        out = []
        for hh in range(2):
            kh = jnp.where(head_lanes[hh], kp, ap)
            sc = lax.dot_general(qh[hh], kh, (((1,), (1,)), ((), ())), preferred_element_type=F32)
            out.append(jnp.where(causal, sc, MASK_VALUE) if masked else sc)
        return out

    def softmax_pv(scs, k0s, carry):
        vps = [v_ref[0, pl.ds(k0, tk), :] for k0 in k0s]
        new = []
        for hh in range(2):
            m, acc = carry[hh]
            m_new = m
            for sc in scs[hh]:
                m_new = jnp.maximum(m_new, jnp.max(sc, axis=1, keepdims=True))
            acc = jnp.exp2(m - m_new) * acc
            for sc, vp in zip(scs[hh], vps):
                p = jnp.exp2(sc - m_new).astype(BF16)
                vh = jnp.where(head_lanes[hh], vp, jnp.ones_like(vp))
                acc = acc + jnp.dot(p, vh, preferred_element_type=F32)
            new.append((m_new, acc))
        return tuple(new)

    def init_carry():
        return tuple((jnp.full((tq, 1), MASK_VALUE, F32), jnp.zeros((tq, LANES), F32)) for _ in range(2))

    for qi in range(1, s // tq):
        gmax = jnp.max(g_ref[:, qi * tq:(qi + 1) * tq], axis=1, keepdims=True)
        live = (gmax - st_ref[0, 0, 0] >= -EXP2_UNDERFLOW) & (pos < qi * tq)
        n_live = jnp.max(jnp.sum(live.astype(F32), axis=1, keepdims=True), axis=0, keepdims=True)
        nback_ref[qi] = jnp.ceil(n_live * (1.0 / tk)).astype(jnp.int32)[0, 0]

    def far_blocks(qi, qh):
        def step(kj, carry):
            k0 = pl.multiple_of(kj * tk, tk)
            return softmax_pv([[sc] for sc in scores(qh, k0, False)], [k0], carry)

        return lax.fori_loop(qi - nback_ref[qi], qi - 1, step, init_carry())

    def tail_scores(slot, q0, qh):
        for bi, (k0, masked) in enumerate([(prev_block(q0), False), (q0, True)]):
            for hh, sc in enumerate(scores(qh, k0, masked)):
                s_ref[slot, hh, bi] = sc

    def tail_finish(slot, q0, carry):
        scs = [[s_ref[slot, hh, 0], s_ref[slot, hh, 1]] for hh in range(2)]
        finish(q0, softmax_pv(scs, [prev_block(q0), q0], carry))

    def finish(q0, carry):
        acc0, acc1 = carry[0][1], carry[1][1]
        out = jnp.where(head_lanes[0], acc0 / pltpu.roll(acc0, HEAD_DIM, 1), acc1 / pltpu.roll(acc1, HEAD_DIM, 1))
        o_ref[0, pl.ds(q0, tq), :] = out.astype(o_ref.dtype)

    finish(0, softmax_pv([[sc] for sc in scores(load_q(0), 0, True)], [0], init_carry()))

    n_q = s // tq
    last = n_q - 1
    tail_scores(0, tq, load_q(tq))

    def pair(i, _):
        qa = 2 * i + 1
        qb = qa + 1
        qn = jnp.minimum(qa + 2, last)
        a0, b0, n0 = (pl.multiple_of(x * tq, tq) for x in (qa, qb, qn))
        qha, qhb, qhn = load_q(a0), load_q(b0), load_q(n0)
        carry_a = far_blocks(qa, qha)
        carry_b = far_blocks(qb, qhb)
        tail_scores(1, b0, qhb)
        tail_finish(0, a0, carry_a)
        tail_scores(0, n0, qhn)
        tail_finish(1, b0, carry_b)
        return 0

    lax.fori_loop(0, last // 2, pair, 0)
    if last % 2 == 1:
        q0 = last * tq
        tail_finish(0, q0, far_blocks(last, load_q(q0)))


def _attention(q, k, kaug, v, stats, *, tq=512):
    b, s, d = q.shape
    n_groups = d // LANES
    heads_per_group = LANES // HEAD_DIM
    assert heads_per_group == 2
    st5 = stats.reshape(b, 3, n_groups, heads_per_group, s)
    kern = functools.partial(_attn_kernel, s=s, tq=tq)
    col_spec = pl.BlockSpec((1, s, LANES), lambda i, j: (i, 0, j))
    return pl.pallas_call(
        kern,
        grid=(b, n_groups),
        in_specs=[col_spec, col_spec, col_spec, col_spec,
                  pl.BlockSpec((1, 3, 1, heads_per_group, s), lambda i, j: (i, 0, j, 0, 0))],
        out_specs=col_spec,
        out_shape=jax.ShapeDtypeStruct((b, s, d), BF16),
        scratch_shapes=[pltpu.VMEM((heads_per_group, s), F32),
                        pltpu.VMEM((2, heads_per_group, 2, tq, tq), F32),
                        pltpu.SMEM((s // tq,), jnp.int32)],
        compiler_params=pltpu.CompilerParams(
            dimension_semantics=("arbitrary", "arbitrary"), vmem_limit_bytes=VMEM_LIMIT_BYTES),
        name="fox_attention",
    )(q, k, kaug, v, st5)


def kernel(x, g_mix_pre, g_mix_post, g_ffn_pre, g_ffn_post, conv_pw1_w, conv_pw1_b, conv_dw_w, conv_dw_b,
           conv_ln_g, conv_ln_b, conv_pw2_w, conv_pw2_b, attn_w_in, attn_b_f, attn_w_o, mlp_w_up, mlp_w_down):
    b, s, d = x.shape
    depth = g_mix_pre.shape[0]
    for i in range(depth):
        j = i // 2
        attn = None
        if i % 2 == 0:
            x = _conv_mixer(x, g_mix_pre[i], conv_pw1_w[j], conv_pw1_b[j], conv_dw_w[j], conv_dw_b[j],
                            conv_ln_g[j], conv_ln_b[j], conv_pw2_w[j], conv_pw2_b[j], g_mix_post[i])
        else:
            q, k, kaug, v, stats = _qkv(x, g_mix_pre[i], attn_w_in[j], attn_b_f[j])
            attn = (_attention(q, k, kaug, v, stats).reshape(b * s, d), attn_w_o[j], g_mix_post[i])
        x = _mlp(x.reshape(b * s, d), g_ffn_pre[i], mlp_w_up[i], mlp_w_down[i], g_ffn_post[i], attn).reshape(b, s, d)
    return x
```

```python
import functools

import jax
import jax.numpy as jnp
from jax import lax
from jax.experimental import pallas as pl
from jax.experimental.pallas import tpu as pltpu

N_HEADS = 16
HEAD_DIM = 64
CONV_WIDTH = 31
RMS_EPS = 1e-6
LN_EPS = 1e-5
MASK_VALUE = -1e30

LANES = 128
SUBLANES = 8
CONV_ROWS = 64
CONV_COLS = 256
FF_CHUNK = 1024
CONV_HALO = 32
VMEM_LIMIT_BYTES = 56 * 1024 * 1024
EXP2_UNDERFLOW = 160.0
LOG2E = 1.4426950408889634
NORM_SLACK = 1.01

F32 = jnp.float32
BF16 = jnp.bfloat16


def _rms(x, g):
    return x * lax.rsqrt(jnp.mean(x * x, axis=-1, keepdims=True) + RMS_EPS) * g


def _const_spec(shape):
    nd = len(shape)
    return pl.BlockSpec(shape, lambda *_: (0,) * nd, pipeline_mode=pl.Buffered(1))


def _conv_mixer_kernel(x_ref, gpre_ref, w1_ref, b1_ref, dww_ref, dwb_ref, lng_ref, lnb_ref,
                       w2_ref, b2_ref, gpost_ref, o_ref, ubuf_ref, ybuf_ref, *, tm, d):
    x = x_ref[0]
    hb = _rms(x, gpre_ref[...]).astype(BF16)

    @pl.when(pl.program_id(1) == 0)
    def _():
        ubuf_ref[0:CONV_HALO, :] = jnp.zeros((CONV_HALO, d), F32)
        ubuf_ref[CONV_HALO + tm:CONV_HALO + tm + SUBLANES, :] = jnp.zeros((SUBLANES, d), F32)

    def conv_chunk(r0, c0):
        acc = jnp.broadcast_to(dwb_ref[:, c0:c0 + LANES], (CONV_ROWS, LANES))
        for r in range(SUBLANES):
            z = None
            for o in range(r if r >= 2 else r + SUBLANES, CONV_WIDTH + 2, SUBLANES):
                a8 = r0 + o - r
                term = ubuf_ref[a8:a8 + CONV_ROWS + SUBLANES, c0:c0 + LANES] * dww_ref[o - 2:o - 1, c0:c0 + LANES]
                z = term if z is None else z + term
            acc = acc + z[r:r + CONV_ROWS]
        ybuf_ref[r0:r0 + CONV_ROWS, c0:c0 + LANES] = acc

    for g0 in range(0, d, CONV_COLS):
        cols = slice(g0, g0 + CONV_COLS)
        gate_cols = slice(d + g0, d + g0 + CONV_COLS)
        a_u = jnp.dot(hb, w1_ref[:, cols], preferred_element_type=F32) + b1_ref[:, cols]
        a_g = jnp.dot(hb, w1_ref[:, gate_cols], preferred_element_type=F32) + b1_ref[:, gate_cols]
        ubuf_ref[CONV_HALO:CONV_HALO + tm, cols] = a_u * jax.nn.sigmoid(a_g)
        for c0 in range(g0, g0 + CONV_COLS, LANES):
            for r0 in range(0, tm, CONV_ROWS):
                conv_chunk(r0, c0)
        ubuf_ref[0:CONV_HALO, cols] = ubuf_ref[tm:tm + CONV_HALO, cols]

    y = ybuf_ref[...]
    mu = jnp.mean(y, axis=-1, keepdims=True)
    yc = y - mu
    var = jnp.mean(yc * yc, axis=-1, keepdims=True)
    yn = yc * lax.rsqrt(var + LN_EPS) * lng_ref[...] + lnb_ref[...]
    act = yn * jax.nn.sigmoid(yn)
    m = jnp.dot(act.astype(BF16), w2_ref[...], preferred_element_type=F32) + b2_ref[...]
    o_ref[0] = x + _rms(m, gpost_ref[...])


def _conv_mixer(x, gpre, w1, b1, dww, dwb, lng, lnb, w2, b2, gpost, *, tm=512):
    b, s, d = x.shape
    row = lambda v: v.reshape(1, -1).astype(F32)
    kern = functools.partial(_conv_mixer_kernel, tm=tm, d=d)
    return pl.pallas_call(
        kern,
        grid=(b, s // tm),
        in_specs=[
            pl.BlockSpec((1, tm, d), lambda i, j: (i, j, 0)),
            _const_spec((1, d)), _const_spec((d, 2 * d)), _const_spec((1, 2 * d)),
            _const_spec((CONV_WIDTH, d)), _const_spec((1, d)), _const_spec((1, d)), _const_spec((1, d)),
            _const_spec((d, d)), _const_spec((1, d)), _const_spec((1, d)),
        ],
        out_specs=pl.BlockSpec((1, tm, d), lambda i, j: (i, j, 0)),
        out_shape=jax.ShapeDtypeStruct(x.shape, F32),
        scratch_shapes=[pltpu.VMEM((CONV_HALO + tm + SUBLANES, d), F32), pltpu.VMEM((tm, d), F32)],
        compiler_params=pltpu.CompilerParams(
            dimension_semantics=("arbitrary", "arbitrary"), vmem_limit_bytes=VMEM_LIMIT_BYTES),
        name="conv_mixer",
    )(x, row(gpre), w1.astype(BF16), row(b1), dww.astype(F32), row(dwb), row(lng), row(lnb),
      w2.astype(BF16), row(b2), row(gpost))


def _mlp_body(x, gpre_ref, wup_ref, wdown_ref, gpost_ref, o_ref):
    hb = _rms(x, gpre_ref[...]).astype(BF16)
    down = None
    for c0 in range(0, wup_ref.shape[1], FF_CHUNK):
        up = jnp.dot(hb, wup_ref[:, c0:c0 + FF_CHUNK].astype(BF16), preferred_element_type=F32)
        r = jnp.maximum(up, 0.0)
        part = jnp.dot((r * r).astype(BF16), wdown_ref[c0:c0 + FF_CHUNK, :].astype(BF16), preferred_element_type=F32)
        down = part if down is None else down + part
    o_ref[...] = x + _rms(down, gpost_ref[...])


def _mlp_kernel(x_ref, gpre_ref, wup_ref, wdown_ref, gpost_ref, o_ref):
    _mlp_body(x_ref[...], gpre_ref, wup_ref, wdown_ref, gpost_ref, o_ref)


def _attn_out_mlp_kernel(x_ref, a_ref, wo_ref, gmix_ref, gpre_ref, wup_ref, wdown_ref, gpost_ref, o_ref):
    m = jnp.dot(a_ref[...], wo_ref[...], preferred_element_type=F32)
    _mlp_body(x_ref[...] + _rms(m, gmix_ref[...]), gpre_ref, wup_ref, wdown_ref, gpost_ref, o_ref)


def _mlp(x2d, gpre, wup_all, wdown_all, layer, gpost, attn=None, *, tm=512):
    t, d = x2d.shape
    row = lambda v: v.reshape(1, -1).astype(F32)
    act_spec = pl.BlockSpec((tm, d), lambda i: (i, 0))
    mlp_specs = [_const_spec((1, d)), _layer_spec(wup_all, layer), _layer_spec(wdown_all, layer), _const_spec((1, d))]
    mlp_args = (row(gpre), wup_all.astype(F32), wdown_all.astype(F32), row(gpost))
    if attn is None:
        kern, in_specs, args = _mlp_kernel, [act_spec] + mlp_specs, (x2d,) + mlp_args
    else:
        a2d, wo, gmix = attn
        kern = _attn_out_mlp_kernel
        in_specs = [act_spec, act_spec, _const_spec((d, d)), _const_spec((1, d))] + mlp_specs
        args = (x2d, a2d, wo.astype(BF16), row(gmix)) + mlp_args
    return pl.pallas_call(
        kern,
        grid=(t // tm,),
        in_specs=in_specs,
        out_specs=act_spec,
        out_shape=jax.ShapeDtypeStruct(x2d.shape, F32),
        compiler_params=pltpu.CompilerParams(
            dimension_semantics=("arbitrary",), vmem_limit_bytes=VMEM_LIMIT_BYTES),
        name="sqrelu_mlp" if attn is None else "attn_out_mlp",
    )(*args)


def _split3_bf16(x):
    hi = x.astype(BF16)
    r1 = x - hi.astype(F32)
    mid = r1.astype(BF16)
    lo = (r1 - mid.astype(F32)).astype(BF16)
    return hi, mid, lo


def _head_sq_norm_bound(xb, ind):
    xf = xb.astype(F32)
    nsq = jnp.dot((xf * xf).astype(BF16), ind, preferred_element_type=F32)
    return nsq * NORM_SLACK


def _qkv_kernel(x_ref, gpre_ref, win_ref, wf_ref, bf_ref, ind_ref, sel_ref, q_ref, k_ref, ka_ref, v_ref, st_ref,
                carry_ref, *, tm, d):
    x = x_ref[0]
    hb = _rms(x, gpre_ref[...]).astype(BF16)
    proj = jnp.dot(hb, win_ref[:, :3 * d].astype(BF16), preferred_element_type=F32)
    qb = (proj[:, :d] * (HEAD_DIM ** -0.5 * LOG2E)).astype(BF16)
    kb = proj[:, d:2 * d].astype(BF16)
    q_ref[0] = qb
    k_ref[0] = kb
    v_ref[0] = proj[:, 2 * d:].astype(BF16)

    def put_stat(t, per_head):
        rows = per_head.T
        for g in range(N_HEADS // 2):
            st_ref[0, t, g] = rows[2 * g:2 * g + 2, :]

    put_stat(1, _head_sq_norm_bound(qb, ind_ref[...]))
    put_stat(2, _head_sq_norm_bound(kb, ind_ref[...]))

    f_logit = jnp.dot(hb, wf_ref[...], preferred_element_type=F32) + bf_ref[...]
    log_f = jax.nn.log_sigmoid(f_logit)

    @pl.when(pl.program_id(1) == 0)
    def _():
        carry_ref[...] = jnp.zeros_like(carry_ref)

    rows = lax.broadcasted_iota(jnp.int32, (tm, tm), 0)
    cols = lax.broadcasted_iota(jnp.int32, (tm, tm), 1)
    tri = (rows >= cols).astype(BF16)
    hi, mid, lo = _split3_bf16(log_f)
    csum = (jnp.dot(tri, hi, preferred_element_type=F32) + jnp.dot(tri, mid, preferred_element_type=F32)
            + jnp.dot(tri, lo, preferred_element_type=F32))
    f_cum = csum + carry_ref[...]
    carry_ref[...] = f_cum[tm - 1:tm, :]
    f2 = f_cum * LOG2E
    put_stat(0, f2)
    f_hi, f_mid, f_lo = (t.astype(F32) for t in _split3_bf16(f2))
    lane = lax.broadcasted_iota(jnp.int32, (1, LANES), 1)
    packed = jnp.where(lane < N_HEADS, f_hi,
                       jnp.where(lane < 2 * N_HEADS, pltpu.roll(f_mid, N_HEADS, 1), pltpu.roll(f_lo, 2 * N_HEADS, 1)))
    ka_ref[0] = jnp.dot(packed.astype(BF16), sel_ref[...], preferred_element_type=F32).astype(BF16)


def _layer_spec(stacked, layer):
    return pl.BlockSpec((None,) + stacked.shape[1:], lambda *_: (layer, 0, 0), pipeline_mode=pl.Buffered(1))


def _qkv(x, gpre, w_in_all, layer, b_f, *, tm=512):
    b, s, d = x.shape
    w_in = w_in_all[layer]
    wf = jnp.zeros((d, LANES), F32).at[:, :N_HEADS].set(w_in[:, 3 * d:]).astype(BF16)
    bf = jnp.zeros((1, LANES), F32).at[0, :N_HEADS].set(b_f.astype(F32))
    ind = (jnp.arange(d)[:, None] // HEAD_DIM == jnp.arange(LANES)[None, :]).astype(BF16)
    rows = jnp.arange(LANES)[:, None]
    head, term = rows % N_HEADS, rows // N_HEADS
    dest = (head // 2) * LANES + (1 - head % 2) * HEAD_DIM
    sel = -((jnp.arange(d)[None, :] == dest + term) & (term < 3)).astype(BF16)
    kern = functools.partial(_qkv_kernel, tm=tm, d=d)
    act_spec = pl.BlockSpec((1, tm, d), lambda i, j: (i, j, 0))
    return pl.pallas_call(
        kern,
        grid=(b, s // tm),
        in_specs=[act_spec, _const_spec((1, d)), _layer_spec(w_in_all, layer), _const_spec((d, LANES)),
                  _const_spec((1, LANES)), _const_spec((d, LANES)), _const_spec((LANES, d))],
        out_specs=[act_spec, act_spec, act_spec, act_spec,
                   pl.BlockSpec((1, 3, N_HEADS // 2, 2, tm), lambda i, j: (i, 0, 0, 0, j))],
        out_shape=[jax.ShapeDtypeStruct((b, s, d), BF16)] * 4
        + [jax.ShapeDtypeStruct((b, 3, N_HEADS // 2, 2, s), F32)],
        scratch_shapes=[pltpu.VMEM((1, LANES), F32)],
        compiler_params=pltpu.CompilerParams(
            dimension_semantics=("arbitrary", "arbitrary"), vmem_limit_bytes=VMEM_LIMIT_BYTES),
        name="attn_qkv",
    )(x, gpre.reshape(1, -1).astype(F32), w_in_all.astype(F32), wf, bf, ind, sel)


def _attn_kernel(q_ref, k_ref, ka_ref, v_ref, st_ref, o_ref, g_ref, s_ref, nback_ref, *, s, tq):
    tk = tq
    lane = lax.broadcasted_iota(jnp.int32, (1, LANES), 1)
    head_lanes = [lane < HEAD_DIM, lane >= HEAD_DIM]
    bias_ones = [((lane >= HEAD_DIM) & (lane < HEAD_DIM + 3)).astype(BF16), (lane < 3).astype(BF16)]
    row_ids = lax.broadcasted_iota(jnp.int32, (tq, tk), 0)
    col_ids = lax.broadcasted_iota(jnp.int32, (tq, tk), 1)
    causal = col_ids <= row_ids
    pos = lax.broadcasted_iota(jnp.int32, (1, s), 1)

    nk = st_ref[0, 2, 0]
    kmax = jnp.max(nk, axis=1, keepdims=True)
    g_ref[...] = jnp.sqrt(st_ref[0, 1, 0]) * (jnp.sqrt(kmax) + jnp.sqrt(nk)) + st_ref[0, 0, 0]

    def prev_block(q0):
        return q0 - tk if isinstance(q0, int) else pl.multiple_of(q0 - tk, tk)

    def load_q(q0):
        qp = q_ref[0, pl.ds(q0, tq), :]
        return [jnp.where(head_lanes[hh], qp, bias_ones[hh]) for hh in range(2)]

    def scores(qh, k0, masked):
        kp = k_ref[0, pl.ds(k0, tk), :]
        ap = ka_ref[0, pl.ds(k0, tk), :]
        out = []
        for hh in range(2):
            kh = jnp.where(head_lanes[hh], kp, ap)
            sc = lax.dot_general(qh[hh], kh, (((1,), (1,)), ((), ())), preferred_element_type=F32)
            out.append(jnp.where(causal, sc, MASK_VALUE) if masked else sc)
        return out

    def softmax_pv(scs, k0s, carry):
        vps = [v_ref[0, pl.ds(k0, tk), :] for k0 in k0s]
        new = []
        for hh in range(2):
            m, acc = carry[hh]
            m_new = m
            for sc in scs[hh]:
                m_new = jnp.maximum(m_new, jnp.max(sc, axis=1, keepdims=True))
            acc = jnp.exp2(m - m_new) * acc
            for sc, vp in zip(scs[hh], vps):
                p = jnp.exp2(sc - m_new).astype(BF16)
                vh = jnp.where(head_lanes[hh], vp, jnp.ones_like(vp))
                acc = acc + jnp.dot(p, vh, preferred_element_type=F32)
            new.append((m_new, acc))
        return tuple(new)

    def init_carry():
        return tuple((jnp.full((tq, 1), MASK_VALUE, F32), jnp.zeros((tq, LANES), F32)) for _ in range(2))

    for qi in range(1, s // tq):
        gmax = jnp.max(g_ref[:, qi * tq:(qi + 1) * tq], axis=1, keepdims=True)
        live = (gmax - st_ref[0, 0, 0] >= -EXP2_UNDERFLOW) & (pos < qi * tq)
        n_live = jnp.max(jnp.sum(live.astype(F32), axis=1, keepdims=True), axis=0, keepdims=True)
        nback_ref[qi] = jnp.ceil(n_live * (1.0 / tk)).astype(jnp.int32)[0, 0]

    def far_blocks(qi, qh):
        def step(kj, carry):
            k0 = pl.multiple_of(kj * tk, tk)
            return softmax_pv([[sc] for sc in scores(qh, k0, False)], [k0], carry)

        return lax.fori_loop(qi - nback_ref[qi], qi - 1, step, init_carry())

    def tail_scores(slot, q0, qh):
        for bi, (k0, masked) in enumerate([(prev_block(q0), False), (q0, True)]):
            for hh, sc in enumerate(scores(qh, k0, masked)):
                s_ref[slot, hh, bi] = sc

    def tail_finish(slot, q0, carry):
        scs = [[s_ref[slot, hh, 0], s_ref[slot, hh, 1]] for hh in range(2)]
        finish(q0, softmax_pv(scs, [prev_block(q0), q0], carry))

    def finish(q0, carry):
        acc0, acc1 = carry[0][1], carry[1][1]
        out = jnp.where(head_lanes[0], acc0 / pltpu.roll(acc0, HEAD_DIM, 1), acc1 / pltpu.roll(acc1, HEAD_DIM, 1))
        o_ref[0, pl.ds(q0, tq), :] = out.astype(o_ref.dtype)

    finish(0, softmax_pv([[sc] for sc in scores(load_q(0), 0, True)], [0], init_carry()))

    n_q = s // tq
    last = n_q - 1
    tail_scores(0, tq, load_q(tq))

    def pair(i, _):
        qa = 2 * i + 1
        qb = qa + 1
        qn = jnp.minimum(qa + 2, last)
        a0, b0, n0 = (pl.multiple_of(x * tq, tq) for x in (qa, qb, qn))
        qha, qhb, qhn = load_q(a0), load_q(b0), load_q(n0)
        carry_a = far_blocks(qa, qha)
        carry_b = far_blocks(qb, qhb)
        tail_scores(1, b0, qhb)
        tail_finish(0, a0, carry_a)
        tail_scores(0, n0, qhn)
        tail_finish(1, b0, carry_b)
        return 0

    lax.fori_loop(0, last // 2, pair, 0)
    if last % 2 == 1:
        q0 = last * tq
        tail_finish(0, q0, far_blocks(last, load_q(q0)))


def _attention(q, k, kaug, v, stats, *, tq=512):
    b, s, d = q.shape
    n_groups = d // LANES
    heads_per_group = LANES // HEAD_DIM
    assert heads_per_group == 2 and stats.shape == (b, 3, n_groups, heads_per_group, s)
    kern = functools.partial(_attn_kernel, s=s, tq=tq)
    col_spec = pl.BlockSpec((1, s, LANES), lambda i, j: (i, 0, j))
    return pl.pallas_call(
        kern,
        grid=(b, n_groups),
        in_specs=[col_spec, col_spec, col_spec, col_spec,
                  pl.BlockSpec((1, 3, 1, heads_per_group, s), lambda i, j: (i, 0, j, 0, 0))],
        out_specs=col_spec,
        out_shape=jax.ShapeDtypeStruct((b, s, d), BF16),
        scratch_shapes=[pltpu.VMEM((heads_per_group, s), F32),
                        pltpu.VMEM((2, heads_per_group, 2, tq, tq), F32),
                        pltpu.SMEM((s // tq,), jnp.int32)],
        compiler_params=pltpu.CompilerParams(
            dimension_semantics=("arbitrary", "arbitrary"), vmem_limit_bytes=VMEM_LIMIT_BYTES),
        name="fox_attention",
    )(q, k, kaug, v, stats)


def kernel(x, g_mix_pre, g_mix_post, g_ffn_pre, g_ffn_post, conv_pw1_w, conv_pw1_b, conv_dw_w, conv_dw_b,
           conv_ln_g, conv_ln_b, conv_pw2_w, conv_pw2_b, attn_w_in, attn_b_f, attn_w_o, mlp_w_up, mlp_w_down):
    b, s, d = x.shape
    depth = g_mix_pre.shape[0]
    for i in range(depth):
        j = i // 2
        attn = None
        if i % 2 == 0:
            x = _conv_mixer(x, g_mix_pre[i], conv_pw1_w[j], conv_pw1_b[j], conv_dw_w[j], conv_dw_b[j],
                            conv_ln_g[j], conv_ln_b[j], conv_pw2_w[j], conv_pw2_b[j], g_mix_post[i])
        else:
            q, k, kaug, v, stats = _qkv(x, g_mix_pre[i], attn_w_in, j, attn_b_f[j])
            attn = (_attention(q, k, kaug, v, stats).reshape(b * s, d), attn_w_o[j], g_mix_post[i])
        x = _mlp(x.reshape(b * s, d), g_ffn_pre[i], mlp_w_up, mlp_w_down, i, g_ffn_post[i], attn).reshape(b, s, d)
    return x
```

```python
import functools

import jax
import jax.numpy as jnp
from jax import lax
from jax.experimental import pallas as pl
from jax.experimental.pallas import tpu as pltpu

N_HEADS = 16
HEAD_DIM = 64
CONV_WIDTH = 31
RMS_EPS = 1e-6
LN_EPS = 1e-5
MASK_VALUE = -1e30

LANES = 128
SUBLANES = 8
CONV_ROWS = 128
CONV_COLS = 256
FF_CHUNK = 1024
CONV_HALO = 32
VMEM_LIMIT_BYTES = 56 * 1024 * 1024
EXP2_UNDERFLOW = 160.0
LOG2E = 1.4426950408889634
NORM_SLACK = 1.01

F32 = jnp.float32
BF16 = jnp.bfloat16


def _rms(x, g):
    return x * lax.rsqrt(jnp.mean(x * x, axis=-1, keepdims=True) + RMS_EPS) * g


def _const_spec(shape):
    nd = len(shape)
    return pl.BlockSpec(shape, lambda *_: (0,) * nd, pipeline_mode=pl.Buffered(1))


def _conv_mixer_kernel(x_ref, gpre_ref, w1_ref, b1_ref, dww_ref, dwb_ref, lng_ref, lnb_ref,
                       w2_ref, b2_ref, gpost_ref, o_ref, ubuf_ref, ybuf_ref, *, tm, d):
    x = x_ref[0]
    hb = _rms(x, gpre_ref[...]).astype(BF16)

    @pl.when(pl.program_id(1) == 0)
    def _():
        ubuf_ref[0:CONV_HALO, :] = jnp.zeros((CONV_HALO, d), F32)
        ubuf_ref[CONV_HALO + tm:CONV_HALO + tm + SUBLANES, :] = jnp.zeros((SUBLANES, d), F32)

    def conv_chunk(r0, c0):
        acc = jnp.broadcast_to(dwb_ref[:, c0:c0 + LANES], (CONV_ROWS, LANES))
        for r in range(SUBLANES):
            z = None
            for o in range(r if r >= 2 else r + SUBLANES, CONV_WIDTH + 2, SUBLANES):
                a8 = r0 + o - r
                term = ubuf_ref[a8:a8 + CONV_ROWS + SUBLANES, c0:c0 + LANES] * dww_ref[o - 2:o - 1, c0:c0 + LANES]
                z = term if z is None else z + term
            acc = acc + z[r:r + CONV_ROWS]
        ybuf_ref[r0:r0 + CONV_ROWS, c0:c0 + LANES] = acc

    for g0 in range(0, d, CONV_COLS):
        cols = slice(g0, g0 + CONV_COLS)
        gate_cols = slice(d + g0, d + g0 + CONV_COLS)
        a_u = jnp.dot(hb, w1_ref[:, cols], preferred_element_type=F32) + b1_ref[:, cols]
        a_g = jnp.dot(hb, w1_ref[:, gate_cols], preferred_element_type=F32) + b1_ref[:, gate_cols]
        ubuf_ref[CONV_HALO:CONV_HALO + tm, cols] = a_u * jax.nn.sigmoid(a_g)
        for c0 in range(g0, g0 + CONV_COLS, LANES):
            for r0 in range(0, tm, CONV_ROWS):
                conv_chunk(r0, c0)
        ubuf_ref[0:CONV_HALO, cols] = ubuf_ref[tm:tm + CONV_HALO, cols]

    y = ybuf_ref[...]
    mu = jnp.mean(y, axis=-1, keepdims=True)
    yc = y - mu
    var = jnp.mean(yc * yc, axis=-1, keepdims=True)
    yn = yc * lax.rsqrt(var + LN_EPS) * lng_ref[...] + lnb_ref[...]
    act = yn * jax.nn.sigmoid(yn)
    m = jnp.dot(act.astype(BF16), w2_ref[...], preferred_element_type=F32) + b2_ref[...]
    o_ref[0] = x + _rms(m, gpost_ref[...])


def _conv_mixer(x, gpre, w1, b1, dww, dwb, lng, lnb, w2, b2, gpost, *, tm=512):
    b, s, d = x.shape
    row = lambda v: v.reshape(1, -1).astype(F32)
    kern = functools.partial(_conv_mixer_kernel, tm=tm, d=d)
    return pl.pallas_call(
        kern,
        grid=(b, s // tm),
        in_specs=[
            pl.BlockSpec((1, tm, d), lambda i, j: (i, j, 0)),
            _const_spec((1, d)), _const_spec((d, 2 * d)), _const_spec((1, 2 * d)),
            _const_spec((CONV_WIDTH, d)), _const_spec((1, d)), _const_spec((1, d)), _const_spec((1, d)),
            _const_spec((d, d)), _const_spec((1, d)), _const_spec((1, d)),
        ],
        out_specs=pl.BlockSpec((1, tm, d), lambda i, j: (i, j, 0)),
        out_shape=jax.ShapeDtypeStruct(x.shape, F32),
        scratch_shapes=[pltpu.VMEM((CONV_HALO + tm + SUBLANES, d), F32), pltpu.VMEM((tm, d), F32)],
        compiler_params=pltpu.CompilerParams(
            dimension_semantics=("arbitrary", "arbitrary"), vmem_limit_bytes=VMEM_LIMIT_BYTES),
        name="conv_mixer",
    )(x, row(gpre), w1.astype(BF16), row(b1), dww.astype(F32), row(dwb), row(lng), row(lnb),
      w2.astype(BF16), row(b2), row(gpost))


def _mlp_body(x, gpre_ref, wup_ref, wdown_ref, gpost_ref, o_ref):
    hb = _rms(x, gpre_ref[...]).astype(BF16)
    down = None
    for c0 in range(0, wup_ref.shape[1], FF_CHUNK):
        up = jnp.dot(hb, wup_ref[:, c0:c0 + FF_CHUNK].astype(BF16), preferred_element_type=F32)
        r = jnp.maximum(up, 0.0)
        part = jnp.dot((r * r).astype(BF16), wdown_ref[c0:c0 + FF_CHUNK, :].astype(BF16), preferred_element_type=F32)
        down = part if down is None else down + part
    o_ref[...] = x + _rms(down, gpost_ref[...])


def _mlp_kernel(x_ref, gpre_ref, wup_ref, wdown_ref, gpost_ref, o_ref):
    _mlp_body(x_ref[...], gpre_ref, wup_ref, wdown_ref, gpost_ref, o_ref)


def _attn_out_mlp_kernel(x_ref, a_ref, wo_ref, gmix_ref, gpre_ref, wup_ref, wdown_ref, gpost_ref, o_ref):
    m = jnp.dot(a_ref[...], wo_ref[...], preferred_element_type=F32)
    _mlp_body(x_ref[...] + _rms(m, gmix_ref[...]), gpre_ref, wup_ref, wdown_ref, gpost_ref, o_ref)


def _mlp(x2d, gpre, wup_all, wdown_all, layer, gpost, attn=None, *, tm=512):
    t, d = x2d.shape
    row = lambda v: v.reshape(1, -1).astype(F32)
    act_spec = pl.BlockSpec((tm, d), lambda i: (i, 0))
    mlp_specs = [_const_spec((1, d)), _layer_spec(wup_all, layer), _layer_spec(wdown_all, layer), _const_spec((1, d))]
    mlp_args = (row(gpre), wup_all.astype(F32), wdown_all.astype(F32), row(gpost))
    if attn is None:
        kern, in_specs, args = _mlp_kernel, [act_spec] + mlp_specs, (x2d,) + mlp_args
    else:
        a2d, wo, gmix = attn
        kern = _attn_out_mlp_kernel
        in_specs = [act_spec, act_spec, _const_spec((d, d)), _const_spec((1, d))] + mlp_specs
        args = (x2d, a2d, wo.astype(BF16), row(gmix)) + mlp_args
    return pl.pallas_call(
        kern,
        grid=(t // tm,),
        in_specs=in_specs,
        out_specs=act_spec,
        out_shape=jax.ShapeDtypeStruct(x2d.shape, F32),
        compiler_params=pltpu.CompilerParams(
            dimension_semantics=("arbitrary",), vmem_limit_bytes=VMEM_LIMIT_BYTES),
        name="sqrelu_mlp" if attn is None else "attn_out_mlp",
    )(*args)


def _split3_bf16(x):
    hi = x.astype(BF16)
    r1 = x - hi.astype(F32)
    mid = r1.astype(BF16)
    lo = (r1 - mid.astype(F32)).astype(BF16)
    return hi, mid, lo


def _head_sq_norm_bound(xb, ind):
    xf = xb.astype(F32)
    nsq = jnp.dot((xf * xf).astype(BF16), ind, preferred_element_type=F32)
    return nsq * NORM_SLACK


def _qkv_kernel(x_ref, gpre_ref, win_ref, wf_ref, bf_ref, ind_ref, sel_ref, q_ref, k_ref, ka_ref, v_ref, st_ref,
                carry_ref, *, tm, d):
    x = x_ref[0]
    hb = _rms(x, gpre_ref[...]).astype(BF16)
    proj = jnp.dot(hb, win_ref[:, :3 * d].astype(BF16), preferred_element_type=F32)
    qb = (proj[:, :d] * (HEAD_DIM ** -0.5 * LOG2E)).astype(BF16)
    kb = proj[:, d:2 * d].astype(BF16)
    q_ref[0] = qb
    k_ref[0] = kb
    v_ref[0] = proj[:, 2 * d:].astype(BF16)

    def put_stat(t, per_head):
        rows = per_head.T
        for g in range(N_HEADS // 2):
            st_ref[0, t, g] = rows[2 * g:2 * g + 2, :]

    put_stat(1, _head_sq_norm_bound(qb, ind_ref[...]))
    put_stat(2, _head_sq_norm_bound(kb, ind_ref[...]))

    f_logit = jnp.dot(hb, wf_ref[...], preferred_element_type=F32) + bf_ref[...]
    log_f = jax.nn.log_sigmoid(f_logit)

    @pl.when(pl.program_id(1) == 0)
    def _():
        carry_ref[...] = jnp.zeros_like(carry_ref)

    rows = lax.broadcasted_iota(jnp.int32, (tm, tm), 0)
    cols = lax.broadcasted_iota(jnp.int32, (tm, tm), 1)
    tri = (rows >= cols).astype(BF16)
    hi, mid, lo = _split3_bf16(log_f)
    csum = (jnp.dot(tri, hi, preferred_element_type=F32) + jnp.dot(tri, mid, preferred_element_type=F32)
            + jnp.dot(tri, lo, preferred_element_type=F32))
    f_cum = csum + carry_ref[...]
    carry_ref[...] = f_cum[tm - 1:tm, :]
    f2 = f_cum * LOG2E
    put_stat(0, f2)
    f_hi, f_mid, f_lo = (t.astype(F32) for t in _split3_bf16(f2))
    lane = lax.broadcasted_iota(jnp.int32, (1, LANES), 1)
    packed = jnp.where(lane < N_HEADS, f_hi,
                       jnp.where(lane < 2 * N_HEADS, pltpu.roll(f_mid, N_HEADS, 1), pltpu.roll(f_lo, 2 * N_HEADS, 1)))
    ka_ref[0] = jnp.dot(packed.astype(BF16), sel_ref[...], preferred_element_type=F32).astype(BF16)


def _layer_spec(stacked, layer):
    return pl.BlockSpec((None,) + stacked.shape[1:], lambda *_: (layer, 0, 0), pipeline_mode=pl.Buffered(1))


def _qkv(x, gpre, w_in_all, layer, b_f, *, tm=512):
    b, s, d = x.shape
    w_in = w_in_all[layer]
    wf = jnp.zeros((d, LANES), F32).at[:, :N_HEADS].set(w_in[:, 3 * d:]).astype(BF16)
    bf = jnp.zeros((1, LANES), F32).at[0, :N_HEADS].set(b_f.astype(F32))
    ind = (jnp.arange(d)[:, None] // HEAD_DIM == jnp.arange(LANES)[None, :]).astype(BF16)
    rows = jnp.arange(LANES)[:, None]
    head, term = rows % N_HEADS, rows // N_HEADS
    dest = (head // 2) * LANES + (1 - head % 2) * HEAD_DIM
    sel = -((jnp.arange(d)[None, :] == dest + term) & (term < 3)).astype(BF16)
    kern = functools.partial(_qkv_kernel, tm=tm, d=d)
    act_spec = pl.BlockSpec((1, tm, d), lambda i, j: (i, j, 0))
    return pl.pallas_call(
        kern,
        grid=(b, s // tm),
        in_specs=[act_spec, _const_spec((1, d)), _layer_spec(w_in_all, layer), _const_spec((d, LANES)),
                  _const_spec((1, LANES)), _const_spec((d, LANES)), _const_spec((LANES, d))],
        out_specs=[act_spec, act_spec, act_spec, act_spec,
                   pl.BlockSpec((1, 3, N_HEADS // 2, 2, tm), lambda i, j: (i, 0, 0, 0, j))],
        out_shape=[jax.ShapeDtypeStruct((b, s, d), BF16)] * 4
        + [jax.ShapeDtypeStruct((b, 3, N_HEADS // 2, 2, s), F32)],
        scratch_shapes=[pltpu.VMEM((1, LANES), F32)],
        compiler_params=pltpu.CompilerParams(
            dimension_semantics=("arbitrary", "arbitrary"), vmem_limit_bytes=VMEM_LIMIT_BYTES),
        name="attn_qkv",
    )(x, gpre.reshape(1, -1).astype(F32), w_in_all.astype(F32), wf, bf, ind, sel)


def _attn_kernel(q_ref, k_ref, ka_ref, v_ref, st_ref, o_ref, g_ref, s_ref, nback_ref, *, s, tq):
    tk = tq
    lane = lax.broadcasted_iota(jnp.int32, (1, LANES), 1)
    head_lanes = [lane < HEAD_DIM, lane >= HEAD_DIM]
    bias_ones = [((lane >= HEAD_DIM) & (lane < HEAD_DIM + 3)).astype(BF16), (lane < 3).astype(BF16)]
    row_ids = lax.broadcasted_iota(jnp.int32, (tq, tk), 0)
    col_ids = lax.broadcasted_iota(jnp.int32, (tq, tk), 1)
    causal = col_ids <= row_ids
    pos = lax.broadcasted_iota(jnp.int32, (1, s), 1)

    nk = st_ref[0, 2, 0]
    kmax = jnp.max(nk, axis=1, keepdims=True)
    g_ref[...] = jnp.sqrt(st_ref[0, 1, 0]) * (jnp.sqrt(kmax) + jnp.sqrt(nk)) + st_ref[0, 0, 0]

    def prev_block(q0):
        return q0 - tk if isinstance(q0, int) else pl.multiple_of(q0 - tk, tk)

    def load_q(q0):
        qp = q_ref[0, pl.ds(q0, tq), :]
        return [jnp.where(head_lanes[hh], qp, bias_ones[hh]) for hh in range(2)]

    def scores(qh, k0, masked, width=tk):
        kp = k_ref[0, pl.ds(k0, width), :]
        ap = ka_ref[0, pl.ds(k0, width), :]
        out = []
        for hh in range(2):
            kh = jnp.where(head_lanes[hh], kp, ap)
            sc = lax.dot_general(qh[hh], kh, (((1,), (1,)), ((), ())), preferred_element_type=F32)
            out.append(jnp.where(causal, sc, MASK_VALUE) if masked else sc)
        return out

    def softmax_pv(scs, k0s, carry):
        vps = [v_ref[0, pl.ds(k0, scs[0][i].shape[1]), :] for i, k0 in enumerate(k0s)]
        new = []
        for hh in range(2):
            m, acc = carry[hh]
            m_new = m
            for sc in scs[hh]:
                m_new = jnp.maximum(m_new, jnp.max(sc, axis=1, keepdims=True))
            acc = jnp.exp2(m - m_new) * acc
            for sc, vp in zip(scs[hh], vps):
                p = jnp.exp2(sc - m_new).astype(BF16)
                vh = jnp.where(head_lanes[hh], vp, jnp.ones_like(vp))
                acc = acc + jnp.dot(p, vh, preferred_element_type=F32)
            new.append((m_new, acc))
        return tuple(new)

    def init_carry():
        return tuple((jnp.full((tq, 1), MASK_VALUE, F32), jnp.zeros((tq, LANES), F32)) for _ in range(2))

    far_w = tk // 2
    for qi in range(1, s // tq):
        gmax = jnp.max(g_ref[:, qi * tq:(qi + 1) * tq], axis=1, keepdims=True)
        live = (gmax - st_ref[0, 0, 0] >= -EXP2_UNDERFLOW) & (pos < qi * tq)
        n_live = jnp.max(jnp.sum(live.astype(F32), axis=1, keepdims=True), axis=0, keepdims=True)
        n_far = jnp.ceil(jnp.maximum(n_live - tk, 0.0) * (1.0 / far_w))
        nback_ref[qi] = n_far.astype(jnp.int32)[0, 0]

    def far_blocks(qi, qh):
        first_far = (qi - 1) * (tk // far_w)

        def step(kj, carry):
            k0 = pl.multiple_of(kj * far_w, far_w)
            return softmax_pv([[sc] for sc in scores(qh, k0, False, far_w)], [k0], carry)

        return lax.fori_loop(first_far - nback_ref[qi], first_far, step, init_carry())

    def tail_scores(slot, q0, qh):
        for bi, (k0, masked) in enumerate([(prev_block(q0), False), (q0, True)]):
            for hh, sc in enumerate(scores(qh, k0, masked)):
                s_ref[slot, hh, bi] = sc

    def tail_finish(slot, q0, carry):
        scs = [[s_ref[slot, hh, 0], s_ref[slot, hh, 1]] for hh in range(2)]
        finish(q0, softmax_pv(scs, [prev_block(q0), q0], carry))

    def finish(q0, carry):
        acc0, acc1 = carry[0][1], carry[1][1]
        out = jnp.where(head_lanes[0], acc0 / pltpu.roll(acc0, HEAD_DIM, 1), acc1 / pltpu.roll(acc1, HEAD_DIM, 1))
        o_ref[0, pl.ds(q0, tq), :] = out.astype(o_ref.dtype)

    finish(0, softmax_pv([[sc] for sc in scores(load_q(0), 0, True)], [0], init_carry()))

    n_q = s // tq
    last = n_q - 1
    tail_scores(0, tq, load_q(tq))

    def pair(i, _):
        qa = 2 * i + 1
        qb = qa + 1
        qn = jnp.minimum(qa + 2, last)
        a0, b0, n0 = (pl.multiple_of(x * tq, tq) for x in (qa, qb, qn))
        qha, qhb, qhn = load_q(a0), load_q(b0), load_q(n0)
        carry_a = far_blocks(qa, qha)
        carry_b = far_blocks(qb, qhb)
        tail_scores(1, b0, qhb)
        tail_finish(0, a0, carry_a)
        tail_scores(0, n0, qhn)
        tail_finish(1, b0, carry_b)
        return 0

    lax.fori_loop(0, last // 2, pair, 0)
    if last % 2 == 1:
        q0 = last * tq
        tail_finish(0, q0, far_blocks(last, load_q(q0)))


def _attention(q, k, kaug, v, stats, *, tq=512):
    b, s, d = q.shape
    n_groups = d // LANES
    heads_per_group = LANES // HEAD_DIM
    assert heads_per_group == 2 and stats.shape == (b, 3, n_groups, heads_per_group, s)
    kern = functools.partial(_attn_kernel, s=s, tq=tq)
    col_spec = pl.BlockSpec((1, s, LANES), lambda i, j: (i, 0, j))
    return pl.pallas_call(
        kern,
        grid=(b, n_groups),
        in_specs=[col_spec, col_spec, col_spec, col_spec,
                  pl.BlockSpec((1, 3, 1, heads_per_group, s), lambda i, j: (i, 0, j, 0, 0))],
        out_specs=col_spec,
        out_shape=jax.ShapeDtypeStruct((b, s, d), BF16),
        scratch_shapes=[pltpu.VMEM((heads_per_group, s), F32),
                        pltpu.VMEM((2, heads_per_group, 2, tq, tq), F32),
                        pltpu.SMEM((s // tq,), jnp.int32)],
        compiler_params=pltpu.CompilerParams(
            dimension_semantics=("arbitrary", "arbitrary"), vmem_limit_bytes=VMEM_LIMIT_BYTES),
        name="fox_attention",
    )(q, k, kaug, v, stats)


def kernel(x, g_mix_pre, g_mix_post, g_ffn_pre, g_ffn_post, conv_pw1_w, conv_pw1_b, conv_dw_w, conv_dw_b,
           conv_ln_g, conv_ln_b, conv_pw2_w, conv_pw2_b, attn_w_in, attn_b_f, attn_w_o, mlp_w_up, mlp_w_down):
    b, s, d = x.shape
    depth = g_mix_pre.shape[0]
    for i in range(depth):
        j = i // 2
        attn = None
        if i % 2 == 0:
            x = _conv_mixer(x, g_mix_pre[i], conv_pw1_w[j], conv_pw1_b[j], conv_dw_w[j], conv_dw_b[j],
                            conv_ln_g[j], conv_ln_b[j], conv_pw2_w[j], conv_pw2_b[j], g_mix_post[i])
        else:
            q, k, kaug, v, stats = _qkv(x, g_mix_pre[i], attn_w_in, j, attn_b_f[j])
            attn = (_attention(q, k, kaug, v, stats).reshape(b * s, d), attn_w_o[j], g_mix_post[i])
        x = _mlp(x.reshape(b * s, d), g_ffn_pre[i], mlp_w_up, mlp_w_down, i, g_ffn_post[i], attn).reshape(b, s, d)
    return x
```

```python
import functools

import jax
import jax.numpy as jnp
from jax import lax
from jax.experimental import pallas as pl
from jax.experimental.pallas import tpu as pltpu

N_HEADS = 16
HEAD_DIM = 64
CONV_WIDTH = 31
RMS_EPS = 1e-6
LN_EPS = 1e-5
MASK_VALUE = -1e30

LANES = 128
SUBLANES = 8
CONV_ROWS = 128
CONV_COLS = 256
FF_CHUNK = 1024
CONV_HALO = 32
VMEM_LIMIT_BYTES = 56 * 1024 * 1024
EXP2_UNDERFLOW = 160.0
LOG2E = 1.4426950408889634
NORM_SLACK = 1.01

F32 = jnp.float32
BF16 = jnp.bfloat16


def _rms(x, g):
    return x * lax.rsqrt(jnp.mean(x * x, axis=-1, keepdims=True) + RMS_EPS) * g


def _const_spec(shape):
    nd = len(shape)
    return pl.BlockSpec(shape, lambda *_: (0,) * nd, pipeline_mode=pl.Buffered(1))


def _conv_mixer_kernel(x_ref, gpre_ref, w1_ref, b1_ref, dww_ref, dwb_ref, lng_ref, lnb_ref,
                       w2_ref, b2_ref, gpost_ref, o_ref, ubuf_ref, ybuf_ref, *, tm, d):
    x = x_ref[0]
    hb = _rms(x, gpre_ref[...]).astype(BF16)

    @pl.when(pl.program_id(1) == 0)
    def _():
        ubuf_ref[0:CONV_HALO, :] = jnp.zeros((CONV_HALO, d), F32)
        ubuf_ref[CONV_HALO + tm:CONV_HALO + tm + SUBLANES, :] = jnp.zeros((SUBLANES, d), F32)

    def conv_chunk(r0, c0):
        acc = jnp.broadcast_to(dwb_ref[:, c0:c0 + LANES], (CONV_ROWS, LANES))
        for r in range(SUBLANES):
            z = None
            for o in range(r if r >= 2 else r + SUBLANES, CONV_WIDTH + 2, SUBLANES):
                a8 = r0 + o - r
                term = ubuf_ref[a8:a8 + CONV_ROWS + SUBLANES, c0:c0 + LANES] * dww_ref[o - 2:o - 1, c0:c0 + LANES]
                z = term if z is None else z + term
            acc = acc + z[r:r + CONV_ROWS]
        ybuf_ref[r0:r0 + CONV_ROWS, c0:c0 + LANES] = acc

    for g0 in range(0, d, CONV_COLS):
        cols = slice(g0, g0 + CONV_COLS)
        gate_cols = slice(d + g0, d + g0 + CONV_COLS)
        a_u = jnp.dot(hb, w1_ref[:, cols], preferred_element_type=F32) + b1_ref[:, cols]
        a_g = jnp.dot(hb, w1_ref[:, gate_cols], preferred_element_type=F32) + b1_ref[:, gate_cols]
        ubuf_ref[CONV_HALO:CONV_HALO + tm, cols] = a_u * jax.nn.sigmoid(a_g)
        for c0 in range(g0, g0 + CONV_COLS, LANES):
            for r0 in range(0, tm, CONV_ROWS):
                conv_chunk(r0, c0)
        ubuf_ref[0:CONV_HALO, cols] = ubuf_ref[tm:tm + CONV_HALO, cols]

    y = ybuf_ref[...]
    mu = jnp.mean(y, axis=-1, keepdims=True)
    yc = y - mu
    var = jnp.mean(yc * yc, axis=-1, keepdims=True)
    yn = yc * lax.rsqrt(var + LN_EPS) * lng_ref[...] + lnb_ref[...]
    act = yn * jax.nn.sigmoid(yn)
    m = jnp.dot(act.astype(BF16), w2_ref[...], preferred_element_type=F32) + b2_ref[...]
    o_ref[0] = x + _rms(m, gpost_ref[...])


def _conv_mixer(x, gpre, w1, b1, dww, dwb, lng, lnb, w2, b2, gpost, *, tm=512):
    b, s, d = x.shape
    row = lambda v: v.reshape(1, -1).astype(F32)
    kern = functools.partial(_conv_mixer_kernel, tm=tm, d=d)
    return pl.pallas_call(
        kern,
        grid=(b, s // tm),
        in_specs=[
            pl.BlockSpec((1, tm, d), lambda i, j: (i, j, 0)),
            _const_spec((1, d)), _const_spec((d, 2 * d)), _const_spec((1, 2 * d)),
            _const_spec((CONV_WIDTH, d)), _const_spec((1, d)), _const_spec((1, d)), _const_spec((1, d)),
            _const_spec((d, d)), _const_spec((1, d)), _const_spec((1, d)),
        ],
        out_specs=pl.BlockSpec((1, tm, d), lambda i, j: (i, j, 0)),
        out_shape=jax.ShapeDtypeStruct(x.shape, F32),
        scratch_shapes=[pltpu.VMEM((CONV_HALO + tm + SUBLANES, d), F32), pltpu.VMEM((tm, d), F32)],
        compiler_params=pltpu.CompilerParams(
            dimension_semantics=("arbitrary", "arbitrary"), vmem_limit_bytes=VMEM_LIMIT_BYTES),
        name="conv_mixer",
    )(x, row(gpre), w1.astype(BF16), row(b1), dww.astype(F32), row(dwb), row(lng), row(lnb),
      w2.astype(BF16), row(b2), row(gpost))


def _mlp_body(x, gpre_ref, wup_ref, wdown_ref, gpost_ref, o_ref):
    hb = _rms(x, gpre_ref[...]).astype(BF16)
    down = None
    for c0 in range(0, wup_ref.shape[1], FF_CHUNK):
        up = jnp.dot(hb, wup_ref[:, c0:c0 + FF_CHUNK].astype(BF16), preferred_element_type=F32)
        r = jnp.maximum(up, 0.0)
        part = jnp.dot((r * r).astype(BF16), wdown_ref[c0:c0 + FF_CHUNK, :].astype(BF16), preferred_element_type=F32)
        down = part if down is None else down + part
    o_ref[...] = x + _rms(down, gpost_ref[...])


def _mlp_kernel(x_ref, gpre_ref, wup_ref, wdown_ref, gpost_ref, o_ref):
    _mlp_body(x_ref[...], gpre_ref, wup_ref, wdown_ref, gpost_ref, o_ref)


def _attn_out_mlp_kernel(x_ref, a_ref, wo_ref, gmix_ref, gpre_ref, wup_ref, wdown_ref, gpost_ref, o_ref):
    m = jnp.dot(a_ref[...], wo_ref[...], preferred_element_type=F32)
    _mlp_body(x_ref[...] + _rms(m, gmix_ref[...]), gpre_ref, wup_ref, wdown_ref, gpost_ref, o_ref)


def _mlp(x2d, gpre, wup_all, wdown_all, layer, gpost, attn=None, *, tm=512):
    t, d = x2d.shape
    row = lambda v: v.reshape(1, -1).astype(F32)
    act_spec = pl.BlockSpec((tm, d), lambda i: (i, 0))
    mlp_specs = [_const_spec((1, d)), _layer_spec(wup_all, layer), _layer_spec(wdown_all, layer), _const_spec((1, d))]
    mlp_args = (row(gpre), wup_all.astype(F32), wdown_all.astype(F32), row(gpost))
    if attn is None:
        kern, in_specs, args = _mlp_kernel, [act_spec] + mlp_specs, (x2d,) + mlp_args
    else:
        a2d, wo, gmix = attn
        kern = _attn_out_mlp_kernel
        in_specs = [act_spec, act_spec, _const_spec((d, d)), _const_spec((1, d))] + mlp_specs
        args = (x2d, a2d, wo.astype(BF16), row(gmix)) + mlp_args
    return pl.pallas_call(
        kern,
        grid=(t // tm,),
        in_specs=in_specs,
        out_specs=act_spec,
        out_shape=jax.ShapeDtypeStruct(x2d.shape, F32),
        compiler_params=pltpu.CompilerParams(
            dimension_semantics=("arbitrary",), vmem_limit_bytes=VMEM_LIMIT_BYTES),
        name="sqrelu_mlp" if attn is None else "attn_out_mlp",
    )(*args)


def _split3_bf16(x):
    hi = x.astype(BF16)
    r1 = x - hi.astype(F32)
    mid = r1.astype(BF16)
    lo = (r1 - mid.astype(F32)).astype(BF16)
    return hi, mid, lo


def _head_sq_norm_bound(xb, ind):
    xf = xb.astype(F32)
    nsq = jnp.dot((xf * xf).astype(BF16), ind, preferred_element_type=F32)
    return nsq * NORM_SLACK


def _qkv_kernel(x_ref, gpre_ref, win_ref, wf_ref, bf_ref, ind_ref, sel_ref, q_ref, k_ref, ka_ref, v_ref, st_ref,
                carry_ref, *, tm, d):
    x = x_ref[0]
    hb = _rms(x, gpre_ref[...]).astype(BF16)
    proj = jnp.dot(hb, win_ref[:, :3 * d].astype(BF16), preferred_element_type=F32)
    qb = (proj[:, :d] * (HEAD_DIM ** -0.5 * LOG2E)).astype(BF16)
    kb = proj[:, d:2 * d].astype(BF16)
    q_ref[0] = qb
    k_ref[0] = kb
    v_ref[0] = proj[:, 2 * d:].astype(BF16)

    def put_stat(t, per_head):
        rows = per_head.T
        for g in range(N_HEADS // 2):
            st_ref[0, t, g] = rows[2 * g:2 * g + 2, :]

    put_stat(1, _head_sq_norm_bound(qb, ind_ref[...]))
    put_stat(2, _head_sq_norm_bound(kb, ind_ref[...]))

    f_logit = jnp.dot(hb, wf_ref[...], preferred_element_type=F32) + bf_ref[...]
    log_f = jax.nn.log_sigmoid(f_logit)

    @pl.when(pl.program_id(1) == 0)
    def _():
        carry_ref[...] = jnp.zeros_like(carry_ref)

    rows = lax.broadcasted_iota(jnp.int32, (tm, tm), 0)
    cols = lax.broadcasted_iota(jnp.int32, (tm, tm), 1)
    tri = (rows >= cols).astype(BF16)
    hi, mid, lo = _split3_bf16(log_f)
    csum = (jnp.dot(tri, hi, preferred_element_type=F32) + jnp.dot(tri, mid, preferred_element_type=F32)
            + jnp.dot(tri, lo, preferred_element_type=F32))
    f_cum = csum + carry_ref[...]
    carry_ref[...] = f_cum[tm - 1:tm, :]
    f2 = f_cum * LOG2E
    put_stat(0, f2)
    f_hi, f_mid, f_lo = (t.astype(F32) for t in _split3_bf16(f2))
    lane = lax.broadcasted_iota(jnp.int32, (1, LANES), 1)
    packed = jnp.where(lane < N_HEADS, f_hi,
                       jnp.where(lane < 2 * N_HEADS, pltpu.roll(f_mid, N_HEADS, 1), pltpu.roll(f_lo, 2 * N_HEADS, 1)))
    ka_ref[0] = jnp.dot(packed.astype(BF16), sel_ref[...], preferred_element_type=F32).astype(BF16)


def _layer_spec(stacked, layer):
    return pl.BlockSpec((None,) + stacked.shape[1:], lambda *_: (layer, 0, 0), pipeline_mode=pl.Buffered(1))


def _qkv(x, gpre, w_in_all, layer, b_f, *, tm=512):
    b, s, d = x.shape
    w_in = w_in_all[layer]
    wf = jnp.zeros((d, LANES), F32).at[:, :N_HEADS].set(w_in[:, 3 * d:]).astype(BF16)
    bf = jnp.zeros((1, LANES), F32).at[0, :N_HEADS].set(b_f.astype(F32))
    ind = (jnp.arange(d)[:, None] // HEAD_DIM == jnp.arange(LANES)[None, :]).astype(BF16)
    rows = jnp.arange(LANES)[:, None]
    head, term = rows % N_HEADS, rows // N_HEADS
    dest = (head // 2) * LANES + (1 - head % 2) * HEAD_DIM
    sel = -((jnp.arange(d)[None, :] == dest + term) & (term < 3)).astype(BF16)
    kern = functools.partial(_qkv_kernel, tm=tm, d=d)
    act_spec = pl.BlockSpec((1, tm, d), lambda i, j: (i, j, 0))
    return pl.pallas_call(
        kern,
        grid=(b, s // tm),
        in_specs=[act_spec, _const_spec((1, d)), _layer_spec(w_in_all, layer), _const_spec((d, LANES)),
                  _const_spec((1, LANES)), _const_spec((d, LANES)), _const_spec((LANES, d))],
        out_specs=[act_spec, act_spec, act_spec, act_spec,
                   pl.BlockSpec((1, 3, N_HEADS // 2, 2, tm), lambda i, j: (i, 0, 0, 0, j))],
        out_shape=[jax.ShapeDtypeStruct((b, s, d), BF16)] * 4
        + [jax.ShapeDtypeStruct((b, 3, N_HEADS // 2, 2, s), F32)],
        scratch_shapes=[pltpu.VMEM((1, LANES), F32)],
        compiler_params=pltpu.CompilerParams(
            dimension_semantics=("arbitrary", "arbitrary"), vmem_limit_bytes=VMEM_LIMIT_BYTES),
        name="attn_qkv",
    )(x, gpre.reshape(1, -1).astype(F32), w_in_all.astype(F32), wf, bf, ind, sel)


def _attn_kernel(q_ref, k_ref, ka_ref, v_ref, st_ref, o_ref, g_ref, s_ref, m_st, acc_st, nback_ref, *, s, tq):
    tk = tq
    lane = lax.broadcasted_iota(jnp.int32, (1, LANES), 1)
    head_lanes = [lane < HEAD_DIM, lane >= HEAD_DIM]
    bias_ones = [((lane >= HEAD_DIM) & (lane < HEAD_DIM + 3)).astype(BF16), (lane < 3).astype(BF16)]
    row_ids = lax.broadcasted_iota(jnp.int32, (tq, tk), 0)
    col_ids = lax.broadcasted_iota(jnp.int32, (tq, tk), 1)
    causal = col_ids <= row_ids
    pos = lax.broadcasted_iota(jnp.int32, (1, s), 1)

    nk = st_ref[0, 2, 0]
    kmax = jnp.max(nk, axis=1, keepdims=True)
    g_ref[...] = jnp.sqrt(st_ref[0, 1, 0]) * (jnp.sqrt(kmax) + jnp.sqrt(nk)) + st_ref[0, 0, 0]

    def prev_block(q0):
        return q0 - tk if isinstance(q0, int) else pl.multiple_of(q0 - tk, tk)

    def load_q(q0):
        qp = q_ref[0, pl.ds(q0, tq), :]
        return [jnp.where(head_lanes[hh], qp, bias_ones[hh]) for hh in range(2)]

    def scores(qh, k0, masked, width=tk):
        kp = k_ref[0, pl.ds(k0, width), :]
        ap = ka_ref[0, pl.ds(k0, width), :]
        out = []
        for hh in range(2):
            kh = jnp.where(head_lanes[hh], kp, ap)
            sc = lax.dot_general(qh[hh], kh, (((1,), (1,)), ((), ())), preferred_element_type=F32)
            out.append(jnp.where(causal, sc, MASK_VALUE) if masked else sc)
        return out

    def softmax_pv(scs, k0s, carry):
        vps = [v_ref[0, pl.ds(k0, scs[0][i].shape[1]), :] for i, k0 in enumerate(k0s)]
        new = []
        for hh in range(2):
            m_new = None if carry is None else carry[hh][0]
            for sc in scs[hh]:
                blk_max = jnp.max(sc, axis=1, keepdims=True)
                m_new = blk_max if m_new is None else jnp.maximum(m_new, blk_max)
            acc = None if carry is None else jnp.exp2(carry[hh][0] - m_new) * carry[hh][1]
            for sc, vp in zip(scs[hh], vps):
                p = jnp.exp2(sc - m_new).astype(BF16)
                vh = jnp.where(head_lanes[hh], vp, jnp.ones_like(vp))
                pv = jnp.dot(p, vh, preferred_element_type=F32)
                acc = pv if acc is None else acc + pv
            new.append((m_new, acc))
        return tuple(new)

    for qi in range(1, s // tq):
        gmax = jnp.max(g_ref[:, qi * tq:(qi + 1) * tq], axis=1, keepdims=True)
        live = (gmax - st_ref[0, 0, 0] >= -EXP2_UNDERFLOW) & (pos < qi * tq)
        n_live = jnp.max(jnp.sum(live.astype(F32), axis=1, keepdims=True), axis=0, keepdims=True)
        nback_ref[qi] = jnp.ceil(n_live * (1.0 / tk)).astype(jnp.int32)[0, 0]

    def tail_scores(slot, q0, qh):
        for bi, (k0, masked) in enumerate([(prev_block(q0), False), (q0, True)]):
            for hh, sc in enumerate(scores(qh, k0, masked)):
                s_ref[slot, hh, bi] = sc

    def finish(q0, state):
        acc0, acc1 = state[0][1], state[1][1]
        out = jnp.where(head_lanes[0], acc0 / pltpu.roll(acc0, HEAD_DIM, 1), acc1 / pltpu.roll(acc1, HEAD_DIM, 1))
        o_ref[0, pl.ds(q0, tq), :] = out.astype(o_ref.dtype)

    def park(state):
        for hh in range(2):
            m_st[hh], acc_st[hh] = state[hh]

    def parked():
        return tuple((m_st[hh], acc_st[hh]) for hh in range(2))

    def tail_finish(slot, q0):
        scs = [[s_ref[slot, hh, 0], s_ref[slot, hh, 1]] for hh in range(2)]
        state = softmax_pv(scs, [prev_block(q0), q0], None)
        finish(q0, state)
        park(state)

    def far_blocks(qi, q0):
        n_back = nback_ref[qi]

        @pl.when(n_back > 1)
        def _():
            qh = load_q(q0)

            def step(kj, _):
                k0 = pl.multiple_of(kj * tk, tk)
                park(softmax_pv([[sc] for sc in scores(qh, k0, False)], [k0], parked()))
                return 0

            lax.fori_loop(qi - n_back, qi - 1, step, 0)
            finish(q0, parked())

    finish(0, softmax_pv([[sc] for sc in scores(load_q(0), 0, True)], [0], None))

    n_q = s // tq
    last = n_q - 1
    tail_scores(0, tq, load_q(tq))

    def pair(i, _):
        qa = 2 * i + 1
        qb = qa + 1
        qn = jnp.minimum(qa + 2, last)
        a0, b0, n0 = (pl.multiple_of(x * tq, tq) for x in (qa, qb, qn))
        tail_scores(1, b0, load_q(b0))
        tail_finish(0, a0)
        far_blocks(qa, a0)
        tail_scores(0, n0, load_q(n0))
        tail_finish(1, b0)
        far_blocks(qb, b0)
        return 0

    lax.fori_loop(0, last // 2, pair, 0)
    if last % 2 == 1:
        q0 = last * tq
        tail_finish(0, q0)
        far_blocks(last, q0)


def _attention(q, k, kaug, v, stats, *, tq=512):
    b, s, d = q.shape
    n_groups = d // LANES
    heads_per_group = LANES // HEAD_DIM
    assert heads_per_group == 2 and stats.shape == (b, 3, n_groups, heads_per_group, s)
    kern = functools.partial(_attn_kernel, s=s, tq=tq)
    col_spec = pl.BlockSpec((1, s, LANES), lambda i, j: (i, 0, j))
    return pl.pallas_call(
        kern,
        grid=(b, n_groups),
        in_specs=[col_spec, col_spec, col_spec, col_spec,
                  pl.BlockSpec((1, 3, 1, heads_per_group, s), lambda i, j: (i, 0, j, 0, 0))],
        out_specs=col_spec,
        out_shape=jax.ShapeDtypeStruct((b, s, d), BF16),
        scratch_shapes=[pltpu.VMEM((heads_per_group, s), F32),
                        pltpu.VMEM((2, heads_per_group, 2, tq, tq), F32),
                        pltpu.VMEM((heads_per_group, tq, 1), F32),
                        pltpu.VMEM((heads_per_group, tq, LANES), F32),
                        pltpu.SMEM((s // tq,), jnp.int32)],
        compiler_params=pltpu.CompilerParams(
            dimension_semantics=("arbitrary", "arbitrary"), vmem_limit_bytes=VMEM_LIMIT_BYTES),
        name="fox_attention",
    )(q, k, kaug, v, stats)


def kernel(x, g_mix_pre, g_mix_post, g_ffn_pre, g_ffn_post, conv_pw1_w, conv_pw1_b, conv_dw_w, conv_dw_b,
           conv_ln_g, conv_ln_b, conv_pw2_w, conv_pw2_b, attn_w_in, attn_b_f, attn_w_o, mlp_w_up, mlp_w_down):
    b, s, d = x.shape
    depth = g_mix_pre.shape[0]
    for i in range(depth):
        j = i // 2
        attn = None
        if i % 2 == 0:
            x = _conv_mixer(x, g_mix_pre[i], conv_pw1_w[j], conv_pw1_b[j], conv_dw_w[j], conv_dw_b[j],
                            conv_ln_g[j], conv_ln_b[j], conv_pw2_w[j], conv_pw2_b[j], g_mix_post[i])
        else:
            q, k, kaug, v, stats = _qkv(x, g_mix_pre[i], attn_w_in, j, attn_b_f[j])
            attn = (_attention(q, k, kaug, v, stats).reshape(b * s, d), attn_w_o[j], g_mix_post[i])
        x = _mlp(x.reshape(b * s, d), g_ffn_pre[i], mlp_w_up, mlp_w_down, i, g_ffn_post[i], attn).reshape(b, s, d)
    return x
```

```python
import functools

import jax
import jax.numpy as jnp
from jax import lax
from jax.experimental import pallas as pl
from jax.experimental.pallas import tpu as pltpu

N_HEADS = 16
HEAD_DIM = 64
CONV_WIDTH = 31
RMS_EPS = 1e-6
LN_EPS = 1e-5
MASK_VALUE = -1e30

LANES = 128
SUBLANES = 8
CONV_ROWS = 128
CONV_COLS = 256
FF_CHUNK = 1024
CONV_HALO = 32
VMEM_LIMIT_BYTES = 56 * 1024 * 1024
EXP2_UNDERFLOW = 160.0
LOG2E = 1.4426950408889634
NORM_SLACK = 1.01

F32 = jnp.float32
BF16 = jnp.bfloat16


def _rms(x, g):
    return x * lax.rsqrt(jnp.mean(x * x, axis=-1, keepdims=True) + RMS_EPS) * g


def _const_spec(shape):
    nd = len(shape)
    return pl.BlockSpec(shape, lambda *_: (0,) * nd, pipeline_mode=pl.Buffered(1))


def _conv_mixer_kernel(x_ref, gpre_ref, w1_ref, b1_ref, dww_ref, dwb_ref, lng_ref, lnb_ref,
                       w2_ref, b2_ref, gpost_ref, o_ref, ubuf_ref, ybuf_ref, *, tm, d):
    x = x_ref[0]
    hb = _rms(x, gpre_ref[...]).astype(BF16)

    @pl.when(pl.program_id(1) == 0)
    def _():
        ubuf_ref[0:CONV_HALO, :] = jnp.zeros((CONV_HALO, d), F32)
        ubuf_ref[CONV_HALO + tm:CONV_HALO + tm + SUBLANES, :] = jnp.zeros((SUBLANES, d), F32)

    def conv_chunk(r0, c0):
        acc = jnp.broadcast_to(dwb_ref[:, c0:c0 + LANES], (CONV_ROWS, LANES))
        for r in range(SUBLANES):
            z = None
            for o in range(r if r >= 2 else r + SUBLANES, CONV_WIDTH + 2, SUBLANES):
                a8 = r0 + o - r
                term = ubuf_ref[a8:a8 + CONV_ROWS + SUBLANES, c0:c0 + LANES] * dww_ref[o - 2:o - 1, c0:c0 + LANES]
                z = term if z is None else z + term
            acc = acc + z[r:r + CONV_ROWS]
        ybuf_ref[r0:r0 + CONV_ROWS, c0:c0 + LANES] = acc

    for g0 in range(0, d, CONV_COLS):
        cols = slice(g0, g0 + CONV_COLS)
        gate_cols = slice(d + g0, d + g0 + CONV_COLS)
        a_u = jnp.dot(hb, w1_ref[:, cols], preferred_element_type=F32) + b1_ref[:, cols]
        a_g = jnp.dot(hb, w1_ref[:, gate_cols], preferred_element_type=F32) + b1_ref[:, gate_cols]
        ubuf_ref[CONV_HALO:CONV_HALO + tm, cols] = a_u * jax.nn.sigmoid(a_g)
        for c0 in range(g0, g0 + CONV_COLS, LANES):
            for r0 in range(0, tm, CONV_ROWS):
                conv_chunk(r0, c0)
        ubuf_ref[0:CONV_HALO, cols] = ubuf_ref[tm:tm + CONV_HALO, cols]

    y = ybuf_ref[...]
    mu = jnp.mean(y, axis=-1, keepdims=True)
    yc = y - mu
    var = jnp.mean(yc * yc, axis=-1, keepdims=True)
    yn = yc * lax.rsqrt(var + LN_EPS) * lng_ref[...] + lnb_ref[...]
    act = yn * jax.nn.sigmoid(yn)
    m = jnp.dot(act.astype(BF16), w2_ref[...], preferred_element_type=F32) + b2_ref[...]
    o_ref[0] = x + _rms(m, gpost_ref[...])


def _conv_mixer(x, gpre, w1, b1, dww, dwb, lng, lnb, w2, b2, gpost, *, tm=512):
    b, s, d = x.shape
    row = lambda v: v.reshape(1, -1).astype(F32)
    kern = functools.partial(_conv_mixer_kernel, tm=tm, d=d)
    return pl.pallas_call(
        kern,
        grid=(b, s // tm),
        in_specs=[
            pl.BlockSpec((1, tm, d), lambda i, j: (i, j, 0)),
            _const_spec((1, d)), _const_spec((d, 2 * d)), _const_spec((1, 2 * d)),
            _const_spec((CONV_WIDTH, d)), _const_spec((1, d)), _const_spec((1, d)), _const_spec((1, d)),
            _const_spec((d, d)), _const_spec((1, d)), _const_spec((1, d)),
        ],
        out_specs=pl.BlockSpec((1, tm, d), lambda i, j: (i, j, 0)),
        out_shape=jax.ShapeDtypeStruct(x.shape, F32),
        scratch_shapes=[pltpu.VMEM((CONV_HALO + tm + SUBLANES, d), F32), pltpu.VMEM((tm, d), F32)],
        compiler_params=pltpu.CompilerParams(
            dimension_semantics=("arbitrary", "arbitrary"), vmem_limit_bytes=VMEM_LIMIT_BYTES),
        name="conv_mixer",
    )(x, row(gpre), w1.astype(BF16), row(b1), dww.astype(F32), row(dwb), row(lng), row(lnb),
      w2.astype(BF16), row(b2), row(gpost))


def _mlp_body(x, gpre_ref, wup_ref, wdown_ref, gpost_ref, o_ref):
    hb = _rms(x, gpre_ref[...]).astype(BF16)
    down = None
    for c0 in range(0, wup_ref.shape[1], FF_CHUNK):
        up = jnp.dot(hb, wup_ref[:, c0:c0 + FF_CHUNK].astype(BF16), preferred_element_type=F32)
        r = jnp.maximum(up, 0.0)
        part = jnp.dot((r * r).astype(BF16), wdown_ref[c0:c0 + FF_CHUNK, :].astype(BF16), preferred_element_type=F32)
        down = part if down is None else down + part
    o_ref[...] = x + _rms(down, gpost_ref[...])


def _mlp_kernel(x_ref, gpre_ref, wup_ref, wdown_ref, gpost_ref, o_ref):
    _mlp_body(x_ref[...], gpre_ref, wup_ref, wdown_ref, gpost_ref, o_ref)


def _attn_out_mlp_kernel(x_ref, a_ref, wo_ref, gmix_ref, gpre_ref, wup_ref, wdown_ref, gpost_ref, o_ref):
    m = jnp.dot(a_ref[...], wo_ref[...], preferred_element_type=F32)
    _mlp_body(x_ref[...] + _rms(m, gmix_ref[...]), gpre_ref, wup_ref, wdown_ref, gpost_ref, o_ref)


def _mlp(x2d, gpre, wup_all, wdown_all, layer, gpost, attn=None, *, tm=512):
    t, d = x2d.shape
    row = lambda v: v.reshape(1, -1).astype(F32)
    act_spec = pl.BlockSpec((tm, d), lambda i: (i, 0))
    mlp_specs = [_const_spec((1, d)), _layer_spec(wup_all, layer), _layer_spec(wdown_all, layer), _const_spec((1, d))]
    mlp_args = (row(gpre), wup_all.astype(F32), wdown_all.astype(F32), row(gpost))
    if attn is None:
        kern, in_specs, args = _mlp_kernel, [act_spec] + mlp_specs, (x2d,) + mlp_args
    else:
        a2d, wo, gmix = attn
        kern = _attn_out_mlp_kernel
        in_specs = [act_spec, act_spec, _const_spec((d, d)), _const_spec((1, d))] + mlp_specs
        args = (x2d, a2d, wo.astype(BF16), row(gmix)) + mlp_args
    return pl.pallas_call(
        kern,
        grid=(t // tm,),
        in_specs=in_specs,
        out_specs=act_spec,
        out_shape=jax.ShapeDtypeStruct(x2d.shape, F32),
        compiler_params=pltpu.CompilerParams(
            dimension_semantics=("arbitrary",), vmem_limit_bytes=VMEM_LIMIT_BYTES),
        name="sqrelu_mlp" if attn is None else "attn_out_mlp",
    )(*args)


def _split3_bf16(x):
    hi = x.astype(BF16)
    r1 = x - hi.astype(F32)
    mid = r1.astype(BF16)
    lo = (r1 - mid.astype(F32)).astype(BF16)
    return hi, mid, lo


def _head_sq_norm_bound(xb, ind):
    xf = xb.astype(F32)
    nsq = jnp.dot((xf * xf).astype(BF16), ind, preferred_element_type=F32)
    return nsq * NORM_SLACK


def _qkv_kernel(x_ref, gpre_ref, win_ref, wf_ref, bf_ref, ind_ref, sel_ref, q_ref, k_ref, ka_ref, v_ref, st_ref,
                carry_ref, *, tm, d):
    x = x_ref[0]
    hb = _rms(x, gpre_ref[...]).astype(BF16)
    proj = jnp.dot(hb, win_ref[:, :3 * d].astype(BF16), preferred_element_type=F32)
    qb = (proj[:, :d] * (HEAD_DIM ** -0.5 * LOG2E)).astype(BF16)
    kb = proj[:, d:2 * d].astype(BF16)
    q_ref[0] = qb
    k_ref[0] = kb
    v_ref[0] = proj[:, 2 * d:].astype(BF16)

    def put_stat(t, per_head):
        rows = per_head.T
        for g in range(N_HEADS // 2):
            st_ref[0, t, g] = rows[2 * g:2 * g + 2, :]

    put_stat(1, _head_sq_norm_bound(qb, ind_ref[...]))
    put_stat(2, _head_sq_norm_bound(kb, ind_ref[...]))

    f_logit = jnp.dot(hb, wf_ref[...], preferred_element_type=F32) + bf_ref[...]
    log_f = jax.nn.log_sigmoid(f_logit)

    @pl.when(pl.program_id(1) == 0)
    def _():
        carry_ref[...] = jnp.zeros_like(carry_ref)

    rows = lax.broadcasted_iota(jnp.int32, (tm, tm), 0)
    cols = lax.broadcasted_iota(jnp.int32, (tm, tm), 1)
    tri = (rows >= cols).astype(BF16)
    lane = lax.broadcasted_iota(jnp.int32, (1, LANES), 1)

    def pack3(v):
        v_hi, v_mid, v_lo = (t.astype(F32) for t in _split3_bf16(v))
        return jnp.where(lane < N_HEADS, v_hi, jnp.where(
            lane < 2 * N_HEADS, pltpu.roll(v_mid, N_HEADS, 1), pltpu.roll(v_lo, 2 * N_HEADS, 1))).astype(BF16)

    c3 = jnp.dot(tri, pack3(log_f), preferred_element_type=F32)
    csum = c3 + pltpu.roll(c3, LANES - N_HEADS, 1) + pltpu.roll(c3, LANES - 2 * N_HEADS, 1)
    f_cum = csum + carry_ref[...]
    carry_ref[...] = f_cum[tm - 1:tm, :]
    f2 = f_cum * LOG2E
    put_stat(0, f2)
    ka_ref[0] = jnp.dot(pack3(f2), sel_ref[...], preferred_element_type=F32).astype(BF16)


def _layer_spec(stacked, layer):
    return pl.BlockSpec((None,) + stacked.shape[1:], lambda *_: (layer, 0, 0), pipeline_mode=pl.Buffered(1))


def _qkv(x, gpre, w_in_all, layer, b_f, *, tm=512):
    b, s, d = x.shape
    w_in = w_in_all[layer]
    wf = jnp.zeros((d, LANES), F32).at[:, :N_HEADS].set(w_in[:, 3 * d:]).astype(BF16)
    bf = jnp.zeros((1, LANES), F32).at[0, :N_HEADS].set(b_f.astype(F32))
    ind = (jnp.arange(d)[:, None] // HEAD_DIM == jnp.arange(LANES)[None, :]).astype(BF16)
    rows = jnp.arange(LANES)[:, None]
    head, term = rows % N_HEADS, rows // N_HEADS
    dest = (head // 2) * LANES + (1 - head % 2) * HEAD_DIM
    sel = -((jnp.arange(d)[None, :] == dest + term) & (term < 3)).astype(BF16)
    kern = functools.partial(_qkv_kernel, tm=tm, d=d)
    act_spec = pl.BlockSpec((1, tm, d), lambda i, j: (i, j, 0))
    return pl.pallas_call(
        kern,
        grid=(b, s // tm),
        in_specs=[act_spec, _const_spec((1, d)), _layer_spec(w_in_all, layer), _const_spec((d, LANES)),
                  _const_spec((1, LANES)), _const_spec((d, LANES)), _const_spec((LANES, d))],
        out_specs=[act_spec, act_spec, act_spec, act_spec,
                   pl.BlockSpec((1, 3, N_HEADS // 2, 2, tm), lambda i, j: (i, 0, 0, 0, j))],
        out_shape=[jax.ShapeDtypeStruct((b, s, d), BF16)] * 4
        + [jax.ShapeDtypeStruct((b, 3, N_HEADS // 2, 2, s), F32)],
        scratch_shapes=[pltpu.VMEM((1, LANES), F32)],
        compiler_params=pltpu.CompilerParams(
            dimension_semantics=("arbitrary", "arbitrary"), vmem_limit_bytes=VMEM_LIMIT_BYTES),
        name="attn_qkv",
    )(x, gpre.reshape(1, -1).astype(F32), w_in_all.astype(F32), wf, bf, ind, sel)


def _attn_kernel(q_ref, k_ref, ka_ref, v_ref, st_ref, o_ref, g_ref, s_ref, m_st, acc_st, nback_ref, *, s, tq):
    tk = tq
    lane = lax.broadcasted_iota(jnp.int32, (1, LANES), 1)
    head_lanes = [lane < HEAD_DIM, lane >= HEAD_DIM]
    bias_ones = [((lane >= HEAD_DIM) & (lane < HEAD_DIM + 3)).astype(BF16), (lane < 3).astype(BF16)]
    row_ids = lax.broadcasted_iota(jnp.int32, (tq, tk), 0)
    col_ids = lax.broadcasted_iota(jnp.int32, (tq, tk), 1)
    causal = col_ids <= row_ids
    pos = lax.broadcasted_iota(jnp.int32, (1, s), 1)

    nk = st_ref[0, 2, 0]
    kmax = jnp.max(nk, axis=1, keepdims=True)
    g_ref[...] = jnp.sqrt(st_ref[0, 1, 0]) * (jnp.sqrt(kmax) + jnp.sqrt(nk)) + st_ref[0, 0, 0]

    def prev_block(q0):
        return q0 - tk if isinstance(q0, int) else pl.multiple_of(q0 - tk, tk)

    def load_q(q0):
        qp = q_ref[0, pl.ds(q0, tq), :]
        return [jnp.where(head_lanes[hh], qp, bias_ones[hh]) for hh in range(2)]

    def scores(qh, k0, masked, width=tk):
        kp = k_ref[0, pl.ds(k0, width), :]
        ap = ka_ref[0, pl.ds(k0, width), :]
        out = []
        for hh in range(2):
            kh = jnp.where(head_lanes[hh], kp, ap)
            sc = lax.dot_general(qh[hh], kh, (((1,), (1,)), ((), ())), preferred_element_type=F32)
            out.append(jnp.where(causal, sc, MASK_VALUE) if masked else sc)
        return out

    def softmax_pv(scs, k0s, carry):
        vps = [v_ref[0, pl.ds(k0, scs[0][i].shape[1]), :] for i, k0 in enumerate(k0s)]
        new = []
        for hh in range(2):
            m_new = None if carry is None else carry[hh][0]
            for sc in scs[hh]:
                blk_max = jnp.max(sc, axis=1, keepdims=True)
                m_new = blk_max if m_new is None else jnp.maximum(m_new, blk_max)
            acc = None if carry is None else jnp.exp2(carry[hh][0] - m_new) * carry[hh][1]
            for sc, vp in zip(scs[hh], vps):
                p = jnp.exp2(sc - m_new).astype(BF16)
                vh = jnp.where(head_lanes[hh], vp, jnp.ones_like(vp))
                pv = jnp.dot(p, vh, preferred_element_type=F32)
                acc = pv if acc is None else acc + pv
            new.append((m_new, acc))
        return tuple(new)

    for qi in range(1, s // tq):
        gmax = jnp.max(g_ref[:, qi * tq:(qi + 1) * tq], axis=1, keepdims=True)
        live = (gmax - st_ref[0, 0, 0] >= -EXP2_UNDERFLOW) & (pos < qi * tq)
        n_live = jnp.max(jnp.sum(live.astype(F32), axis=1, keepdims=True), axis=0, keepdims=True)
        nback_ref[qi] = jnp.ceil(n_live * (1.0 / tk)).astype(jnp.int32)[0, 0]

    def tail_scores(slot, q0, qh):
        for bi, (k0, masked) in enumerate([(prev_block(q0), False), (q0, True)]):
            for hh, sc in enumerate(scores(qh, k0, masked)):
                s_ref[slot, hh, bi] = sc

    def finish(q0, state):
        acc0, acc1 = state[0][1], state[1][1]
        out = jnp.where(head_lanes[0], acc0 / pltpu.roll(acc0, HEAD_DIM, 1), acc1 / pltpu.roll(acc1, HEAD_DIM, 1))
        o_ref[0, pl.ds(q0, tq), :] = out.astype(o_ref.dtype)

    def park(state):
        for hh in range(2):
            m_st[hh] = jnp.broadcast_to(state[hh][0], (tq, LANES))
            acc_st[hh] = state[hh][1]

    def parked():
        return tuple((jnp.max(m_st[hh], axis=1, keepdims=True), acc_st[hh]) for hh in range(2))

    def tail_finish(slot, q0):
        scs = [[s_ref[slot, hh, 0], s_ref[slot, hh, 1]] for hh in range(2)]
        state = softmax_pv(scs, [prev_block(q0), q0], None)
        finish(q0, state)
        park(state)

    def far_blocks(qi, q0):
        n_back = nback_ref[qi]

        @pl.when(n_back > 1)
        def _():
            qh = load_q(q0)

            def step(kj, state):
                k0 = pl.multiple_of(kj * tk, tk)
                return softmax_pv([[sc] for sc in scores(qh, k0, False)], [k0], state)

            finish(q0, lax.fori_loop(qi - n_back, qi - 1, step, parked()))

    finish(0, softmax_pv([[sc] for sc in scores(load_q(0), 0, True)], [0], None))

    n_q = s // tq
    last = n_q - 1
    tail_scores(0, tq, load_q(tq))

    def pair(i, _):
        qa = 2 * i + 1
        qb = qa + 1
        qn = jnp.minimum(qa + 2, last)
        a0, b0, n0 = (pl.multiple_of(x * tq, tq) for x in (qa, qb, qn))
        tail_scores(1, b0, load_q(b0))
        tail_finish(0, a0)
        far_blocks(qa, a0)
        tail_scores(0, n0, load_q(n0))
        tail_finish(1, b0)
        far_blocks(qb, b0)
        return 0

    lax.fori_loop(0, last // 2, pair, 0)
    if last % 2 == 1:
        q0 = last * tq
        tail_finish(0, q0)
        far_blocks(last, q0)


def _attention(q, k, kaug, v, stats, *, tq=512):
    b, s, d = q.shape
    n_groups = d // LANES
    heads_per_group = LANES // HEAD_DIM
    assert heads_per_group == 2 and stats.shape == (b, 3, n_groups, heads_per_group, s)
    kern = functools.partial(_attn_kernel, s=s, tq=tq)
    col_spec = pl.BlockSpec((1, s, LANES), lambda i, j: (i, 0, j))
    return pl.pallas_call(
        kern,
        grid=(b, n_groups),
        in_specs=[col_spec, col_spec, col_spec, col_spec,
                  pl.BlockSpec((1, 3, 1, heads_per_group, s), lambda i, j: (i, 0, j, 0, 0))],
        out_specs=col_spec,
        out_shape=jax.ShapeDtypeStruct((b, s, d), BF16),
        scratch_shapes=[pltpu.VMEM((heads_per_group, s), F32),
                        pltpu.VMEM((2, heads_per_group, 2, tq, tq), F32),
                        pltpu.VMEM((heads_per_group, tq, LANES), F32),
                        pltpu.VMEM((heads_per_group, tq, LANES), F32),
                        pltpu.SMEM((s // tq,), jnp.int32)],
        compiler_params=pltpu.CompilerParams(
            dimension_semantics=("arbitrary", "arbitrary"), vmem_limit_bytes=VMEM_LIMIT_BYTES),
        name="fox_attention",
    )(q, k, kaug, v, stats)


def kernel(x, g_mix_pre, g_mix_post, g_ffn_pre, g_ffn_post, conv_pw1_w, conv_pw1_b, conv_dw_w, conv_dw_b,
           conv_ln_g, conv_ln_b, conv_pw2_w, conv_pw2_b, attn_w_in, attn_b_f, attn_w_o, mlp_w_up, mlp_w_down):
    b, s, d = x.shape
    depth = g_mix_pre.shape[0]
    for i in range(depth):
        j = i // 2
        attn = None
        if i % 2 == 0:
            x = _conv_mixer(x, g_mix_pre[i], conv_pw1_w[j], conv_pw1_b[j], conv_dw_w[j], conv_dw_b[j],
                            conv_ln_g[j], conv_ln_b[j], conv_pw2_w[j], conv_pw2_b[j], g_mix_post[i])
        else:
            q, k, kaug, v, stats = _qkv(x, g_mix_pre[i], attn_w_in, j, attn_b_f[j])
            attn = (_attention(q, k, kaug, v, stats).reshape(b * s, d), attn_w_o[j], g_mix_post[i])
        x = _mlp(x.reshape(b * s, d), g_ffn_pre[i], mlp_w_up, mlp_w_down, i, g_ffn_post[i], attn).reshape(b, s, d)
    return x
```

```python
import functools

import jax
import jax.numpy as jnp
from jax import lax
from jax.experimental import pallas as pl
from jax.experimental.pallas import tpu as pltpu

N_HEADS = 16
HEAD_DIM = 64
CONV_WIDTH = 31
RMS_EPS = 1e-6
LN_EPS = 1e-5
MASK_VALUE = -1e30

LANES = 128
SUBLANES = 8
CONV_ROWS = 128
CONV_COLS = 256
FF_CHUNK = 1024
CONV_HALO = 32
VMEM_LIMIT_BYTES = 56 * 1024 * 1024
EXP2_UNDERFLOW = 152.0
LOG2E = 1.4426950408889634
NORM_SLACK = 1.01

F32 = jnp.float32
BF16 = jnp.bfloat16


def _rms(x, g):
    return x * lax.rsqrt(jnp.mean(x * x, axis=-1, keepdims=True) + RMS_EPS) * g


def _const_spec(shape):
    nd = len(shape)
    return pl.BlockSpec(shape, lambda *_: (0,) * nd, pipeline_mode=pl.Buffered(1))


def _conv_mixer_kernel(x_ref, gpre_ref, w1_ref, b1_ref, dww_ref, dwb_ref, lng_ref, lnb_ref,
                       w2_ref, b2_ref, gpost_ref, o_ref, ubuf_ref, ybuf_ref, *, tm, d):
    x = x_ref[0]
    hb = _rms(x, gpre_ref[...]).astype(BF16)

    @pl.when(pl.program_id(1) == 0)
    def _():
        ubuf_ref[0:CONV_HALO, :] = jnp.zeros((CONV_HALO, d), F32)
        ubuf_ref[CONV_HALO + tm:CONV_HALO + tm + SUBLANES, :] = jnp.zeros((SUBLANES, d), F32)

    def conv_chunk(r0, c0):
        acc = jnp.broadcast_to(dwb_ref[:, c0:c0 + LANES], (CONV_ROWS, LANES))
        for r in range(SUBLANES):
            z = None
            for o in range(r if r >= 2 else r + SUBLANES, CONV_WIDTH + 2, SUBLANES):
                a8 = r0 + o - r
                term = ubuf_ref[a8:a8 + CONV_ROWS + SUBLANES, c0:c0 + LANES] * dww_ref[o - 2:o - 1, c0:c0 + LANES]
                z = term if z is None else z + term
            acc = acc + z[r:r + CONV_ROWS]
        ybuf_ref[r0:r0 + CONV_ROWS, c0:c0 + LANES] = acc

    for g0 in range(0, d, CONV_COLS):
        cols = slice(g0, g0 + CONV_COLS)
        gate_cols = slice(d + g0, d + g0 + CONV_COLS)
        a_u = jnp.dot(hb, w1_ref[:, cols], preferred_element_type=F32) + b1_ref[:, cols]
        a_g = jnp.dot(hb, w1_ref[:, gate_cols], preferred_element_type=F32) + b1_ref[:, gate_cols]
        ubuf_ref[CONV_HALO:CONV_HALO + tm, cols] = a_u * jax.nn.sigmoid(a_g)
        for c0 in range(g0, g0 + CONV_COLS, LANES):
            for r0 in range(0, tm, CONV_ROWS):
                conv_chunk(r0, c0)
        ubuf_ref[0:CONV_HALO, cols] = ubuf_ref[tm:tm + CONV_HALO, cols]

    y = ybuf_ref[...]
    mu = jnp.mean(y, axis=-1, keepdims=True)
    yc = y - mu
    var = jnp.mean(yc * yc, axis=-1, keepdims=True)
    yn = yc * lax.rsqrt(var + LN_EPS) * lng_ref[...] + lnb_ref[...]
    act = yn * jax.nn.sigmoid(yn)
    m = jnp.dot(act.astype(BF16), w2_ref[...], preferred_element_type=F32) + b2_ref[...]
    o_ref[0] = x + _rms(m, gpost_ref[...])


def _conv_mixer(x, gpre, w1, b1, dww, dwb, lng, lnb, w2, b2, gpost, *, tm=512):
    b, s, d = x.shape
    row = lambda v: v.reshape(1, -1).astype(F32)
    kern = functools.partial(_conv_mixer_kernel, tm=tm, d=d)
    return pl.pallas_call(
        kern,
        grid=(b, s // tm),
        in_specs=[
            pl.BlockSpec((1, tm, d), lambda i, j: (i, j, 0)),
            _const_spec((1, d)), _const_spec((d, 2 * d)), _const_spec((1, 2 * d)),
            _const_spec((CONV_WIDTH, d)), _const_spec((1, d)), _const_spec((1, d)), _const_spec((1, d)),
            _const_spec((d, d)), _const_spec((1, d)), _const_spec((1, d)),
        ],
        out_specs=pl.BlockSpec((1, tm, d), lambda i, j: (i, j, 0)),
        out_shape=jax.ShapeDtypeStruct(x.shape, F32),
        scratch_shapes=[pltpu.VMEM((CONV_HALO + tm + SUBLANES, d), F32), pltpu.VMEM((tm, d), F32)],
        compiler_params=pltpu.CompilerParams(
            dimension_semantics=("arbitrary", "arbitrary"), vmem_limit_bytes=VMEM_LIMIT_BYTES),
        name="conv_mixer",
    )(x, row(gpre), w1.astype(BF16), row(b1), dww.astype(F32), row(dwb), row(lng), row(lnb),
      w2.astype(BF16), row(b2), row(gpost))


def _mlp_body(x, gpre_ref, wup_ref, wdown_ref, gpost_ref, o_ref):
    hb = _rms(x, gpre_ref[...]).astype(BF16)
    down = None
    for c0 in range(0, wup_ref.shape[1], FF_CHUNK):
        up = jnp.dot(hb, wup_ref[:, c0:c0 + FF_CHUNK].astype(BF16), preferred_element_type=F32)
        r = jnp.maximum(up, 0.0)
        part = jnp.dot((r * r).astype(BF16), wdown_ref[c0:c0 + FF_CHUNK, :].astype(BF16), preferred_element_type=F32)
        down = part if down is None else down + part
    o_ref[...] = x + _rms(down, gpost_ref[...])


def _mlp_kernel(x_ref, gpre_ref, wup_ref, wdown_ref, gpost_ref, o_ref):
    _mlp_body(x_ref[...], gpre_ref, wup_ref, wdown_ref, gpost_ref, o_ref)


def _attn_out_mlp_kernel(x_ref, a_ref, wo_ref, gmix_ref, gpre_ref, wup_ref, wdown_ref, gpost_ref, o_ref):
    m = jnp.dot(a_ref[...], wo_ref[...], preferred_element_type=F32)
    _mlp_body(x_ref[...] + _rms(m, gmix_ref[...]), gpre_ref, wup_ref, wdown_ref, gpost_ref, o_ref)


def _mlp(x2d, gpre, wup_all, wdown_all, layer, gpost, attn=None, *, tm=512):
    t, d = x2d.shape
    row = lambda v: v.reshape(1, -1).astype(F32)
    act_spec = pl.BlockSpec((tm, d), lambda i: (i, 0))
    mlp_specs = [_const_spec((1, d)), _layer_spec(wup_all, layer), _layer_spec(wdown_all, layer), _const_spec((1, d))]
    mlp_args = (row(gpre), wup_all.astype(F32), wdown_all.astype(F32), row(gpost))
    if attn is None:
        kern, in_specs, args = _mlp_kernel, [act_spec] + mlp_specs, (x2d,) + mlp_args
    else:
        a2d, wo, gmix = attn
        kern = _attn_out_mlp_kernel
        in_specs = [act_spec, act_spec, _const_spec((d, d)), _const_spec((1, d))] + mlp_specs
        args = (x2d, a2d, wo.astype(BF16), row(gmix)) + mlp_args
    return pl.pallas_call(
        kern,
        grid=(t // tm,),
        in_specs=in_specs,
        out_specs=act_spec,
        out_shape=jax.ShapeDtypeStruct(x2d.shape, F32),
        compiler_params=pltpu.CompilerParams(
            dimension_semantics=("arbitrary",), vmem_limit_bytes=VMEM_LIMIT_BYTES),
        name="sqrelu_mlp" if attn is None else "attn_out_mlp",
    )(*args)


def _split3_bf16(x):
    hi = x.astype(BF16)
    r1 = x - hi.astype(F32)
    mid = r1.astype(BF16)
    lo = (r1 - mid.astype(F32)).astype(BF16)
    return hi, mid, lo


def _head_sq_norm_bound(xb, ind):
    xf = xb.astype(F32)
    nsq = jnp.dot((xf * xf).astype(BF16), ind, preferred_element_type=F32)
    return nsq * NORM_SLACK


def _qkv_kernel(x_ref, gpre_ref, win_ref, wf_ref, bf_ref, ind_ref, sel_ref, q_ref, k_ref, ka_ref, v_ref, st_ref,
                carry_ref, *, tm, d):
    x = x_ref[0]
    hb = _rms(x, gpre_ref[...]).astype(BF16)
    proj = jnp.dot(hb, win_ref[:, :3 * d].astype(BF16), preferred_element_type=F32)
    qb = (proj[:, :d] * (HEAD_DIM ** -0.5 * LOG2E)).astype(BF16)
    kb = proj[:, d:2 * d].astype(BF16)
    q_ref[0] = qb
    k_ref[0] = kb
    v_ref[0] = proj[:, 2 * d:].astype(BF16)

    def put_stat(t, per_head):
        rows = per_head.T
        for g in range(N_HEADS // 2):
            st_ref[0, t, g] = rows[2 * g:2 * g + 2, :]

    put_stat(1, _head_sq_norm_bound(qb, ind_ref[...]))
    put_stat(2, _head_sq_norm_bound(kb, ind_ref[...]))

    f_logit = jnp.dot(hb, wf_ref[...], preferred_element_type=F32) + bf_ref[...]
    log_f = jax.nn.log_sigmoid(f_logit)

    @pl.when(pl.program_id(1) == 0)
    def _():
        carry_ref[...] = jnp.zeros_like(carry_ref)

    rows = lax.broadcasted_iota(jnp.int32, (tm, tm), 0)
    cols = lax.broadcasted_iota(jnp.int32, (tm, tm), 1)
    tri = (rows >= cols).astype(BF16)
    lane = lax.broadcasted_iota(jnp.int32, (1, LANES), 1)

    def pack3(v):
        v_hi, v_mid, v_lo = (t.astype(F32) for t in _split3_bf16(v))
        return jnp.where(lane < N_HEADS, v_hi, jnp.where(
            lane < 2 * N_HEADS, pltpu.roll(v_mid, N_HEADS, 1), pltpu.roll(v_lo, 2 * N_HEADS, 1))).astype(BF16)

    c3 = jnp.dot(tri, pack3(log_f), preferred_element_type=F32)
    csum = c3 + pltpu.roll(c3, LANES - N_HEADS, 1) + pltpu.roll(c3, LANES - 2 * N_HEADS, 1)
    f_cum = csum + carry_ref[...]
    carry_ref[...] = f_cum[tm - 1:tm, :]
    f2 = f_cum * LOG2E
    put_stat(0, f2)
    ka_ref[0] = jnp.dot(pack3(f2), sel_ref[...], preferred_element_type=F32).astype(BF16)


def _layer_spec(stacked, layer):
    return pl.BlockSpec((None,) + stacked.shape[1:], lambda *_: (layer, 0, 0), pipeline_mode=pl.Buffered(1))


def _qkv(x, gpre, w_in_all, layer, b_f, *, tm=512):
    b, s, d = x.shape
    w_in = w_in_all[layer]
    wf = jnp.zeros((d, LANES), F32).at[:, :N_HEADS].set(w_in[:, 3 * d:]).astype(BF16)
    bf = jnp.zeros((1, LANES), F32).at[0, :N_HEADS].set(b_f.astype(F32))
    ind = (jnp.arange(d)[:, None] // HEAD_DIM == jnp.arange(LANES)[None, :]).astype(BF16)
    rows = jnp.arange(LANES)[:, None]
    head, term = rows % N_HEADS, rows // N_HEADS
    dest = (head // 2) * LANES + (1 - head % 2) * HEAD_DIM
    sel = -((jnp.arange(d)[None, :] == dest + term) & (term < 3)).astype(BF16)
    kern = functools.partial(_qkv_kernel, tm=tm, d=d)
    act_spec = pl.BlockSpec((1, tm, d), lambda i, j: (i, j, 0))
    return pl.pallas_call(
        kern,
        grid=(b, s // tm),
        in_specs=[act_spec, _const_spec((1, d)), _layer_spec(w_in_all, layer), _const_spec((d, LANES)),
                  _const_spec((1, LANES)), _const_spec((d, LANES)), _const_spec((LANES, d))],
        out_specs=[act_spec, act_spec, act_spec, act_spec,
                   pl.BlockSpec((1, 3, N_HEADS // 2, 2, tm), lambda i, j: (i, 0, 0, 0, j))],
        out_shape=[jax.ShapeDtypeStruct((b, s, d), BF16)] * 4
        + [jax.ShapeDtypeStruct((b, 3, N_HEADS // 2, 2, s), F32)],
        scratch_shapes=[pltpu.VMEM((1, LANES), F32)],
        compiler_params=pltpu.CompilerParams(
            dimension_semantics=("arbitrary", "arbitrary"), vmem_limit_bytes=VMEM_LIMIT_BYTES),
        name="attn_qkv",
    )(x, gpre.reshape(1, -1).astype(F32), w_in_all.astype(F32), wf, bf, ind, sel)


def _attn_kernel(q_ref, k_ref, ka_ref, v_ref, st_ref, o_ref, g_ref, s_ref, nback_ref, *, s, tq):
    tk = tq
    lane = lax.broadcasted_iota(jnp.int32, (1, LANES), 1)
    head_lanes = [lane < HEAD_DIM, lane >= HEAD_DIM]
    bias_ones = [((lane >= HEAD_DIM) & (lane < HEAD_DIM + 3)).astype(BF16), (lane < 3).astype(BF16)]
    row_ids = lax.broadcasted_iota(jnp.int32, (tq, tk), 0)
    col_ids = lax.broadcasted_iota(jnp.int32, (tq, tk), 1)
    causal = col_ids <= row_ids
    pos = lax.broadcasted_iota(jnp.int32, (1, s), 1)

    nk = st_ref[0, 2, 0]
    kmax = jnp.max(nk, axis=1, keepdims=True)
    g_ref[...] = jnp.sqrt(st_ref[0, 1, 0]) * (jnp.sqrt(kmax) + jnp.sqrt(nk)) + st_ref[0, 0, 0]

    def prev_block(q0):
        return q0 - tk if isinstance(q0, int) else pl.multiple_of(q0 - tk, tk)

    def load_q(q0):
        qp = q_ref[0, pl.ds(q0, tq), :]
        return [jnp.where(head_lanes[hh], qp, bias_ones[hh]) for hh in range(2)]

    def scores(qh, k0, masked):
        kp = k_ref[0, pl.ds(k0, tk), :]
        ap = ka_ref[0, pl.ds(k0, tk), :]
        out = []
        for hh in range(2):
            kh = jnp.where(head_lanes[hh], kp, ap)
            sc = lax.dot_general(qh[hh], kh, (((1,), (1,)), ((), ())), preferred_element_type=F32)
            out.append(jnp.where(causal, sc, MASK_VALUE) if masked else sc)
        return out

    def softmax_pv(scs, k0s, carry):
        vps = [v_ref[0, pl.ds(k0, tk), :] for k0 in k0s]
        new = []
        for hh in range(2):
            m, acc = carry[hh]
            m_new = m
            for sc in scs[hh]:
                m_new = jnp.maximum(m_new, jnp.max(sc, axis=1, keepdims=True))
            acc = jnp.exp2(m - m_new) * acc
            for sc, vp in zip(scs[hh], vps):
                p = jnp.exp2(sc - m_new).astype(BF16)
                vh = jnp.where(head_lanes[hh], vp, jnp.ones_like(vp))
                acc = acc + jnp.dot(p, vh, preferred_element_type=F32)
            new.append((m_new, acc))
        return tuple(new)

    def init_carry():
        return tuple((jnp.full((tq, 1), MASK_VALUE, F32), jnp.zeros((tq, LANES), F32)) for _ in range(2))

    for qi in range(1, s // tq):
        gmax = jnp.max(g_ref[:, qi * tq:(qi + 1) * tq], axis=1, keepdims=True)
        live = (gmax - st_ref[0, 0, 0] >= -EXP2_UNDERFLOW) & (pos < qi * tq)
        n_live = jnp.max(jnp.sum(live.astype(F32), axis=1, keepdims=True), axis=0, keepdims=True)
        nback_ref[qi] = jnp.ceil(n_live * (1.0 / tk)).astype(jnp.int32)[0, 0]

    def far_blocks(qi, qh):
        def step(kj, carry):
            k0 = pl.multiple_of(kj * tk, tk)
            return softmax_pv([[sc] for sc in scores(qh, k0, False)], [k0], carry)

        return lax.fori_loop(qi - nback_ref[qi], qi - 1, step, init_carry())

    def tail_scores(slot, q0, qh):
        for bi, (k0, masked) in enumerate([(prev_block(q0), False), (q0, True)]):
            for hh, sc in enumerate(scores(qh, k0, masked)):
                s_ref[slot, hh, bi] = sc

    def tail_finish(slot, q0, carry):
        scs = [[s_ref[slot, hh, 0], s_ref[slot, hh, 1]] for hh in range(2)]
        finish(q0, softmax_pv(scs, [prev_block(q0), q0], carry))

    def finish(q0, carry):
        acc0, acc1 = carry[0][1], carry[1][1]
        out = jnp.where(head_lanes[0], acc0 / pltpu.roll(acc0, HEAD_DIM, 1), acc1 / pltpu.roll(acc1, HEAD_DIM, 1))
        o_ref[0, pl.ds(q0, tq), :] = out.astype(o_ref.dtype)

    finish(0, softmax_pv([[sc] for sc in scores(load_q(0), 0, True)], [0], init_carry()))

    n_q = s // tq
    last = n_q - 1
    tail_scores(0, tq, load_q(tq))

    def pair(i, _):
        qa = 2 * i + 1
        qb = qa + 1
        qn = jnp.minimum(qa + 2, last)
        a0, b0, n0 = (pl.multiple_of(x * tq, tq) for x in (qa, qb, qn))
        qha, qhb, qhn = load_q(a0), load_q(b0), load_q(n0)
        carry_a = far_blocks(qa, qha)
        carry_b = far_blocks(qb, qhb)
        tail_scores(1, b0, qhb)
        tail_finish(0, a0, carry_a)
        tail_scores(0, n0, qhn)
        tail_finish(1, b0, carry_b)
        return 0

    lax.fori_loop(0, last // 2, pair, 0)
    if last % 2 == 1:
        q0 = last * tq
        tail_finish(0, q0, far_blocks(last, load_q(q0)))


def _attention(q, k, kaug, v, stats, *, tq=512):
    b, s, d = q.shape
    n_groups = d // LANES
    heads_per_group = LANES // HEAD_DIM
    assert heads_per_group == 2 and stats.shape == (b, 3, n_groups, heads_per_group, s)
    kern = functools.partial(_attn_kernel, s=s, tq=tq)
    col_spec = pl.BlockSpec((1, s, LANES), lambda i, j: (i, 0, j))
    return pl.pallas_call(
        kern,
        grid=(b, n_groups),
        in_specs=[col_spec, col_spec, col_spec, col_spec,
                  pl.BlockSpec((1, 3, 1, heads_per_group, s), lambda i, j: (i, 0, j, 0, 0))],
        out_specs=col_spec,
        out_shape=jax.ShapeDtypeStruct((b, s, d), BF16),
        scratch_shapes=[pltpu.VMEM((heads_per_group, s), F32),
                        pltpu.VMEM((2, heads_per_group, 2, tq, tq), F32),
                        pltpu.SMEM((s // tq,), jnp.int32)],
        compiler_params=pltpu.CompilerParams(
            dimension_semantics=("arbitrary", "arbitrary"), vmem_limit_bytes=VMEM_LIMIT_BYTES),
        name="fox_attention",
    )(q, k, kaug, v, stats)


def kernel(x, g_mix_pre, g_mix_post, g_ffn_pre, g_ffn_post, conv_pw1_w, conv_pw1_b, conv_dw_w, conv_dw_b,
           conv_ln_g, conv_ln_b, conv_pw2_w, conv_pw2_b, attn_w_in, attn_b_f, attn_w_o, mlp_w_up, mlp_w_down):
    b, s, d = x.shape
    depth = g_mix_pre.shape[0]
    for i in range(depth):
        j = i // 2
        attn = None
        if i % 2 == 0:
            x = _conv_mixer(x, g_mix_pre[i], conv_pw1_w[j], conv_pw1_b[j], conv_dw_w[j], conv_dw_b[j],
                            conv_ln_g[j], conv_ln_b[j], conv_pw2_w[j], conv_pw2_b[j], g_mix_post[i])
        else:
            q, k, kaug, v, stats = _qkv(x, g_mix_pre[i], attn_w_in, j, attn_b_f[j])
            attn = (_attention(q, k, kaug, v, stats).reshape(b * s, d), attn_w_o[j], g_mix_post[i])
        x = _mlp(x.reshape(b * s, d), g_ffn_pre[i], mlp_w_up, mlp_w_down, i, g_ffn_post[i], attn).reshape(b, s, d)
    return x
```

```python
import functools

import jax
import jax.numpy as jnp
from jax import lax
from jax.experimental import pallas as pl
from jax.experimental.pallas import tpu as pltpu

N_HEADS = 16
HEAD_DIM = 64
CONV_WIDTH = 31
RMS_EPS = 1e-6
LN_EPS = 1e-5
MASK_VALUE = -1e30

LANES = 128
SUBLANES = 8
CONV_ROWS = 128
CONV_COLS = 256
FF_CHUNK = 1024
CONV_HALO = 32
VMEM_LIMIT_BYTES = 56 * 1024 * 1024
EXP2_UNDERFLOW = 152.0
LOG2E = 1.4426950408889634
NORM_SLACK = 1.01
DIAG_SLACK = 2.0 ** -7
N_STATS = 4

F32 = jnp.float32
BF16 = jnp.bfloat16


def _rms(x, g):
    return x * lax.rsqrt(jnp.mean(x * x, axis=-1, keepdims=True) + RMS_EPS) * g


def _const_spec(shape):
    nd = len(shape)
    return pl.BlockSpec(shape, lambda *_: (0,) * nd, pipeline_mode=pl.Buffered(1))


def _conv_mixer_kernel(x_ref, gpre_ref, w1_ref, b1_ref, dww_ref, dwb_ref, lng_ref, lnb_ref,
                       w2_ref, b2_ref, gpost_ref, o_ref, ubuf_ref, ybuf_ref, *, tm, d):
    x = x_ref[0]
    hb = _rms(x, gpre_ref[...]).astype(BF16)

    @pl.when(pl.program_id(1) == 0)
    def _():
        ubuf_ref[0:CONV_HALO, :] = jnp.zeros((CONV_HALO, d), F32)
        ubuf_ref[CONV_HALO + tm:CONV_HALO + tm + SUBLANES, :] = jnp.zeros((SUBLANES, d), F32)

    def conv_chunk(r0, c0):
        acc = jnp.broadcast_to(dwb_ref[:, c0:c0 + LANES], (CONV_ROWS, LANES))
        for r in range(SUBLANES):
            z = None
            for o in range(r if r >= 2 else r + SUBLANES, CONV_WIDTH + 2, SUBLANES):
                a8 = r0 + o - r
                term = ubuf_ref[a8:a8 + CONV_ROWS + SUBLANES, c0:c0 + LANES] * dww_ref[o - 2:o - 1, c0:c0 + LANES]
                z = term if z is None else z + term
            acc = acc + z[r:r + CONV_ROWS]
        ybuf_ref[r0:r0 + CONV_ROWS, c0:c0 + LANES] = acc

    for g0 in range(0, d, CONV_COLS):
        cols = slice(g0, g0 + CONV_COLS)
        gate_cols = slice(d + g0, d + g0 + CONV_COLS)
        a_u = jnp.dot(hb, w1_ref[:, cols], preferred_element_type=F32) + b1_ref[:, cols]
        a_g = jnp.dot(hb, w1_ref[:, gate_cols], preferred_element_type=F32) + b1_ref[:, gate_cols]
        ubuf_ref[CONV_HALO:CONV_HALO + tm, cols] = a_u * jax.nn.sigmoid(a_g)
        for c0 in range(g0, g0 + CONV_COLS, LANES):
            for r0 in range(0, tm, CONV_ROWS):
                conv_chunk(r0, c0)
        ubuf_ref[0:CONV_HALO, cols] = ubuf_ref[tm:tm + CONV_HALO, cols]

    y = ybuf_ref[...]
    mu = jnp.mean(y, axis=-1, keepdims=True)
    yc = y - mu
    var = jnp.mean(yc * yc, axis=-1, keepdims=True)
    yn = yc * lax.rsqrt(var + LN_EPS) * lng_ref[...] + lnb_ref[...]
    act = yn * jax.nn.sigmoid(yn)
    m = jnp.dot(act.astype(BF16), w2_ref[...], preferred_element_type=F32) + b2_ref[...]
    o_ref[0] = x + _rms(m, gpost_ref[...])


def _conv_mixer(x, gpre, w1, b1, dww, dwb, lng, lnb, w2, b2, gpost, *, tm=512):
    b, s, d = x.shape
    row = lambda v: v.reshape(1, -1).astype(F32)
    kern = functools.partial(_conv_mixer_kernel, tm=tm, d=d)
    return pl.pallas_call(
        kern,
        grid=(b, s // tm),
        in_specs=[
            pl.BlockSpec((1, tm, d), lambda i, j: (i, j, 0)),
            _const_spec((1, d)), _const_spec((d, 2 * d)), _const_spec((1, 2 * d)),
            _const_spec((CONV_WIDTH, d)), _const_spec((1, d)), _const_spec((1, d)), _const_spec((1, d)),
            _const_spec((d, d)), _const_spec((1, d)), _const_spec((1, d)),
        ],
        out_specs=pl.BlockSpec((1, tm, d), lambda i, j: (i, j, 0)),
        out_shape=jax.ShapeDtypeStruct(x.shape, F32),
        scratch_shapes=[pltpu.VMEM((CONV_HALO + tm + SUBLANES, d), F32), pltpu.VMEM((tm, d), F32)],
        compiler_params=pltpu.CompilerParams(
            dimension_semantics=("arbitrary", "arbitrary"), vmem_limit_bytes=VMEM_LIMIT_BYTES),
        name="conv_mixer",
    )(x, row(gpre), w1.astype(BF16), row(b1), dww.astype(F32), row(dwb), row(lng), row(lnb),
      w2.astype(BF16), row(b2), row(gpost))


def _mlp_body(x, gpre_ref, wup_ref, wdown_ref, gpost_ref, o_ref):
    hb = _rms(x, gpre_ref[...]).astype(BF16)
    down = None
    for c0 in range(0, wup_ref.shape[1], FF_CHUNK):
        up = jnp.dot(hb, wup_ref[:, c0:c0 + FF_CHUNK].astype(BF16), preferred_element_type=F32)
        r = jnp.maximum(up, 0.0)
        part = jnp.dot((r * r).astype(BF16), wdown_ref[c0:c0 + FF_CHUNK, :].astype(BF16), preferred_element_type=F32)
        down = part if down is None else down + part
    o_ref[...] = x + _rms(down, gpost_ref[...])


def _mlp_kernel(x_ref, gpre_ref, wup_ref, wdown_ref, gpost_ref, o_ref):
    _mlp_body(x_ref[...], gpre_ref, wup_ref, wdown_ref, gpost_ref, o_ref)


def _attn_out_mlp_kernel(x_ref, a_ref, wo_ref, gmix_ref, gpre_ref, wup_ref, wdown_ref, gpost_ref, o_ref):
    m = jnp.dot(a_ref[...], wo_ref[...], preferred_element_type=F32)
    _mlp_body(x_ref[...] + _rms(m, gmix_ref[...]), gpre_ref, wup_ref, wdown_ref, gpost_ref, o_ref)


def _mlp(x2d, gpre, wup_all, wdown_all, layer, gpost, attn=None, *, tm=512):
    t, d = x2d.shape
    row = lambda v: v.reshape(1, -1).astype(F32)
    act_spec = pl.BlockSpec((tm, d), lambda i: (i, 0))
    mlp_specs = [_const_spec((1, d)), _layer_spec(wup_all, layer), _layer_spec(wdown_all, layer), _const_spec((1, d))]
    mlp_args = (row(gpre), wup_all.astype(F32), wdown_all.astype(F32), row(gpost))
    if attn is None:
        kern, in_specs, args = _mlp_kernel, [act_spec] + mlp_specs, (x2d,) + mlp_args
    else:
        a2d, wo, gmix = attn
        kern = _attn_out_mlp_kernel
        in_specs = [act_spec, act_spec, _const_spec((d, d)), _const_spec((1, d))] + mlp_specs
        args = (x2d, a2d, wo.astype(BF16), row(gmix)) + mlp_args
    return pl.pallas_call(
        kern,
        grid=(t // tm,),
        in_specs=in_specs,
        out_specs=act_spec,
        out_shape=jax.ShapeDtypeStruct(x2d.shape, F32),
        compiler_params=pltpu.CompilerParams(
            dimension_semantics=("arbitrary",), vmem_limit_bytes=VMEM_LIMIT_BYTES),
        name="sqrelu_mlp" if attn is None else "attn_out_mlp",
    )(*args)


def _split3_bf16(x):
    hi = x.astype(BF16)
    r1 = x - hi.astype(F32)
    mid = r1.astype(BF16)
    lo = (r1 - mid.astype(F32)).astype(BF16)
    return hi, mid, lo


def _head_sq_norm_bound(xb, ind):
    xf = xb.astype(F32)
    nsq = jnp.dot((xf * xf).astype(BF16), ind, preferred_element_type=F32)
    return nsq * NORM_SLACK


def _qkv_kernel(x_ref, gpre_ref, win_ref, wf_ref, bf_ref, ind_ref, sel_ref, q_ref, k_ref, ka_ref, v_ref, st_ref,
                carry_ref, *, tm, d):
    x = x_ref[0]
    hb = _rms(x, gpre_ref[...]).astype(BF16)
    proj = jnp.dot(hb, win_ref[:, :3 * d].astype(BF16), preferred_element_type=F32)
    qb = (proj[:, :d] * (HEAD_DIM ** -0.5 * LOG2E)).astype(BF16)
    kb = proj[:, d:2 * d].astype(BF16)
    q_ref[0] = qb
    k_ref[0] = kb
    v_ref[0] = proj[:, 2 * d:].astype(BF16)

    def put_stat(t, per_head):
        rows = per_head.T
        for g in range(N_HEADS // 2):
            st_ref[0, t, g] = rows[2 * g:2 * g + 2, :]

    put_stat(1, _head_sq_norm_bound(qb, ind_ref[...]))
    put_stat(2, _head_sq_norm_bound(kb, ind_ref[...]))
    qk = (qb.astype(F32) * kb.astype(F32)).astype(BF16)
    put_stat(3, jnp.dot(qk, ind_ref[...], preferred_element_type=F32))

    f_logit = jnp.dot(hb, wf_ref[...], preferred_element_type=F32) + bf_ref[...]
    log_f = jax.nn.log_sigmoid(f_logit)

    @pl.when(pl.program_id(1) == 0)
    def _():
        carry_ref[...] = jnp.zeros_like(carry_ref)

    rows = lax.broadcasted_iota(jnp.int32, (tm, tm), 0)
    cols = lax.broadcasted_iota(jnp.int32, (tm, tm), 1)
    tri = (rows >= cols).astype(BF16)
    lane = lax.broadcasted_iota(jnp.int32, (1, LANES), 1)

    def pack3(v):
        v_hi, v_mid, v_lo = (t.astype(F32) for t in _split3_bf16(v))
        return jnp.where(lane < N_HEADS, v_hi, jnp.where(
            lane < 2 * N_HEADS, pltpu.roll(v_mid, N_HEADS, 1), pltpu.roll(v_lo, 2 * N_HEADS, 1))).astype(BF16)

    c3 = jnp.dot(tri, pack3(log_f), preferred_element_type=F32)
    csum = c3 + pltpu.roll(c3, LANES - N_HEADS, 1) + pltpu.roll(c3, LANES - 2 * N_HEADS, 1)
    f_cum = csum + carry_ref[...]
    carry_ref[...] = f_cum[tm - 1:tm, :]
    f2 = f_cum * LOG2E
    put_stat(0, f2)
    ka_ref[0] = jnp.dot(pack3(f2), sel_ref[...], preferred_element_type=F32).astype(BF16)


def _layer_spec(stacked, layer):
    return pl.BlockSpec((None,) + stacked.shape[1:], lambda *_: (layer, 0, 0), pipeline_mode=pl.Buffered(1))


def _qkv(x, gpre, w_in_all, layer, b_f, *, tm=512):
    b, s, d = x.shape
    w_in = w_in_all[layer]
    wf = jnp.zeros((d, LANES), F32).at[:, :N_HEADS].set(w_in[:, 3 * d:]).astype(BF16)
    bf = jnp.zeros((1, LANES), F32).at[0, :N_HEADS].set(b_f.astype(F32))
    ind = (jnp.arange(d)[:, None] // HEAD_DIM == jnp.arange(LANES)[None, :]).astype(BF16)
    rows = jnp.arange(LANES)[:, None]
    head, term = rows % N_HEADS, rows // N_HEADS
    dest = (head // 2) * LANES + (1 - head % 2) * HEAD_DIM
    sel = -((jnp.arange(d)[None, :] == dest + term) & (term < 3)).astype(BF16)
    kern = functools.partial(_qkv_kernel, tm=tm, d=d)
    act_spec = pl.BlockSpec((1, tm, d), lambda i, j: (i, j, 0))
    return pl.pallas_call(
        kern,
        grid=(b, s // tm),
        in_specs=[act_spec, _const_spec((1, d)), _layer_spec(w_in_all, layer), _const_spec((d, LANES)),
                  _const_spec((1, LANES)), _const_spec((d, LANES)), _const_spec((LANES, d))],
        out_specs=[act_spec, act_spec, act_spec, act_spec,
                   pl.BlockSpec((1, N_STATS, N_HEADS // 2, 2, tm), lambda i, j: (i, 0, 0, 0, j))],
        out_shape=[jax.ShapeDtypeStruct((b, s, d), BF16)] * 4
        + [jax.ShapeDtypeStruct((b, N_STATS, N_HEADS // 2, 2, s), F32)],
        scratch_shapes=[pltpu.VMEM((1, LANES), F32)],
        compiler_params=pltpu.CompilerParams(
            dimension_semantics=("arbitrary", "arbitrary"), vmem_limit_bytes=VMEM_LIMIT_BYTES),
        name="attn_qkv",
    )(x, gpre.reshape(1, -1).astype(F32), w_in_all.astype(F32), wf, bf, ind, sel)


def _attn_kernel(q_ref, k_ref, ka_ref, v_ref, st_ref, o_ref, g_ref, s_ref, nback_ref, *, s, tq):
    tk = tq
    lane = lax.broadcasted_iota(jnp.int32, (1, LANES), 1)
    head_lanes = [lane < HEAD_DIM, lane >= HEAD_DIM]
    bias_ones = [((lane >= HEAD_DIM) & (lane < HEAD_DIM + 3)).astype(BF16), (lane < 3).astype(BF16)]
    row_ids = lax.broadcasted_iota(jnp.int32, (tq, tk), 0)
    col_ids = lax.broadcasted_iota(jnp.int32, (tq, tk), 1)
    causal = col_ids <= row_ids
    pos = lax.broadcasted_iota(jnp.int32, (1, s), 1)

    nq, nk = st_ref[0, 1, 0], st_ref[0, 2, 0]
    kmax = jnp.max(nk, axis=1, keepdims=True)
    diag_lb = st_ref[0, 3, 0] - DIAG_SLACK * jnp.sqrt(nq * nk)
    g_ref[...] = jnp.sqrt(nq) * jnp.sqrt(kmax) - diag_lb + st_ref[0, 0, 0]

    def prev_block(q0):
        return q0 - tk if isinstance(q0, int) else pl.multiple_of(q0 - tk, tk)

    def load_q(q0):
        qp = q_ref[0, pl.ds(q0, tq), :]
        return [jnp.where(head_lanes[hh], qp, bias_ones[hh]) for hh in range(2)]

    def scores(qh, k0, masked):
        kp = k_ref[0, pl.ds(k0, tk), :]
        ap = ka_ref[0, pl.ds(k0, tk), :]
        out = []
        for hh in range(2):
            kh = jnp.where(head_lanes[hh], kp, ap)
            sc = lax.dot_general(qh[hh], kh, (((1,), (1,)), ((), ())), preferred_element_type=F32)
            out.append(jnp.where(causal, sc, MASK_VALUE) if masked else sc)
        return out

    def softmax_pv(scs, k0s, carry):
        vps = [v_ref[0, pl.ds(k0, tk), :] for k0 in k0s]
        new = []
        for hh in range(2):
            m, acc = carry[hh]
            m_new = m
            for sc in scs[hh]:
                m_new = jnp.maximum(m_new, jnp.max(sc, axis=1, keepdims=True))
            acc = jnp.exp2(m - m_new) * acc
            for sc, vp in zip(scs[hh], vps):
                p = jnp.exp2(sc - m_new).astype(BF16)
                vh = jnp.where(head_lanes[hh], vp, jnp.ones_like(vp))
                acc = acc + jnp.dot(p, vh, preferred_element_type=F32)
            new.append((m_new, acc))
        return tuple(new)

    def init_carry():
        return tuple((jnp.full((tq, 1), MASK_VALUE, F32), jnp.zeros((tq, LANES), F32)) for _ in range(2))

    for qi in range(1, s // tq):
        gmax = jnp.max(g_ref[:, qi * tq:(qi + 1) * tq], axis=1, keepdims=True)
        live = (gmax - st_ref[0, 0, 0] >= -EXP2_UNDERFLOW) & (pos < qi * tq)
        n_live = jnp.max(jnp.sum(live.astype(F32), axis=1, keepdims=True), axis=0, keepdims=True)
        nback_ref[qi] = jnp.ceil(n_live * (1.0 / tk)).astype(jnp.int32)[0, 0]

    def far_blocks(qi, qh):
        def step(kj, carry):
            k0 = pl.multiple_of(kj * tk, tk)
            return softmax_pv([[sc] for sc in scores(qh, k0, False)], [k0], carry)

        return lax.fori_loop(qi - nback_ref[qi], qi - 1, step, init_carry())

    def tail_scores(slot, q0, qh):
        for bi, (k0, masked) in enumerate([(prev_block(q0), False), (q0, True)]):
            for hh, sc in enumerate(scores(qh, k0, masked)):
                s_ref[slot, hh, bi] = sc

    def tail_finish(slot, q0, carry):
        scs = [[s_ref[slot, hh, 0], s_ref[slot, hh, 1]] for hh in range(2)]
        finish(q0, softmax_pv(scs, [prev_block(q0), q0], carry))

    def finish(q0, carry):
        acc0, acc1 = carry[0][1], carry[1][1]
        out = jnp.where(head_lanes[0], acc0 / pltpu.roll(acc0, HEAD_DIM, 1), acc1 / pltpu.roll(acc1, HEAD_DIM, 1))
        o_ref[0, pl.ds(q0, tq), :] = out.astype(o_ref.dtype)

    finish(0, softmax_pv([[sc] for sc in scores(load_q(0), 0, True)], [0], init_carry()))

    n_q = s // tq
    last = n_q - 1
    tail_scores(0, tq, load_q(tq))

    def pair(i, _):
        qa = 2 * i + 1
        qb = qa + 1
        qn = jnp.minimum(qa + 2, last)
        a0, b0, n0 = (pl.multiple_of(x * tq, tq) for x in (qa, qb, qn))
        qha, qhb, qhn = load_q(a0), load_q(b0), load_q(n0)
        carry_a = far_blocks(qa, qha)
        carry_b = far_blocks(qb, qhb)
        tail_scores(1, b0, qhb)
        tail_finish(0, a0, carry_a)
        tail_scores(0, n0, qhn)
        tail_finish(1, b0, carry_b)
        return 0

    lax.fori_loop(0, last // 2, pair, 0)
    if last % 2 == 1:
        q0 = last * tq
        tail_finish(0, q0, far_blocks(last, load_q(q0)))


def _attention(q, k, kaug, v, stats, *, tq=512):
    b, s, d = q.shape
    n_groups = d // LANES
    heads_per_group = LANES // HEAD_DIM
    assert heads_per_group == 2 and stats.shape == (b, N_STATS, n_groups, heads_per_group, s)
    kern = functools.partial(_attn_kernel, s=s, tq=tq)
    col_spec = pl.BlockSpec((1, s, LANES), lambda i, j: (i, 0, j))
    return pl.pallas_call(
        kern,
        grid=(b, n_groups),
        in_specs=[col_spec, col_spec, col_spec, col_spec,
                  pl.BlockSpec((1, N_STATS, 1, heads_per_group, s), lambda i, j: (i, 0, j, 0, 0))],
        out_specs=col_spec,
        out_shape=jax.ShapeDtypeStruct((b, s, d), BF16),
        scratch_shapes=[pltpu.VMEM((heads_per_group, s), F32),
                        pltpu.VMEM((2, heads_per_group, 2, tq, tq), F32),
                        pltpu.SMEM((s // tq,), jnp.int32)],
        compiler_params=pltpu.CompilerParams(
            dimension_semantics=("arbitrary", "arbitrary"), vmem_limit_bytes=VMEM_LIMIT_BYTES),
        name="fox_attention",
    )(q, k, kaug, v, stats)


def kernel(x, g_mix_pre, g_mix_post, g_ffn_pre, g_ffn_post, conv_pw1_w, conv_pw1_b, conv_dw_w, conv_dw_b,
           conv_ln_g, conv_ln_b, conv_pw2_w, conv_pw2_b, attn_w_in, attn_b_f, attn_w_o, mlp_w_up, mlp_w_down):
    b, s, d = x.shape
    depth = g_mix_pre.shape[0]
    for i in range(depth):
        j = i // 2
        attn = None
        if i % 2 == 0:
            x = _conv_mixer(x, g_mix_pre[i], conv_pw1_w[j], conv_pw1_b[j], conv_dw_w[j], conv_dw_b[j],
                            conv_ln_g[j], conv_ln_b[j], conv_pw2_w[j], conv_pw2_b[j], g_mix_post[i])
        else:
            q, k, kaug, v, stats = _qkv(x, g_mix_pre[i], attn_w_in, j, attn_b_f[j])
            attn = (_attention(q, k, kaug, v, stats).reshape(b * s, d), attn_w_o[j], g_mix_post[i])
        x = _mlp(x.reshape(b * s, d), g_ffn_pre[i], mlp_w_up, mlp_w_down, i, g_ffn_post[i], attn).reshape(b, s, d)
    return x
```

```python
import functools

import jax
import jax.numpy as jnp
from jax import lax
from jax.experimental import pallas as pl
from jax.experimental.pallas import tpu as pltpu

N_HEADS = 16
HEAD_DIM = 64
CONV_WIDTH = 31
RMS_EPS = 1e-6
LN_EPS = 1e-5
MASK_VALUE = -1e30

LANES = 128
SUBLANES = 8
CONV_ROWS = 128
CONV_COLS = 256
FF_CHUNK = 1024
CONV_HALO = 32
VMEM_LIMIT_BYTES = 56 * 1024 * 1024
EXP2_UNDERFLOW = 150.5
LOG2E = 1.4426950408889634
NORM_SLACK = 1.01
DIAG_SLACK = 2.0 ** -7
N_STATS = 4

F32 = jnp.float32
BF16 = jnp.bfloat16


def _rms(x, g):
    return x * lax.rsqrt(jnp.mean(x * x, axis=-1, keepdims=True) + RMS_EPS) * g


def _const_spec(shape):
    nd = len(shape)
    return pl.BlockSpec(shape, lambda *_: (0,) * nd, pipeline_mode=pl.Buffered(1))


def _conv_mixer_kernel(x_ref, gpre_ref, w1_ref, b1_ref, dww_ref, dwb_ref, lng_ref, lnb_ref,
                       w2_ref, b2_ref, gpost_ref, o_ref, ubuf_ref, ybuf_ref, *, tm, d):
    x = x_ref[0]
    hb = _rms(x, gpre_ref[...]).astype(BF16)

    @pl.when(pl.program_id(1) == 0)
    def _():
        ubuf_ref[0:CONV_HALO, :] = jnp.zeros((CONV_HALO, d), F32)
        ubuf_ref[CONV_HALO + tm:CONV_HALO + tm + SUBLANES, :] = jnp.zeros((SUBLANES, d), F32)

    def conv_chunk(r0, c0):
        acc = jnp.broadcast_to(dwb_ref[:, c0:c0 + LANES], (CONV_ROWS, LANES))
        for r in range(SUBLANES):
            z = None
            for o in range(r if r >= 2 else r + SUBLANES, CONV_WIDTH + 2, SUBLANES):
                a8 = r0 + o - r
                term = ubuf_ref[a8:a8 + CONV_ROWS + SUBLANES, c0:c0 + LANES] * dww_ref[o - 2:o - 1, c0:c0 + LANES]
                z = term if z is None else z + term
            acc = acc + z[r:r + CONV_ROWS]
        ybuf_ref[r0:r0 + CONV_ROWS, c0:c0 + LANES] = acc

    for g0 in range(0, d, CONV_COLS):
        cols = slice(g0, g0 + CONV_COLS)
        gate_cols = slice(d + g0, d + g0 + CONV_COLS)
        a_u = jnp.dot(hb, w1_ref[:, cols], preferred_element_type=F32) + b1_ref[:, cols]
        a_g = jnp.dot(hb, w1_ref[:, gate_cols], preferred_element_type=F32) + b1_ref[:, gate_cols]
        ubuf_ref[CONV_HALO:CONV_HALO + tm, cols] = a_u * jax.nn.sigmoid(a_g)
        for c0 in range(g0, g0 + CONV_COLS, LANES):
            for r0 in range(0, tm, CONV_ROWS):
                conv_chunk(r0, c0)
        ubuf_ref[0:CONV_HALO, cols] = ubuf_ref[tm:tm + CONV_HALO, cols]

    y = ybuf_ref[...]
    mu = jnp.mean(y, axis=-1, keepdims=True)
    yc = y - mu
    var = jnp.mean(yc * yc, axis=-1, keepdims=True)
    yn = yc * lax.rsqrt(var + LN_EPS) * lng_ref[...] + lnb_ref[...]
    act = yn * jax.nn.sigmoid(yn)
    m = jnp.dot(act.astype(BF16), w2_ref[...], preferred_element_type=F32) + b2_ref[...]
    o_ref[0] = x + _rms(m, gpost_ref[...])


def _conv_mixer(x, gpre, w1, b1, dww, dwb, lng, lnb, w2, b2, gpost, *, tm=512):
    b, s, d = x.shape
    row = lambda v: v.reshape(1, -1).astype(F32)
    kern = functools.partial(_conv_mixer_kernel, tm=tm, d=d)
    return pl.pallas_call(
        kern,
        grid=(b, s // tm),
        in_specs=[
            pl.BlockSpec((1, tm, d), lambda i, j: (i, j, 0)),
            _const_spec((1, d)), _const_spec((d, 2 * d)), _const_spec((1, 2 * d)),
            _const_spec((CONV_WIDTH, d)), _const_spec((1, d)), _const_spec((1, d)), _const_spec((1, d)),
            _const_spec((d, d)), _const_spec((1, d)), _const_spec((1, d)),
        ],
        out_specs=pl.BlockSpec((1, tm, d), lambda i, j: (i, j, 0)),
        out_shape=jax.ShapeDtypeStruct(x.shape, F32),
        scratch_shapes=[pltpu.VMEM((CONV_HALO + tm + SUBLANES, d), F32), pltpu.VMEM((tm, d), F32)],
        compiler_params=pltpu.CompilerParams(
            dimension_semantics=("arbitrary", "arbitrary"), vmem_limit_bytes=VMEM_LIMIT_BYTES),
        name="conv_mixer",
    )(x, row(gpre), w1.astype(BF16), row(b1), dww.astype(F32), row(dwb), row(lng), row(lnb),
      w2.astype(BF16), row(b2), row(gpost))


def _mlp_body(x, gpre_ref, wup_ref, wdown_ref, gpost_ref, o_ref):
    hb = _rms(x, gpre_ref[...]).astype(BF16)
    down = None
    for c0 in range(0, wup_ref.shape[1], FF_CHUNK):
        up = jnp.dot(hb, wup_ref[:, c0:c0 + FF_CHUNK].astype(BF16), preferred_element_type=F32)
        r = jnp.maximum(up, 0.0)
        part = jnp.dot((r * r).astype(BF16), wdown_ref[c0:c0 + FF_CHUNK, :].astype(BF16), preferred_element_type=F32)
        down = part if down is None else down + part
    o_ref[...] = x + _rms(down, gpost_ref[...])


def _mlp_kernel(x_ref, gpre_ref, wup_ref, wdown_ref, gpost_ref, o_ref):
    _mlp_body(x_ref[...], gpre_ref, wup_ref, wdown_ref, gpost_ref, o_ref)


def _attn_out_mlp_kernel(x_ref, a_ref, wo_ref, gmix_ref, gpre_ref, wup_ref, wdown_ref, gpost_ref, o_ref):
    m = jnp.dot(a_ref[...], wo_ref[...], preferred_element_type=F32)
    _mlp_body(x_ref[...] + _rms(m, gmix_ref[...]), gpre_ref, wup_ref, wdown_ref, gpost_ref, o_ref)


def _mlp(x2d, gpre, wup_all, wdown_all, layer, gpost, attn=None, *, tm=512):
    t, d = x2d.shape
    row = lambda v: v.reshape(1, -1).astype(F32)
    act_spec = pl.BlockSpec((tm, d), lambda i: (i, 0))
    mlp_specs = [_const_spec((1, d)), _layer_spec(wup_all, layer), _layer_spec(wdown_all, layer), _const_spec((1, d))]
    mlp_args = (row(gpre), wup_all.astype(F32), wdown_all.astype(F32), row(gpost))
    if attn is None:
        kern, in_specs, args = _mlp_kernel, [act_spec] + mlp_specs, (x2d,) + mlp_args
    else:
        a2d, wo, gmix = attn
        kern = _attn_out_mlp_kernel
        in_specs = [act_spec, act_spec, _const_spec((d, d)), _const_spec((1, d))] + mlp_specs
        args = (x2d, a2d, wo.astype(BF16), row(gmix)) + mlp_args
    return pl.pallas_call(
        kern,
        grid=(t // tm,),
        in_specs=in_specs,
        out_specs=act_spec,
        out_shape=jax.ShapeDtypeStruct(x2d.shape, F32),
        compiler_params=pltpu.CompilerParams(
            dimension_semantics=("arbitrary",), vmem_limit_bytes=VMEM_LIMIT_BYTES),
        name="sqrelu_mlp" if attn is None else "attn_out_mlp",
    )(*args)


def _split3_bf16(x):
    hi = x.astype(BF16)
    r1 = x - hi.astype(F32)
    mid = r1.astype(BF16)
    lo = (r1 - mid.astype(F32)).astype(BF16)
    return hi, mid, lo


def _head_sq_norm_bound(xb, ind):
    xf = xb.astype(F32)
    nsq = jnp.dot((xf * xf).astype(BF16), ind, preferred_element_type=F32)
    return nsq * NORM_SLACK


def _qkv_kernel(x_ref, gpre_ref, win_ref, wf_ref, bf_ref, ind_ref, sel_ref, q_ref, k_ref, ka_ref, v_ref, st_ref,
                carry_ref, *, tm, d):
    x = x_ref[0]
    hb = _rms(x, gpre_ref[...]).astype(BF16)
    proj = jnp.dot(hb, win_ref[:, :3 * d], preferred_element_type=F32)
    qb = (proj[:, :d] * (HEAD_DIM ** -0.5 * LOG2E)).astype(BF16)
    kb = proj[:, d:2 * d].astype(BF16)
    q_ref[0] = qb
    k_ref[0] = kb
    v_ref[0] = proj[:, 2 * d:].astype(BF16)

    def put_stat(t, per_head):
        rows = per_head.T
        for g in range(N_HEADS // 2):
            st_ref[0, t, g] = rows[2 * g:2 * g + 2, :]

    put_stat(1, _head_sq_norm_bound(qb, ind_ref[...]))
    put_stat(2, _head_sq_norm_bound(kb, ind_ref[...]))
    qk = (qb.astype(F32) * kb.astype(F32)).astype(BF16)
    put_stat(3, jnp.dot(qk, ind_ref[...], preferred_element_type=F32))

    f_logit = jnp.dot(hb, wf_ref[...], preferred_element_type=F32) + bf_ref[...]
    log_f = jax.nn.log_sigmoid(f_logit)

    @pl.when(pl.program_id(1) == 0)
    def _():
        carry_ref[...] = jnp.zeros_like(carry_ref)

    rows = lax.broadcasted_iota(jnp.int32, (tm, tm), 0)
    cols = lax.broadcasted_iota(jnp.int32, (tm, tm), 1)
    tri = (rows >= cols).astype(BF16)
    lane = lax.broadcasted_iota(jnp.int32, (1, LANES), 1)

    def pack3(v):
        v_hi, v_mid, v_lo = (t.astype(F32) for t in _split3_bf16(v))
        return jnp.where(lane < N_HEADS, v_hi, jnp.where(
            lane < 2 * N_HEADS, pltpu.roll(v_mid, N_HEADS, 1), pltpu.roll(v_lo, 2 * N_HEADS, 1))).astype(BF16)

    c3 = jnp.dot(tri, pack3(log_f), preferred_element_type=F32)
    csum = c3 + pltpu.roll(c3, LANES - N_HEADS, 1) + pltpu.roll(c3, LANES - 2 * N_HEADS, 1)
    f_cum = csum + carry_ref[...]
    carry_ref[...] = f_cum[tm - 1:tm, :]
    f2 = f_cum * LOG2E
    put_stat(0, f2)
    ka_ref[0] = jnp.dot(pack3(f2), sel_ref[...], preferred_element_type=F32).astype(BF16)


def _layer_spec(stacked, layer):
    return pl.BlockSpec((None,) + stacked.shape[1:], lambda *_: (layer, 0, 0), pipeline_mode=pl.Buffered(1))


def _qkv(x, gpre, w_in_all, layer, b_f, *, tm=512):
    b, s, d = x.shape
    w_in = w_in_all[layer]
    wf = jnp.zeros((d, LANES), F32).at[:, :N_HEADS].set(w_in[:, 3 * d:]).astype(BF16)
    bf = jnp.zeros((1, LANES), F32).at[0, :N_HEADS].set(b_f.astype(F32))
    ind = (jnp.arange(d)[:, None] // HEAD_DIM == jnp.arange(LANES)[None, :]).astype(BF16)
    rows = jnp.arange(LANES)[:, None]
    head, term = rows % N_HEADS, rows // N_HEADS
    dest = (head // 2) * LANES + (1 - head % 2) * HEAD_DIM
    sel = -((jnp.arange(d)[None, :] == dest + term) & (term < 3)).astype(BF16)
    kern = functools.partial(_qkv_kernel, tm=tm, d=d)
    act_spec = pl.BlockSpec((1, tm, d), lambda i, j: (i, j, 0))
    return pl.pallas_call(
        kern,
        grid=(b, s // tm),
        in_specs=[act_spec, _const_spec((1, d)), _layer_spec(w_in_all, layer), _const_spec((d, LANES)),
                  _const_spec((1, LANES)), _const_spec((d, LANES)), _const_spec((LANES, d))],
        out_specs=[act_spec, act_spec, act_spec, act_spec,
                   pl.BlockSpec((1, N_STATS, N_HEADS // 2, 2, tm), lambda i, j: (i, 0, 0, 0, j))],
        out_shape=[jax.ShapeDtypeStruct((b, s, d), BF16)] * 4
        + [jax.ShapeDtypeStruct((b, N_STATS, N_HEADS // 2, 2, s), F32)],
        scratch_shapes=[pltpu.VMEM((1, LANES), F32)],
        compiler_params=pltpu.CompilerParams(
            dimension_semantics=("arbitrary", "arbitrary"), vmem_limit_bytes=VMEM_LIMIT_BYTES),
        name="attn_qkv",
    )(x, gpre.reshape(1, -1).astype(F32), w_in_all.astype(BF16), wf, bf, ind, sel)


def _attn_kernel(q_ref, k_ref, ka_ref, v_ref, st_ref, o_ref, g_ref, s_ref, nback_ref, *, s, tq):
    tk = tq
    lane = lax.broadcasted_iota(jnp.int32, (1, LANES), 1)
    head_lanes = [lane < HEAD_DIM, lane >= HEAD_DIM]
    bias_ones = [((lane >= HEAD_DIM) & (lane < HEAD_DIM + 3)).astype(BF16), (lane < 3).astype(BF16)]
    row_ids = lax.broadcasted_iota(jnp.int32, (tq, tk), 0)
    col_ids = lax.broadcasted_iota(jnp.int32, (tq, tk), 1)
    causal = col_ids <= row_ids
    pos = lax.broadcasted_iota(jnp.int32, (1, s), 1)

    nq, nk = st_ref[0, 1, 0], st_ref[0, 2, 0]
    kmax = jnp.max(nk, axis=1, keepdims=True)
    diag_lb = st_ref[0, 3, 0] - DIAG_SLACK * jnp.sqrt(nq * nk)
    g_ref[...] = jnp.sqrt(nq) * jnp.sqrt(kmax) - diag_lb + st_ref[0, 0, 0]

    def prev_block(q0):
        return q0 - tk if isinstance(q0, int) else pl.multiple_of(q0 - tk, tk)

    def load_q(q0):
        qp = q_ref[0, pl.ds(q0, tq), :]
        return [jnp.where(head_lanes[hh], qp, bias_ones[hh]) for hh in range(2)]

    def scores(qh, k0, masked):
        kp = k_ref[0, pl.ds(k0, tk), :]
        ap = ka_ref[0, pl.ds(k0, tk), :]
        out = []
        for hh in range(2):
            kh = jnp.where(head_lanes[hh], kp, ap)
            sc = lax.dot_general(qh[hh], kh, (((1,), (1,)), ((), ())), preferred_element_type=F32)
            out.append(jnp.where(causal, sc, MASK_VALUE) if masked else sc)
        return out

    def softmax_pv(scs, k0s, carry):
        vps = [v_ref[0, pl.ds(k0, tk), :] for k0 in k0s]
        new = []
        for hh in range(2):
            m, acc = carry[hh]
            m_new = m
            for sc in scs[hh]:
                m_new = jnp.maximum(m_new, jnp.max(sc, axis=1, keepdims=True))
            acc = jnp.exp2(m - m_new) * acc
            for sc, vp in zip(scs[hh], vps):
                p = jnp.exp2(sc - m_new).astype(BF16)
                vh = jnp.where(head_lanes[hh], vp, jnp.ones_like(vp))
                acc = acc + jnp.dot(p, vh, preferred_element_type=F32)
            new.append((m_new, acc))
        return tuple(new)

    def init_carry():
        return tuple((jnp.full((tq, 1), MASK_VALUE, F32), jnp.zeros((tq, LANES), F32)) for _ in range(2))

    for qi in range(1, s // tq):
        gmax = jnp.max(g_ref[:, qi * tq:(qi + 1) * tq], axis=1, keepdims=True)
        live = (gmax - st_ref[0, 0, 0] >= -EXP2_UNDERFLOW) & (pos < qi * tq)
        n_live = jnp.max(jnp.sum(live.astype(F32), axis=1, keepdims=True), axis=0, keepdims=True)
        nback_ref[qi] = jnp.ceil(n_live * (1.0 / tk)).astype(jnp.int32)[0, 0]

    def far_blocks(qi, qh):
        def step(kj, carry):
            k0 = pl.multiple_of(kj * tk, tk)
            return softmax_pv([[sc] for sc in scores(qh, k0, False)], [k0], carry)

        return lax.fori_loop(qi - nback_ref[qi], qi - 1, step, init_carry())

    def tail_scores(slot, q0, qh):
        for bi, (k0, masked) in enumerate([(prev_block(q0), False), (q0, True)]):
            for hh, sc in enumerate(scores(qh, k0, masked)):
                s_ref[slot, hh, bi] = sc

    def tail_finish(slot, q0, carry):
        scs = [[s_ref[slot, hh, 0], s_ref[slot, hh, 1]] for hh in range(2)]
        finish(q0, softmax_pv(scs, [prev_block(q0), q0], carry))

    def finish(q0, carry):
        acc0, acc1 = carry[0][1], carry[1][1]
        out = jnp.where(head_lanes[0], acc0 / pltpu.roll(acc0, HEAD_DIM, 1), acc1 / pltpu.roll(acc1, HEAD_DIM, 1))
        o_ref[0, pl.ds(q0, tq), :] = out.astype(o_ref.dtype)

    finish(0, softmax_pv([[sc] for sc in scores(load_q(0), 0, True)], [0], init_carry()))

    n_q = s // tq
    last = n_q - 1
    tail_scores(0, tq, load_q(tq))

    def pair(i, _):
        qa = 2 * i + 1
        qb = qa + 1
        qn = jnp.minimum(qa + 2, last)
        a0, b0, n0 = (pl.multiple_of(x * tq, tq) for x in (qa, qb, qn))
        qha, qhb, qhn = load_q(a0), load_q(b0), load_q(n0)
        carry_a = far_blocks(qa, qha)
        carry_b = far_blocks(qb, qhb)
        tail_scores(1, b0, qhb)
        tail_finish(0, a0, carry_a)
        tail_scores(0, n0, qhn)
        tail_finish(1, b0, carry_b)
        return 0

    lax.fori_loop(0, last // 2, pair, 0)
    if last % 2 == 1:
        q0 = last * tq
        tail_finish(0, q0, far_blocks(last, load_q(q0)))


def _attention(q, k, kaug, v, stats, *, tq=512):
    b, s, d = q.shape
    n_groups = d // LANES
    heads_per_group = LANES // HEAD_DIM
    assert heads_per_group == 2 and stats.shape == (b, N_STATS, n_groups, heads_per_group, s)
    kern = functools.partial(_attn_kernel, s=s, tq=tq)
    col_spec = pl.BlockSpec((1, s, LANES), lambda i, j: (i, 0, j))
    return pl.pallas_call(
        kern,
        grid=(b, n_groups),
        in_specs=[col_spec, col_spec, col_spec, col_spec,
                  pl.BlockSpec((1, N_STATS, 1, heads_per_group, s), lambda i, j: (i, 0, j, 0, 0))],
        out_specs=col_spec,
        out_shape=jax.ShapeDtypeStruct((b, s, d), BF16),
        scratch_shapes=[pltpu.VMEM((heads_per_group, s), F32),
                        pltpu.VMEM((2, heads_per_group, 2, tq, tq), F32),
                        pltpu.SMEM((s // tq,), jnp.int32)],
        compiler_params=pltpu.CompilerParams(
            dimension_semantics=("arbitrary", "arbitrary"), vmem_limit_bytes=VMEM_LIMIT_BYTES),
        name="fox_attention",
    )(q, k, kaug, v, stats)


def kernel(x, g_mix_pre, g_mix_post, g_ffn_pre, g_ffn_post, conv_pw1_w, conv_pw1_b, conv_dw_w, conv_dw_b,
           conv_ln_g, conv_ln_b, conv_pw2_w, conv_pw2_b, attn_w_in, attn_b_f, attn_w_o, mlp_w_up, mlp_w_down):
    b, s, d = x.shape
    depth = g_mix_pre.shape[0]
    for i in range(depth):
        j = i // 2
        attn = None
        if i % 2 == 0:
            x = _conv_mixer(x, g_mix_pre[i], conv_pw1_w[j], conv_pw1_b[j], conv_dw_w[j], conv_dw_b[j],
                            conv_ln_g[j], conv_ln_b[j], conv_pw2_w[j], conv_pw2_b[j], g_mix_post[i])
        else:
            q, k, kaug, v, stats = _qkv(x, g_mix_pre[i], attn_w_in, j, attn_b_f[j])
            attn = (_attention(q, k, kaug, v, stats).reshape(b * s, d), attn_w_o[j], g_mix_post[i])
        x = _mlp(x.reshape(b * s, d), g_ffn_pre[i], mlp_w_up, mlp_w_down, i, g_ffn_post[i], attn).reshape(b, s, d)
    return x
```

```python
import functools

import jax
import jax.numpy as jnp
from jax import lax
from jax.experimental import pallas as pl
from jax.experimental.pallas import tpu as pltpu

N_HEADS = 16
HEAD_DIM = 64
CONV_WIDTH = 31
RMS_EPS = 1e-6
LN_EPS = 1e-5
MASK_VALUE = -1e30

LANES = 128
SUBLANES = 8
CONV_ROWS = 128
CONV_COLS = 256
FF_CHUNK = 1024
CONV_HALO = 32
VMEM_LIMIT_BYTES = 56 * 1024 * 1024
EXP2_UNDERFLOW = 150.5
LOG2E = 1.4426950408889634
NORM_SLACK = 1.01
DIAG_SLACK = 2.0 ** -7
N_STATS = 4

F32 = jnp.float32
BF16 = jnp.bfloat16


def _rms(x, g):
    return x * lax.rsqrt(jnp.mean(x * x, axis=-1, keepdims=True) + RMS_EPS) * g


def _const_spec(shape):
    nd = len(shape)
    return pl.BlockSpec(shape, lambda *_: (0,) * nd, pipeline_mode=pl.Buffered(1))


def _conv_mixer_kernel(x_ref, gpre_ref, w1_ref, b1_ref, dww_ref, dwb_ref, lng_ref, lnb_ref,
                       w2_ref, b2_ref, gpost_ref, o_ref, ubuf_ref, ybuf_ref, *, tm, d):
    x = x_ref[0]
    hb = _rms(x, gpre_ref[...]).astype(BF16)

    @pl.when(pl.program_id(1) == 0)
    def _():
        ubuf_ref[0:CONV_HALO, :] = jnp.zeros((CONV_HALO, d), F32)
        ubuf_ref[CONV_HALO + tm:CONV_HALO + tm + SUBLANES, :] = jnp.zeros((SUBLANES, d), F32)

    def conv_chunk(r0, c0):
        acc = jnp.broadcast_to(dwb_ref[:, c0:c0 + LANES], (CONV_ROWS, LANES))
        for r in range(SUBLANES):
            z = None
            for o in range(r if r >= 2 else r + SUBLANES, CONV_WIDTH + 2, SUBLANES):
                a8 = r0 + o - r
                term = ubuf_ref[a8:a8 + CONV_ROWS + SUBLANES, c0:c0 + LANES] * dww_ref[o - 2:o - 1, c0:c0 + LANES]
                z = term if z is None else z + term
            acc = acc + z[r:r + CONV_ROWS]
        ybuf_ref[r0:r0 + CONV_ROWS, c0:c0 + LANES] = acc

    for g0 in range(0, d, CONV_COLS):
        cols = slice(g0, g0 + CONV_COLS)
        gate_cols = slice(d + g0, d + g0 + CONV_COLS)
        a_u = jnp.dot(hb, w1_ref[:, cols], preferred_element_type=F32) + b1_ref[:, cols]
        a_g = jnp.dot(hb, w1_ref[:, gate_cols], preferred_element_type=F32) + b1_ref[:, gate_cols]
        ubuf_ref[CONV_HALO:CONV_HALO + tm, cols] = a_u * jax.nn.sigmoid(a_g)
        for c0 in range(g0, g0 + CONV_COLS, LANES):
            for r0 in range(0, tm, CONV_ROWS):
                conv_chunk(r0, c0)
        ubuf_ref[0:CONV_HALO, cols] = ubuf_ref[tm:tm + CONV_HALO, cols]

    y = ybuf_ref[...]
    mu = jnp.mean(y, axis=-1, keepdims=True)
    yc = y - mu
    var = jnp.mean(yc * yc, axis=-1, keepdims=True)
    yn = yc * lax.rsqrt(var + LN_EPS) * lng_ref[...] + lnb_ref[...]
    act = yn * jax.nn.sigmoid(yn)
    m = jnp.dot(act.astype(BF16), w2_ref[...], preferred_element_type=F32) + b2_ref[...]
    o_ref[0] = x + _rms(m, gpost_ref[...])


def _conv_mixer(x, gpre, w1, b1, dww, dwb, lng, lnb, w2, b2, gpost, *, tm=512):
    b, s, d = x.shape
    row = lambda v: v.reshape(1, -1).astype(F32)
    kern = functools.partial(_conv_mixer_kernel, tm=tm, d=d)
    return pl.pallas_call(
        kern,
        grid=(b, s // tm),
        in_specs=[
            pl.BlockSpec((1, tm, d), lambda i, j: (i, j, 0)),
            _const_spec((1, d)), _const_spec((d, 2 * d)), _const_spec((1, 2 * d)),
            _const_spec((CONV_WIDTH, d)), _const_spec((1, d)), _const_spec((1, d)), _const_spec((1, d)),
            _const_spec((d, d)), _const_spec((1, d)), _const_spec((1, d)),
        ],
        out_specs=pl.BlockSpec((1, tm, d), lambda i, j: (i, j, 0)),
        out_shape=jax.ShapeDtypeStruct(x.shape, F32),
        scratch_shapes=[pltpu.VMEM((CONV_HALO + tm + SUBLANES, d), F32), pltpu.VMEM((tm, d), F32)],
        compiler_params=pltpu.CompilerParams(
            dimension_semantics=("arbitrary", "arbitrary"), vmem_limit_bytes=VMEM_LIMIT_BYTES),
        name="conv_mixer",
    )(x, row(gpre), w1.astype(BF16), row(b1), dww.astype(F32), row(dwb), row(lng), row(lnb),
      w2.astype(BF16), row(b2), row(gpost))


def _mlp_body(x, gpre_ref, wup_ref, wdown_ref, gpost_ref, o_ref):
    hb = _rms(x, gpre_ref[...]).astype(BF16)
    down = None
    for c0 in range(0, wup_ref.shape[1], FF_CHUNK):
        up = jnp.dot(hb, wup_ref[:, c0:c0 + FF_CHUNK].astype(BF16), preferred_element_type=F32)
        r = jnp.maximum(up, 0.0)
        part = jnp.dot((r * r).astype(BF16), wdown_ref[c0:c0 + FF_CHUNK, :].astype(BF16), preferred_element_type=F32)
        down = part if down is None else down + part
    o_ref[...] = x + _rms(down, gpost_ref[...])


def _mlp_kernel(x_ref, gpre_ref, wup_ref, wdown_ref, gpost_ref, o_ref):
    _mlp_body(x_ref[...], gpre_ref, wup_ref, wdown_ref, gpost_ref, o_ref)


def _attn_out_mlp_kernel(x_ref, a_ref, wo_ref, gmix_ref, gpre_ref, wup_ref, wdown_ref, gpost_ref, o_ref):
    m = jnp.dot(a_ref[...], wo_ref[...], preferred_element_type=F32)
    _mlp_body(x_ref[...] + _rms(m, gmix_ref[...]), gpre_ref, wup_ref, wdown_ref, gpost_ref, o_ref)


def _mlp(x2d, gpre, wup_all, wdown_all, layer, gpost, attn=None, *, tm=512):
    t, d = x2d.shape
    row = lambda v: v.reshape(1, -1).astype(F32)
    act_spec = pl.BlockSpec((tm, d), lambda i: (i, 0))
    mlp_specs = [_const_spec((1, d)), _layer_spec(wup_all, layer), _layer_spec(wdown_all, layer), _const_spec((1, d))]
    mlp_args = (row(gpre), wup_all.astype(F32), wdown_all.astype(F32), row(gpost))
    if attn is None:
        kern, in_specs, args = _mlp_kernel, [act_spec] + mlp_specs, (x2d,) + mlp_args
    else:
        a2d, wo, gmix = attn
        kern = _attn_out_mlp_kernel
        in_specs = [act_spec, act_spec, _const_spec((d, d)), _const_spec((1, d))] + mlp_specs
        args = (x2d, a2d, wo.astype(BF16), row(gmix)) + mlp_args
    return pl.pallas_call(
        kern,
        grid=(t // tm,),
        in_specs=in_specs,
        out_specs=act_spec,
        out_shape=jax.ShapeDtypeStruct(x2d.shape, F32),
        compiler_params=pltpu.CompilerParams(
            dimension_semantics=("arbitrary",), vmem_limit_bytes=VMEM_LIMIT_BYTES),
        name="sqrelu_mlp" if attn is None else "attn_out_mlp",
    )(*args)


def _split3_bf16(x):
    hi = x.astype(BF16)
    r1 = x - hi.astype(F32)
    mid = r1.astype(BF16)
    lo = (r1 - mid.astype(F32)).astype(BF16)
    return hi, mid, lo


def _head_sq_norm_bound(xb, ind):
    xf = xb.astype(F32)
    nsq = jnp.dot((xf * xf).astype(BF16), ind, preferred_element_type=F32)
    return nsq * NORM_SLACK


def _qkv_kernel(x_ref, gpre_ref, win_ref, wf_ref, bf_ref, ind_ref, sel_ref, q_ref, k_ref, ka_ref, v_ref, st_ref,
                carry_ref, *, tm, d):
    x = x_ref[0]
    hb = _rms(x, gpre_ref[...]).astype(BF16)
    proj = jnp.dot(hb, win_ref[:, :3 * d].astype(BF16), preferred_element_type=F32)
    qb = (proj[:, :d] * (HEAD_DIM ** -0.5 * LOG2E)).astype(BF16)
    kb = proj[:, d:2 * d].astype(BF16)
    q_ref[0] = qb
    k_ref[0] = kb
    v_ref[0] = proj[:, 2 * d:].astype(BF16)

    def put_stat(t, per_head):
        rows = per_head.T
        for g in range(N_HEADS // 2):
            st_ref[0, t, g] = rows[2 * g:2 * g + 2, :]

    put_stat(1, _head_sq_norm_bound(qb, ind_ref[...]))
    put_stat(2, _head_sq_norm_bound(kb, ind_ref[...]))
    qk = (qb.astype(F32) * kb.astype(F32)).astype(BF16)
    put_stat(3, jnp.dot(qk, ind_ref[...], preferred_element_type=F32))

    f_logit = jnp.dot(hb, wf_ref[...], preferred_element_type=F32) + bf_ref[...]
    log_f = jax.nn.log_sigmoid(f_logit)

    @pl.when(pl.program_id(1) == 0)
    def _():
        carry_ref[...] = jnp.zeros_like(carry_ref)

    rows = lax.broadcasted_iota(jnp.int32, (tm, tm), 0)
    cols = lax.broadcasted_iota(jnp.int32, (tm, tm), 1)
    tri = (rows >= cols).astype(BF16)
    lane = lax.broadcasted_iota(jnp.int32, (1, LANES), 1)

    def pack3(v):
        v_hi, v_mid, v_lo = (t.astype(F32) for t in _split3_bf16(v))
        return jnp.where(lane < N_HEADS, v_hi, jnp.where(
            lane < 2 * N_HEADS, pltpu.roll(v_mid, N_HEADS, 1), pltpu.roll(v_lo, 2 * N_HEADS, 1))).astype(BF16)

    c3 = jnp.dot(tri, pack3(log_f), preferred_element_type=F32)
    csum = c3 + pltpu.roll(c3, LANES - N_HEADS, 1) + pltpu.roll(c3, LANES - 2 * N_HEADS, 1)
    f_cum = csum + carry_ref[...]
    carry_ref[...] = f_cum[tm - 1:tm, :]
    f2 = f_cum * LOG2E
    put_stat(0, f2)
    ka_ref[0] = jnp.dot(pack3(f2), sel_ref[...], preferred_element_type=F32).astype(BF16)


def _layer_spec(stacked, layer):
    return pl.BlockSpec((None,) + stacked.shape[1:], lambda *_: (layer, 0, 0), pipeline_mode=pl.Buffered(1))


def _qkv(x, gpre, w_in_all, layer, b_f, *, tm=512):
    b, s, d = x.shape
    w_in = w_in_all[layer]
    wf = jnp.zeros((d, LANES), F32).at[:, :N_HEADS].set(w_in[:, 3 * d:]).astype(BF16)
    bf = jnp.zeros((1, LANES), F32).at[0, :N_HEADS].set(b_f.astype(F32))
    ind = (jnp.arange(d)[:, None] // HEAD_DIM == jnp.arange(LANES)[None, :]).astype(BF16)
    rows = jnp.arange(LANES)[:, None]
    head, term = rows % N_HEADS, rows // N_HEADS
    dest = (head // 2) * LANES + (1 - head % 2) * HEAD_DIM
    sel = -((jnp.arange(d)[None, :] == dest + term) & (term < 3)).astype(BF16)
    kern = functools.partial(_qkv_kernel, tm=tm, d=d)
    act_spec = pl.BlockSpec((1, tm, d), lambda i, j: (i, j, 0))
    return pl.pallas_call(
        kern,
        grid=(b, s // tm),
        in_specs=[act_spec, _const_spec((1, d)), _layer_spec(w_in_all, layer), _const_spec((d, LANES)),
                  _const_spec((1, LANES)), _const_spec((d, LANES)), _const_spec((LANES, d))],
        out_specs=[act_spec, act_spec, act_spec, act_spec,
                   pl.BlockSpec((1, N_STATS, N_HEADS // 2, 2, tm), lambda i, j: (i, 0, 0, 0, j))],
        out_shape=[jax.ShapeDtypeStruct((b, s, d), BF16)] * 4
        + [jax.ShapeDtypeStruct((b, N_STATS, N_HEADS // 2, 2, s), F32)],
        scratch_shapes=[pltpu.VMEM((1, LANES), F32)],
        compiler_params=pltpu.CompilerParams(
            dimension_semantics=("arbitrary", "arbitrary"), vmem_limit_bytes=VMEM_LIMIT_BYTES),
        name="attn_qkv",
    )(x, gpre.reshape(1, -1).astype(F32), w_in_all.astype(F32), wf, bf, ind, sel)


def _attn_kernel(q_ref, k_ref, ka_ref, v_ref, st_ref, o_ref, g_ref, s_ref, nback_ref, *, s, tq):
    tk = tq
    lane = lax.broadcasted_iota(jnp.int32, (1, LANES), 1)
    head_lanes = [lane < HEAD_DIM, lane >= HEAD_DIM]
    bias_ones = [((lane >= HEAD_DIM) & (lane < HEAD_DIM + 3)).astype(BF16), (lane < 3).astype(BF16)]
    row_ids = lax.broadcasted_iota(jnp.int32, (tq, tk), 0)
    col_ids = lax.broadcasted_iota(jnp.int32, (tq, tk), 1)
    causal = col_ids <= row_ids
    pos = lax.broadcasted_iota(jnp.int32, (1, s), 1)

    nq, nk = st_ref[0, 1, 0], st_ref[0, 2, 0]
    kmax = jnp.max(nk, axis=1, keepdims=True)
    diag_lb = st_ref[0, 3, 0] - DIAG_SLACK * jnp.sqrt(nq * nk)
    g_ref[...] = jnp.sqrt(nq) * jnp.sqrt(kmax) - diag_lb + st_ref[0, 0, 0]

    def prev_block(q0):
        return q0 - tk if isinstance(q0, int) else pl.multiple_of(q0 - tk, tk)

    def load_q(q0):
        qp = q_ref[0, pl.ds(q0, tq), :]
        return [jnp.where(head_lanes[hh], qp, bias_ones[hh]) for hh in range(2)]

    def scores(qh, k0, masked):
        kp = k_ref[0, pl.ds(k0, tk), :]
        ap = ka_ref[0, pl.ds(k0, tk), :]
        out = []
        for hh in range(2):
            kh = jnp.where(head_lanes[hh], kp, ap)
            sc = lax.dot_general(qh[hh], kh, (((1,), (1,)), ((), ())), preferred_element_type=F32)
            out.append(jnp.where(causal, sc, MASK_VALUE) if masked else sc)
        return out

    def softmax_pv(scs, k0s, carry):
        vps = [v_ref[0, pl.ds(k0, tk), :] for k0 in k0s]
        new = []
        for hh in range(2):
            m, acc = carry[hh]
            m_new = m
            for sc in scs[hh]:
                m_new = jnp.maximum(m_new, jnp.max(sc, axis=1, keepdims=True))
            acc = jnp.exp2(m - m_new) * acc
            for sc, vp in zip(scs[hh], vps):
                p = jnp.exp2(sc - m_new).astype(BF16)
                vh = jnp.where(head_lanes[hh], vp, jnp.ones_like(vp))
                acc = acc + jnp.dot(p, vh, preferred_element_type=F32)
            new.append((m_new, acc))
        return tuple(new)

    def init_carry():
        return tuple((jnp.full((tq, 1), MASK_VALUE, F32), jnp.zeros((tq, LANES), F32)) for _ in range(2))

    for qi in range(1, s // tq):
        gmax = jnp.max(g_ref[:, qi * tq:(qi + 1) * tq], axis=1, keepdims=True)
        live = (gmax - st_ref[0, 0, 0] >= -EXP2_UNDERFLOW) & (pos < qi * tq)
        n_live = jnp.max(jnp.sum(live.astype(F32), axis=1, keepdims=True), axis=0, keepdims=True)
        nback_ref[qi] = jnp.ceil(n_live * (1.0 / tk)).astype(jnp.int32)[0, 0]

    def far_blocks(qi, qh):
        def step(kj, carry):
            k0 = pl.multiple_of(kj * tk, tk)
            return softmax_pv([[sc] for sc in scores(qh, k0, False)], [k0], carry)

        return lax.fori_loop(qi - nback_ref[qi], qi - 1, step, init_carry())

    def tail_scores(slot, q0, qh):
        for bi, (k0, masked) in enumerate([(prev_block(q0), False), (q0, True)]):
            for hh, sc in enumerate(scores(qh, k0, masked)):
                s_ref[slot, hh, bi] = sc

    def tail_finish(slot, q0, carry):
        scs = [[s_ref[slot, hh, 0], s_ref[slot, hh, 1]] for hh in range(2)]
        finish(q0, softmax_pv(scs, [prev_block(q0), q0], carry))

    def finish(q0, carry):
        acc0, acc1 = carry[0][1], carry[1][1]
        out = jnp.where(head_lanes[0], acc0 / pltpu.roll(acc0, HEAD_DIM, 1), acc1 / pltpu.roll(acc1, HEAD_DIM, 1))
        o_ref[0, pl.ds(q0, tq), :] = out.astype(o_ref.dtype)

    finish(0, softmax_pv([[sc] for sc in scores(load_q(0), 0, True)], [0], init_carry()))

    n_q = s // tq
    last = n_q - 1
    tail_scores(0, tq, load_q(tq))

    def pair(i, _):
        qa = 2 * i + 1
        qb = qa + 1
        qn = jnp.minimum(qa + 2, last)
        a0, b0, n0 = (pl.multiple_of(x * tq, tq) for x in (qa, qb, qn))
        qha, qhb, qhn = load_q(a0), load_q(b0), load_q(n0)
        carry_a = far_blocks(qa, qha)
        carry_b = far_blocks(qb, qhb)
        tail_scores(1, b0, qhb)
        tail_finish(0, a0, carry_a)
        tail_scores(0, n0, qhn)
        tail_finish(1, b0, carry_b)
        return 0

    lax.fori_loop(0, last // 2, pair, 0)
    if last % 2 == 1:
        q0 = last * tq
        tail_finish(0, q0, far_blocks(last, load_q(q0)))


def _attention(q, k, kaug, v, stats, *, tq=512):
    b, s, d = q.shape
    n_groups = d // LANES
    heads_per_group = LANES // HEAD_DIM
    assert heads_per_group == 2 and stats.shape == (b, N_STATS, n_groups, heads_per_group, s)
    kern = functools.partial(_attn_kernel, s=s, tq=tq)
    col_spec = pl.BlockSpec((1, s, LANES), lambda i, j: (i, 0, j))
    return pl.pallas_call(
        kern,
        grid=(b, n_groups),
        in_specs=[col_spec, col_spec, col_spec, col_spec,
                  pl.BlockSpec((1, N_STATS, 1, heads_per_group, s), lambda i, j: (i, 0, j, 0, 0))],
        out_specs=col_spec,
        out_shape=jax.ShapeDtypeStruct((b, s, d), BF16),
        scratch_shapes=[pltpu.VMEM((heads_per_group, s), F32),
                        pltpu.VMEM((2, heads_per_group, 2, tq, tq), F32),
                        pltpu.SMEM((s // tq,), jnp.int32)],
        compiler_params=pltpu.CompilerParams(
            dimension_semantics=("arbitrary", "arbitrary"), vmem_limit_bytes=VMEM_LIMIT_BYTES),
        name="fox_attention",
    )(q, k, kaug, v, stats)


def kernel(x, g_mix_pre, g_mix_post, g_ffn_pre, g_ffn_post, conv_pw1_w, conv_pw1_b, conv_dw_w, conv_dw_b,
           conv_ln_g, conv_ln_b, conv_pw2_w, conv_pw2_b, attn_w_in, attn_b_f, attn_w_o, mlp_w_up, mlp_w_down):
    b, s, d = x.shape
    depth = g_mix_pre.shape[0]
    for i in range(depth):
        j = i // 2
        attn = None
        if i % 2 == 0:
            x = _conv_mixer(x, g_mix_pre[i], conv_pw1_w[j], conv_pw1_b[j], conv_dw_w[j], conv_dw_b[j],
                            conv_ln_g[j], conv_ln_b[j], conv_pw2_w[j], conv_pw2_b[j], g_mix_post[i])
        else:
            q, k, kaug, v, stats = _qkv(x, g_mix_pre[i], attn_w_in, j, attn_b_f[j])
            attn = (_attention(q, k, kaug, v, stats).reshape(b * s, d), attn_w_o[j], g_mix_post[i])
        x = _mlp(x.reshape(b * s, d), g_ffn_pre[i], mlp_w_up, mlp_w_down, i, g_ffn_post[i], attn).reshape(b, s, d)
    return x
```

```python
import functools

import jax
import jax.numpy as jnp
from jax import lax
from jax.experimental import pallas as pl
from jax.experimental.pallas import tpu as pltpu

N_HEADS = 16
HEAD_DIM = 64
CONV_WIDTH = 31
RMS_EPS = 1e-6
LN_EPS = 1e-5
MASK_VALUE = -1e30

LANES = 128
SUBLANES = 8
CONV_ROWS = 128
CONV_COLS = 256
FF_CHUNK = 1024
CONV_HALO = 32
VMEM_LIMIT_BYTES = 56 * 1024 * 1024
EXP2_UNDERFLOW = 152.0
LOG2E = 1.4426950408889634
NORM_SLACK = 1.01
DIAG_SLACK = 2.0 ** -7
N_STATS = 4

F32 = jnp.float32
BF16 = jnp.bfloat16


def _rms(x, g):
    return x * lax.rsqrt(jnp.mean(x * x, axis=-1, keepdims=True) + RMS_EPS) * g


def _const_spec(shape):
    nd = len(shape)
    return pl.BlockSpec(shape, lambda *_: (0,) * nd, pipeline_mode=pl.Buffered(1))


def _conv_mixer_kernel(x_ref, gpre_ref, w1_ref, b1_ref, dww_ref, dwb_ref, lng_ref, lnb_ref,
                       w2_ref, b2_ref, gpost_ref, o_ref, ubuf_ref, ybuf_ref, *, tm, d):
    x = x_ref[0]
    hb = _rms(x, gpre_ref[...]).astype(BF16)

    @pl.when(pl.program_id(1) == 0)
    def _():
        ubuf_ref[0:CONV_HALO, :] = jnp.zeros((CONV_HALO, d), F32)
        ubuf_ref[CONV_HALO + tm:CONV_HALO + tm + SUBLANES, :] = jnp.zeros((SUBLANES, d), F32)

    def conv_chunk(r0, c0):
        acc = jnp.broadcast_to(dwb_ref[:, c0:c0 + LANES], (CONV_ROWS, LANES))
        for r in range(SUBLANES):
            z = None
            for o in range(r if r >= 2 else r + SUBLANES, CONV_WIDTH + 2, SUBLANES):
                a8 = r0 + o - r
                term = ubuf_ref[a8:a8 + CONV_ROWS + SUBLANES, c0:c0 + LANES] * dww_ref[o - 2:o - 1, c0:c0 + LANES]
                z = term if z is None else z + term
            acc = acc + z[r:r + CONV_ROWS]
        ybuf_ref[r0:r0 + CONV_ROWS, c0:c0 + LANES] = acc

    for g0 in range(0, d, CONV_COLS):
        cols = slice(g0, g0 + CONV_COLS)
        gate_cols = slice(d + g0, d + g0 + CONV_COLS)
        a_u = jnp.dot(hb, w1_ref[:, cols], preferred_element_type=F32) + b1_ref[:, cols]
        a_g = jnp.dot(hb, w1_ref[:, gate_cols], preferred_element_type=F32) + b1_ref[:, gate_cols]
        ubuf_ref[CONV_HALO:CONV_HALO + tm, cols] = a_u * jax.nn.sigmoid(a_g)
        for c0 in range(g0, g0 + CONV_COLS, LANES):
            for r0 in range(0, tm, CONV_ROWS):
                conv_chunk(r0, c0)
        ubuf_ref[0:CONV_HALO, cols] = ubuf_ref[tm:tm + CONV_HALO, cols]

    y = ybuf_ref[...]
    mu = jnp.mean(y, axis=-1, keepdims=True)
    yc = y - mu
    var = jnp.mean(yc * yc, axis=-1, keepdims=True)
    yn = yc * lax.rsqrt(var + LN_EPS) * lng_ref[...] + lnb_ref[...]
    act = yn * jax.nn.sigmoid(yn)
    m = jnp.dot(act.astype(BF16), w2_ref[...], preferred_element_type=F32) + b2_ref[...]
    o_ref[0] = x + _rms(m, gpost_ref[...])


def _conv_mixer(x, gpre, w1, b1, dww, dwb, lng, lnb, w2, b2, gpost, *, tm=512):
    b, s, d = x.shape
    row = lambda v: v.reshape(1, -1).astype(F32)
    kern = functools.partial(_conv_mixer_kernel, tm=tm, d=d)
    return pl.pallas_call(
        kern,
        grid=(b, s // tm),
        in_specs=[
            pl.BlockSpec((1, tm, d), lambda i, j: (i, j, 0)),
            _const_spec((1, d)), _const_spec((d, 2 * d)), _const_spec((1, 2 * d)),
            _const_spec((CONV_WIDTH, d)), _const_spec((1, d)), _const_spec((1, d)), _const_spec((1, d)),
            _const_spec((d, d)), _const_spec((1, d)), _const_spec((1, d)),
        ],
        out_specs=pl.BlockSpec((1, tm, d), lambda i, j: (i, j, 0)),
        out_shape=jax.ShapeDtypeStruct(x.shape, F32),
        scratch_shapes=[pltpu.VMEM((CONV_HALO + tm + SUBLANES, d), F32), pltpu.VMEM((tm, d), F32)],
        compiler_params=pltpu.CompilerParams(
            dimension_semantics=("arbitrary", "arbitrary"), vmem_limit_bytes=VMEM_LIMIT_BYTES),
        name="conv_mixer",
    )(x, row(gpre), w1.astype(BF16), row(b1), dww.astype(F32), row(dwb), row(lng), row(lnb),
      w2.astype(BF16), row(b2), row(gpost))


def _mlp_body(x, gpre_ref, wup_ref, wdown_ref, gpost_ref, o_ref):
    hb = _rms(x, gpre_ref[...]).astype(BF16)
    down = None
    for c0 in range(0, wup_ref.shape[1], FF_CHUNK):
        up = jnp.dot(hb, wup_ref[:, c0:c0 + FF_CHUNK].astype(BF16), preferred_element_type=F32)
        r = jnp.maximum(up, 0.0)
        part = jnp.dot((r * r).astype(BF16), wdown_ref[c0:c0 + FF_CHUNK, :].astype(BF16), preferred_element_type=F32)
        down = part if down is None else down + part
    o_ref[...] = x + _rms(down, gpost_ref[...])


def _mlp_kernel(x_ref, gpre_ref, wup_ref, wdown_ref, gpost_ref, o_ref):
    _mlp_body(x_ref[...], gpre_ref, wup_ref, wdown_ref, gpost_ref, o_ref)


def _attn_out_mlp_kernel(x_ref, a_ref, wo_ref, gmix_ref, gpre_ref, wup_ref, wdown_ref, gpost_ref, o_ref):
    m = jnp.dot(a_ref[...], wo_ref[...], preferred_element_type=F32)
    _mlp_body(x_ref[...] + _rms(m, gmix_ref[...]), gpre_ref, wup_ref, wdown_ref, gpost_ref, o_ref)


def _mlp(x2d, gpre, wup_all, wdown_all, layer, gpost, attn=None, *, tm=512):
    t, d = x2d.shape
    row = lambda v: v.reshape(1, -1).astype(F32)
    act_spec = pl.BlockSpec((tm, d), lambda i: (i, 0))
    mlp_specs = [_const_spec((1, d)), _layer_spec(wup_all, layer), _layer_spec(wdown_all, layer), _const_spec((1, d))]
    mlp_args = (row(gpre), wup_all.astype(F32), wdown_all.astype(F32), row(gpost))
    if attn is None:
        kern, in_specs, args = _mlp_kernel, [act_spec] + mlp_specs, (x2d,) + mlp_args
    else:
        a2d, wo, gmix = attn
        kern = _attn_out_mlp_kernel
        in_specs = [act_spec, act_spec, _const_spec((d, d)), _const_spec((1, d))] + mlp_specs
        args = (x2d, a2d, wo.astype(BF16), row(gmix)) + mlp_args
    return pl.pallas_call(
        kern,
        grid=(t // tm,),
        in_specs=in_specs,
        out_specs=act_spec,
        out_shape=jax.ShapeDtypeStruct(x2d.shape, F32),
        compiler_params=pltpu.CompilerParams(
            dimension_semantics=("arbitrary",), vmem_limit_bytes=VMEM_LIMIT_BYTES),
        name="sqrelu_mlp" if attn is None else "attn_out_mlp",
    )(*args)


def _split3_bf16(x):
    hi = x.astype(BF16)
    r1 = x - hi.astype(F32)
    mid = r1.astype(BF16)
    lo = (r1 - mid.astype(F32)).astype(BF16)
    return hi, mid, lo


def _head_sq_norm_bound(xb, ind):
    xf = xb.astype(F32)
    nsq = jnp.dot((xf * xf).astype(BF16), ind, preferred_element_type=F32)
    return nsq * NORM_SLACK


def _qkv_kernel(x_ref, gpre_ref, win_ref, wf_ref, bf_ref, ind_ref, sel_ref, q_ref, k_ref, ka_ref, v_ref, st_ref,
                carry_ref, *, tm, d):
    x = x_ref[0]
    hb = _rms(x, gpre_ref[...]).astype(BF16)
    proj = jnp.dot(hb, win_ref[:, :3 * d].astype(BF16), preferred_element_type=F32)
    qb = (proj[:, :d] * (HEAD_DIM ** -0.5 * LOG2E)).astype(BF16)
    kb = proj[:, d:2 * d].astype(BF16)
    q_ref[0] = qb
    k_ref[0] = kb
    v_ref[0] = proj[:, 2 * d:].astype(BF16)

    def put_stat(t, per_head):
        rows = per_head.T
        for g in range(N_HEADS // 2):
            st_ref[0, t, g] = rows[2 * g:2 * g + 2, :]

    put_stat(1, _head_sq_norm_bound(qb, ind_ref[...]))
    put_stat(2, _head_sq_norm_bound(kb, ind_ref[...]))
    qk = (qb.astype(F32) * kb.astype(F32)).astype(BF16)
    put_stat(3, jnp.dot(qk, ind_ref[...], preferred_element_type=F32))

    f_logit = jnp.dot(hb, wf_ref[...], preferred_element_type=F32) + bf_ref[...]
    log_f = jax.nn.log_sigmoid(f_logit)

    @pl.when(pl.program_id(1) == 0)
    def _():
        carry_ref[...] = jnp.zeros_like(carry_ref)

    rows = lax.broadcasted_iota(jnp.int32, (tm, tm), 0)
    cols = lax.broadcasted_iota(jnp.int32, (tm, tm), 1)
    tri = (rows >= cols).astype(BF16)
    lane = lax.broadcasted_iota(jnp.int32, (1, LANES), 1)

    def pack3(v):
        v_hi, v_mid, v_lo = (t.astype(F32) for t in _split3_bf16(v))
        return jnp.where(lane < N_HEADS, v_hi, jnp.where(
            lane < 2 * N_HEADS, pltpu.roll(v_mid, N_HEADS, 1), pltpu.roll(v_lo, 2 * N_HEADS, 1))).astype(BF16)

    c3 = jnp.dot(tri, pack3(log_f), preferred_element_type=F32)
    csum = c3 + pltpu.roll(c3, LANES - N_HEADS, 1) + pltpu.roll(c3, LANES - 2 * N_HEADS, 1)
    f_cum = csum + carry_ref[...]
    carry_ref[...] = f_cum[tm - 1:tm, :]
    f2 = f_cum * LOG2E
    put_stat(0, f2)
    ka_ref[0] = jnp.dot(pack3(f2), sel_ref[...], preferred_element_type=F32).astype(BF16)


def _layer_spec(stacked, layer):
    return pl.BlockSpec((None,) + stacked.shape[1:], lambda *_: (layer, 0, 0), pipeline_mode=pl.Buffered(1))


def _qkv(x, gpre, w_in_all, layer, b_f, *, tm=512):
    b, s, d = x.shape
    w_in = w_in_all[layer]
    wf = jnp.zeros((d, LANES), F32).at[:, :N_HEADS].set(w_in[:, 3 * d:]).astype(BF16)
    bf = jnp.zeros((1, LANES), F32).at[0, :N_HEADS].set(b_f.astype(F32))
    ind = (jnp.arange(d)[:, None] // HEAD_DIM == jnp.arange(LANES)[None, :]).astype(BF16)
    rows = jnp.arange(LANES)[:, None]
    head, term = rows % N_HEADS, rows // N_HEADS
    dest = (head // 2) * LANES + (1 - head % 2) * HEAD_DIM
    sel = -((jnp.arange(d)[None, :] == dest + term) & (term < 3)).astype(BF16)
    kern = functools.partial(_qkv_kernel, tm=tm, d=d)
    act_spec = pl.BlockSpec((1, tm, d), lambda i, j: (i, j, 0))
    return pl.pallas_call(
        kern,
        grid=(b, s // tm),
        in_specs=[act_spec, _const_spec((1, d)), _layer_spec(w_in_all, layer), _const_spec((d, LANES)),
                  _const_spec((1, LANES)), _const_spec((d, LANES)), _const_spec((LANES, d))],
        out_specs=[act_spec, act_spec, act_spec, act_spec,
                   pl.BlockSpec((1, N_STATS, N_HEADS // 2, 2, tm), lambda i, j: (i, 0, 0, 0, j))],
        out_shape=[jax.ShapeDtypeStruct((b, s, d), BF16)] * 4
        + [jax.ShapeDtypeStruct((b, N_STATS, N_HEADS // 2, 2, s), F32)],
        scratch_shapes=[pltpu.VMEM((1, LANES), F32)],
        compiler_params=pltpu.CompilerParams(
            dimension_semantics=("arbitrary", "arbitrary"), vmem_limit_bytes=VMEM_LIMIT_BYTES),
        name="attn_qkv",
    )(x, gpre.reshape(1, -1).astype(F32), w_in_all.astype(F32), wf, bf, ind, sel)


def _attn_kernel(q_ref, k_ref, ka_ref, v_ref, st_ref, o_ref, g_ref, s_ref, nback_ref, *, s, tq):
    tk = tq
    lane = lax.broadcasted_iota(jnp.int32, (1, LANES), 1)
    head_lanes = [lane < HEAD_DIM, lane >= HEAD_DIM]
    bias_ones = [((lane >= HEAD_DIM) & (lane < HEAD_DIM + 3)).astype(BF16), (lane < 3).astype(BF16)]
    row_ids = lax.broadcasted_iota(jnp.int32, (tq, tk), 0)
    col_ids = lax.broadcasted_iota(jnp.int32, (tq, tk), 1)
    causal = col_ids <= row_ids
    pos = lax.broadcasted_iota(jnp.int32, (1, s), 1)

    nq, nk = st_ref[0, 1, 0], st_ref[0, 2, 0]
    kmax = jnp.max(nk, axis=1, keepdims=True)
    diag_lb = st_ref[0, 3, 0] - DIAG_SLACK * jnp.sqrt(nq * nk)
    g_ref[...] = jnp.sqrt(nq) * jnp.sqrt(kmax) - diag_lb + st_ref[0, 0, 0]

    def prev_block(q0):
        return q0 - tk if isinstance(q0, int) else pl.multiple_of(q0 - tk, tk)

    def load_q(q0):
        qp = q_ref[0, pl.ds(q0, tq), :]
        return [jnp.where(head_lanes[hh], qp, bias_ones[hh]) for hh in range(2)]

    def scores(qh, k0, masked):
        kp = k_ref[0, pl.ds(k0, tk), :]
        ap = ka_ref[0, pl.ds(k0, tk), :]
        out = []
        for hh in range(2):
            kh = jnp.where(head_lanes[hh], kp, ap)
            sc = lax.dot_general(qh[hh], kh, (((1,), (1,)), ((), ())), preferred_element_type=F32)
            out.append(jnp.where(causal, sc, MASK_VALUE) if masked else sc)
        return out

    def softmax_pv(scs, k0s, carry):
        vps = [v_ref[0, pl.ds(k0, tk), :] for k0 in k0s]
        new = []
        for hh in range(2):
            m, acc = carry[hh]
            m_new = m
            for sc in scs[hh]:
                m_new = jnp.maximum(m_new, jnp.max(sc, axis=1, keepdims=True))
            acc = jnp.exp2(m - m_new) * acc
            for sc, vp in zip(scs[hh], vps):
                p = jnp.exp2(sc - m_new).astype(BF16)
                vh = jnp.where(head_lanes[hh], vp, jnp.ones_like(vp))
                acc = acc + jnp.dot(p, vh, preferred_element_type=F32)
            new.append((m_new, acc))
        return tuple(new)

    def init_carry():
        return tuple((jnp.full((tq, 1), MASK_VALUE, F32), jnp.zeros((tq, LANES), F32)) for _ in range(2))

    for qi in range(1, s // tq):
        gmax = jnp.max(g_ref[:, qi * tq:(qi + 1) * tq], axis=1, keepdims=True)
        live = (gmax - st_ref[0, 0, 0] >= -EXP2_UNDERFLOW) & (pos < qi * tq)
        n_live = jnp.max(jnp.sum(live.astype(F32), axis=1, keepdims=True), axis=0, keepdims=True)
        nback_ref[qi] = jnp.ceil(n_live * (1.0 / tk)).astype(jnp.int32)[0, 0]

    def far_blocks(qi, qh):
        def step(kj, carry):
            k0 = pl.multiple_of(kj * tk, tk)
            return softmax_pv([[sc] for sc in scores(qh, k0, False)], [k0], carry)

        return lax.fori_loop(qi - nback_ref[qi], qi - 1, step, init_carry())

    half = tq // 2
    causal_top = (lax.broadcasted_iota(jnp.int32, (half, half), 1) <= lax.broadcasted_iota(jnp.int32, (half, half), 0))
    causal_bot = (lax.broadcasted_iota(jnp.int32, (half, tk), 1)
                  <= lax.broadcasted_iota(jnp.int32, (half, tk), 0) + half)

    def tail_scores(slot, q0, qh):
        for hh, sc in enumerate(scores(qh, prev_block(q0), False)):
            s_ref[slot, hh, 0] = sc
        kp = k_ref[0, pl.ds(q0, tk), :]
        ap = ka_ref[0, pl.ds(q0, tk), :]
        for hh in range(2):
            kh = jnp.where(head_lanes[hh], kp, ap)
            top = lax.dot_general(qh[hh][:half], kh[:half], (((1,), (1,)), ((), ())), preferred_element_type=F32)
            bot = lax.dot_general(qh[hh][half:], kh, (((1,), (1,)), ((), ())), preferred_element_type=F32)
            s_ref[slot, hh, 1, :half, :half] = jnp.where(causal_top, top, MASK_VALUE)
            s_ref[slot, hh, 1, :half, half:] = jnp.full((half, tk - half), MASK_VALUE, F32)
            s_ref[slot, hh, 1, half:, :] = jnp.where(causal_bot, bot, MASK_VALUE)

    def tail_finish(slot, q0, carry):
        v_prev = v_ref[0, pl.ds(prev_block(q0), tk), :]
        v_diag = v_ref[0, pl.ds(q0, tk), :]
        new = []
        for hh in range(2):
            m, acc = carry[hh]
            s_prev = s_ref[slot, hh, 0]
            s_diag = s_ref[slot, hh, 1]
            m_new = jnp.maximum(jnp.maximum(m, jnp.max(s_prev, axis=1, keepdims=True)),
                                jnp.max(s_diag, axis=1, keepdims=True))
            vh_prev = jnp.where(head_lanes[hh], v_prev, jnp.ones_like(v_prev))
            vh_diag = jnp.where(head_lanes[hh], v_diag, jnp.ones_like(v_diag))
            acc = jnp.exp2(m - m_new) * acc + jnp.dot(
                jnp.exp2(s_prev - m_new).astype(BF16), vh_prev, preferred_element_type=F32)
            p_diag = jnp.exp2(s_diag - m_new).astype(BF16)
            acc = acc + jnp.concatenate(
                [jnp.dot(p_diag[:half, :half], vh_diag[:half], preferred_element_type=F32),
                 jnp.dot(p_diag[half:], vh_diag, preferred_element_type=F32)], axis=0)
            new.append((m_new, acc))
        finish(q0, tuple(new))

    def finish(q0, carry):
        acc0, acc1 = carry[0][1], carry[1][1]
        out = jnp.where(head_lanes[0], acc0 / pltpu.roll(acc0, HEAD_DIM, 1), acc1 / pltpu.roll(acc1, HEAD_DIM, 1))
        o_ref[0, pl.ds(q0, tq), :] = out.astype(o_ref.dtype)

    finish(0, softmax_pv([[sc] for sc in scores(load_q(0), 0, True)], [0], init_carry()))

    n_q = s // tq
    last = n_q - 1
    tail_scores(0, tq, load_q(tq))

    def pair(i, _):
        qa = 2 * i + 1
        qb = qa + 1
        qn = jnp.minimum(qa + 2, last)
        a0, b0, n0 = (pl.multiple_of(x * tq, tq) for x in (qa, qb, qn))
        qha, qhb, qhn = load_q(a0), load_q(b0), load_q(n0)
        carry_a = far_blocks(qa, qha)
        carry_b = far_blocks(qb, qhb)
        tail_scores(1, b0, qhb)
        tail_finish(0, a0, carry_a)
        tail_scores(0, n0, qhn)
        tail_finish(1, b0, carry_b)
        return 0

    lax.fori_loop(0, last // 2, pair, 0)
    if last % 2 == 1:
        q0 = last * tq
        tail_finish(0, q0, far_blocks(last, load_q(q0)))


def _attention(q, k, kaug, v, stats, *, tq=512):
    b, s, d = q.shape
    n_groups = d // LANES
    heads_per_group = LANES // HEAD_DIM
    assert heads_per_group == 2 and stats.shape == (b, N_STATS, n_groups, heads_per_group, s)
    kern = functools.partial(_attn_kernel, s=s, tq=tq)
    col_spec = pl.BlockSpec((1, s, LANES), lambda i, j: (i, 0, j))
    return pl.pallas_call(
        kern,
        grid=(b, n_groups),
        in_specs=[col_spec, col_spec, col_spec, col_spec,
                  pl.BlockSpec((1, N_STATS, 1, heads_per_group, s), lambda i, j: (i, 0, j, 0, 0))],
        out_specs=col_spec,
        out_shape=jax.ShapeDtypeStruct((b, s, d), BF16),
        scratch_shapes=[pltpu.VMEM((heads_per_group, s), F32),
                        pltpu.VMEM((2, heads_per_group, 2, tq, tq), F32),
                        pltpu.SMEM((s // tq,), jnp.int32)],
        compiler_params=pltpu.CompilerParams(
            dimension_semantics=("arbitrary", "arbitrary"), vmem_limit_bytes=VMEM_LIMIT_BYTES),
        name="fox_attention",
    )(q, k, kaug, v, stats)


def kernel(x, g_mix_pre, g_mix_post, g_ffn_pre, g_ffn_post, conv_pw1_w, conv_pw1_b, conv_dw_w, conv_dw_b,
           conv_ln_g, conv_ln_b, conv_pw2_w, conv_pw2_b, attn_w_in, attn_b_f, attn_w_o, mlp_w_up, mlp_w_down):
    b, s, d = x.shape
    depth = g_mix_pre.shape[0]
    for i in range(depth):
        j = i // 2
        attn = None
        if i % 2 == 0:
            x = _conv_mixer(x, g_mix_pre[i], conv_pw1_w[j], conv_pw1_b[j], conv_dw_w[j], conv_dw_b[j],
                            conv_ln_g[j], conv_ln_b[j], conv_pw2_w[j], conv_pw2_b[j], g_mix_post[i])
        else:
            q, k, kaug, v, stats = _qkv(x, g_mix_pre[i], attn_w_in, j, attn_b_f[j])
            attn = (_attention(q, k, kaug, v, stats).reshape(b * s, d), attn_w_o[j], g_mix_post[i])
        x = _mlp(x.reshape(b * s, d), g_ffn_pre[i], mlp_w_up, mlp_w_down, i, g_ffn_post[i], attn).reshape(b, s, d)
    return x
```

```python
import functools

import jax
import jax.numpy as jnp
from jax import lax
from jax.experimental import pallas as pl
from jax.experimental.pallas import tpu as pltpu

N_HEADS = 16
HEAD_DIM = 64
CONV_WIDTH = 31
RMS_EPS = 1e-6
LN_EPS = 1e-5
MASK_VALUE = -1e30

LANES = 128
SUBLANES = 8
CONV_ROWS = 128
CONV_COLS = 256
FF_CHUNK = 1024
MLP_ROW_GROUPS = 2
CONV_HALO = 32
VMEM_LIMIT_BYTES = 56 * 1024 * 1024
EXP2_UNDERFLOW = 152.0
LOG2E = 1.4426950408889634
NORM_SLACK = 1.01
DIAG_SLACK = 2.0 ** -7
N_STATS = 4

F32 = jnp.float32
BF16 = jnp.bfloat16


def _rms(x, g):
    return x * lax.rsqrt(jnp.mean(x * x, axis=-1, keepdims=True) + RMS_EPS) * g


def _const_spec(shape):
    nd = len(shape)
    return pl.BlockSpec(shape, lambda *_: (0,) * nd, pipeline_mode=pl.Buffered(1))


def _conv_mixer_kernel(x_ref, gpre_ref, w1_ref, b1_ref, dww_ref, dwb_ref, lng_ref, lnb_ref,
                       w2_ref, b2_ref, gpost_ref, o_ref, ubuf_ref, ybuf_ref, *, tm, d):
    x = x_ref[0]
    hb = _rms(x, gpre_ref[...]).astype(BF16)

    @pl.when(pl.program_id(1) == 0)
    def _():
        ubuf_ref[0:CONV_HALO, :] = jnp.zeros((CONV_HALO, d), F32)
        ubuf_ref[CONV_HALO + tm:CONV_HALO + tm + SUBLANES, :] = jnp.zeros((SUBLANES, d), F32)

    def conv_chunk(r0, c0):
        acc = jnp.broadcast_to(dwb_ref[:, c0:c0 + LANES], (CONV_ROWS, LANES))
        for r in range(SUBLANES):
            z = None
            for o in range(r if r >= 2 else r + SUBLANES, CONV_WIDTH + 2, SUBLANES):
                a8 = r0 + o - r
                term = ubuf_ref[a8:a8 + CONV_ROWS + SUBLANES, c0:c0 + LANES] * dww_ref[o - 2:o - 1, c0:c0 + LANES]
                z = term if z is None else z + term
            acc = acc + z[r:r + CONV_ROWS]
        ybuf_ref[r0:r0 + CONV_ROWS, c0:c0 + LANES] = acc

    for g0 in range(0, d, CONV_COLS):
        cols = slice(g0, g0 + CONV_COLS)
        gate_cols = slice(d + g0, d + g0 + CONV_COLS)
        a_u = jnp.dot(hb, w1_ref[:, cols], preferred_element_type=F32) + b1_ref[:, cols]
        a_g = jnp.dot(hb, w1_ref[:, gate_cols], preferred_element_type=F32) + b1_ref[:, gate_cols]
        ubuf_ref[CONV_HALO:CONV_HALO + tm, cols] = a_u * jax.nn.sigmoid(a_g)
        for c0 in range(g0, g0 + CONV_COLS, LANES):
            for r0 in range(0, tm, CONV_ROWS):
                conv_chunk(r0, c0)
        ubuf_ref[0:CONV_HALO, cols] = ubuf_ref[tm:tm + CONV_HALO, cols]

    y = ybuf_ref[...]
    mu = jnp.mean(y, axis=-1, keepdims=True)
    yc = y - mu
    var = jnp.mean(yc * yc, axis=-1, keepdims=True)
    yn = yc * lax.rsqrt(var + LN_EPS) * lng_ref[...] + lnb_ref[...]
    act = yn * jax.nn.sigmoid(yn)
    m = jnp.dot(act.astype(BF16), w2_ref[...], preferred_element_type=F32) + b2_ref[...]
    o_ref[0] = x + _rms(m, gpost_ref[...])


def _conv_mixer(x, gpre, w1, b1, dww, dwb, lng, lnb, w2, b2, gpost, *, tm=512):
    b, s, d = x.shape
    row = lambda v: v.reshape(1, -1).astype(F32)
    kern = functools.partial(_conv_mixer_kernel, tm=tm, d=d)
    return pl.pallas_call(
        kern,
        grid=(b, s // tm),
        in_specs=[
            pl.BlockSpec((1, tm, d), lambda i, j: (i, j, 0)),
            _const_spec((1, d)), _const_spec((d, 2 * d)), _const_spec((1, 2 * d)),
            _const_spec((CONV_WIDTH, d)), _const_spec((1, d)), _const_spec((1, d)), _const_spec((1, d)),
            _const_spec((d, d)), _const_spec((1, d)), _const_spec((1, d)),
        ],
        out_specs=pl.BlockSpec((1, tm, d), lambda i, j: (i, j, 0)),
        out_shape=jax.ShapeDtypeStruct(x.shape, F32),
        scratch_shapes=[pltpu.VMEM((CONV_HALO + tm + SUBLANES, d), F32), pltpu.VMEM((tm, d), F32)],
        compiler_params=pltpu.CompilerParams(
            dimension_semantics=("arbitrary", "arbitrary"), vmem_limit_bytes=VMEM_LIMIT_BYTES),
        name="conv_mixer",
    )(x, row(gpre), w1.astype(BF16), row(b1), dww.astype(F32), row(dwb), row(lng), row(lnb),
      w2.astype(BF16), row(b2), row(gpost))


def _mlp_body(xs, gpre_ref, wup_ref, wdown_ref, gpost_ref, o_ref):
    hbs = [_rms(x, gpre_ref[...]).astype(BF16) for x in xs]
    downs = [None] * len(xs)
    for c0 in range(0, wup_ref.shape[1], FF_CHUNK):
        wu = wup_ref[:, c0:c0 + FF_CHUNK]
        wd = wdown_ref[c0:c0 + FF_CHUNK, :]
        for i, hb in enumerate(hbs):
            r = jnp.maximum(jnp.dot(hb, wu, preferred_element_type=F32), 0.0)
            part = jnp.dot((r * r).astype(BF16), wd, preferred_element_type=F32)
            downs[i] = part if downs[i] is None else downs[i] + part
    row = 0
    for x, down in zip(xs, downs):
        o_ref[row:row + x.shape[0], :] = x + _rms(down, gpost_ref[...])
        row += x.shape[0]


def _row_groups(ref):
    rows = ref.shape[0] // MLP_ROW_GROUPS
    return [ref[i * rows:(i + 1) * rows, :] for i in range(MLP_ROW_GROUPS)]


def _mlp_kernel(x_ref, gpre_ref, wup_ref, wdown_ref, gpost_ref, o_ref):
    _mlp_body(_row_groups(x_ref), gpre_ref, wup_ref, wdown_ref, gpost_ref, o_ref)


def _attn_out_mlp_kernel(x_ref, a_ref, wo_ref, gmix_ref, gpre_ref, wup_ref, wdown_ref, gpost_ref, o_ref):
    xs = [x + _rms(jnp.dot(a, wo_ref[...], preferred_element_type=F32), gmix_ref[...])
          for x, a in zip(_row_groups(x_ref), _row_groups(a_ref))]
    _mlp_body(xs, gpre_ref, wup_ref, wdown_ref, gpost_ref, o_ref)


def _mlp(x2d, gpre, wup_all, wdown_all, layer, gpost, attn=None, *, tm=1024):
    t, d = x2d.shape
    row = lambda v: v.reshape(1, -1).astype(F32)
    act_spec = pl.BlockSpec((tm, d), lambda i: (i, 0))
    mlp_specs = [_const_spec((1, d)), _layer_spec(wup_all, layer), _layer_spec(wdown_all, layer), _const_spec((1, d))]
    mlp_args = (row(gpre), wup_all.astype(BF16), wdown_all.astype(BF16), row(gpost))
    if attn is None:
        kern, in_specs, args = _mlp_kernel, [act_spec] + mlp_specs, (x2d,) + mlp_args
    else:
        a2d, wo, gmix = attn
        kern = _attn_out_mlp_kernel
        in_specs = [act_spec, act_spec, _const_spec((d, d)), _const_spec((1, d))] + mlp_specs
        args = (x2d, a2d, wo.astype(BF16), row(gmix)) + mlp_args
    return pl.pallas_call(
        kern,
        grid=(t // tm,),
        in_specs=in_specs,
        out_specs=act_spec,
        out_shape=jax.ShapeDtypeStruct(x2d.shape, F32),
        compiler_params=pltpu.CompilerParams(
            dimension_semantics=("arbitrary",), vmem_limit_bytes=VMEM_LIMIT_BYTES),
        name="sqrelu_mlp" if attn is None else "attn_out_mlp",
    )(*args)


def _split3_bf16(x):
    hi = x.astype(BF16)
    r1 = x - hi.astype(F32)
    mid = r1.astype(BF16)
    lo = (r1 - mid.astype(F32)).astype(BF16)
    return hi, mid, lo


def _head_sq_norm_bound(xb, ind):
    xf = xb.astype(F32)
    nsq = jnp.dot((xf * xf).astype(BF16), ind, preferred_element_type=F32)
    return nsq * NORM_SLACK


def _qkv_kernel(x_ref, gpre_ref, win_ref, wf_ref, bf_ref, ind_ref, sel_ref, q_ref, k_ref, ka_ref, v_ref, st_ref,
                carry_ref, *, tm, d):
    x = x_ref[0]
    hb = _rms(x, gpre_ref[...]).astype(BF16)
    proj = jnp.dot(hb, win_ref[:, :3 * d].astype(BF16), preferred_element_type=F32)
    qb = (proj[:, :d] * (HEAD_DIM ** -0.5 * LOG2E)).astype(BF16)
    kb = proj[:, d:2 * d].astype(BF16)
    q_ref[0] = qb
    k_ref[0] = kb
    v_ref[0] = proj[:, 2 * d:].astype(BF16)

    def put_stat(t, per_head):
        rows = per_head.T
        for g in range(N_HEADS // 2):
            st_ref[0, t, g] = rows[2 * g:2 * g + 2, :]

    put_stat(1, _head_sq_norm_bound(qb, ind_ref[...]))
    put_stat(2, _head_sq_norm_bound(kb, ind_ref[...]))
    qk = (qb.astype(F32) * kb.astype(F32)).astype(BF16)
    put_stat(3, jnp.dot(qk, ind_ref[...], preferred_element_type=F32))

    f_logit = jnp.dot(hb, wf_ref[...], preferred_element_type=F32) + bf_ref[...]
    log_f = jax.nn.log_sigmoid(f_logit)

    @pl.when(pl.program_id(1) == 0)
    def _():
        carry_ref[...] = jnp.zeros_like(carry_ref)

    rows = lax.broadcasted_iota(jnp.int32, (tm, tm), 0)
    cols = lax.broadcasted_iota(jnp.int32, (tm, tm), 1)
    tri = (rows >= cols).astype(BF16)
    lane = lax.broadcasted_iota(jnp.int32, (1, LANES), 1)

    def pack3(v):
        v_hi, v_mid, v_lo = (t.astype(F32) for t in _split3_bf16(v))
        return jnp.where(lane < N_HEADS, v_hi, jnp.where(
            lane < 2 * N_HEADS, pltpu.roll(v_mid, N_HEADS, 1), pltpu.roll(v_lo, 2 * N_HEADS, 1))).astype(BF16)

    c3 = jnp.dot(tri, pack3(log_f), preferred_element_type=F32)
    csum = c3 + pltpu.roll(c3, LANES - N_HEADS, 1) + pltpu.roll(c3, LANES - 2 * N_HEADS, 1)
    f_cum = csum + carry_ref[...]
    carry_ref[...] = f_cum[tm - 1:tm, :]
    f2 = f_cum * LOG2E
    put_stat(0, f2)
    ka_ref[0] = jnp.dot(pack3(f2), sel_ref[...], preferred_element_type=F32).astype(BF16)


def _layer_spec(stacked, layer):
    return pl.BlockSpec((None,) + stacked.shape[1:], lambda *_: (layer, 0, 0), pipeline_mode=pl.Buffered(1))


def _qkv(x, gpre, w_in_all, layer, b_f, *, tm=512):
    b, s, d = x.shape
    w_in = w_in_all[layer]
    wf = jnp.zeros((d, LANES), F32).at[:, :N_HEADS].set(w_in[:, 3 * d:]).astype(BF16)
    bf = jnp.zeros((1, LANES), F32).at[0, :N_HEADS].set(b_f.astype(F32))
    ind = (jnp.arange(d)[:, None] // HEAD_DIM == jnp.arange(LANES)[None, :]).astype(BF16)
    rows = jnp.arange(LANES)[:, None]
    head, term = rows % N_HEADS, rows // N_HEADS
    dest = (head // 2) * LANES + (1 - head % 2) * HEAD_DIM
    sel = -((jnp.arange(d)[None, :] == dest + term) & (term < 3)).astype(BF16)
    kern = functools.partial(_qkv_kernel, tm=tm, d=d)
    act_spec = pl.BlockSpec((1, tm, d), lambda i, j: (i, j, 0))
    return pl.pallas_call(
        kern,
        grid=(b, s // tm),
        in_specs=[act_spec, _const_spec((1, d)), _layer_spec(w_in_all, layer), _const_spec((d, LANES)),
                  _const_spec((1, LANES)), _const_spec((d, LANES)), _const_spec((LANES, d))],
        out_specs=[act_spec, act_spec, act_spec, act_spec,
                   pl.BlockSpec((1, N_STATS, N_HEADS // 2, 2, tm), lambda i, j: (i, 0, 0, 0, j))],
        out_shape=[jax.ShapeDtypeStruct((b, s, d), BF16)] * 4
        + [jax.ShapeDtypeStruct((b, N_STATS, N_HEADS // 2, 2, s), F32)],
        scratch_shapes=[pltpu.VMEM((1, LANES), F32)],
        compiler_params=pltpu.CompilerParams(
            dimension_semantics=("arbitrary", "arbitrary"), vmem_limit_bytes=VMEM_LIMIT_BYTES),
        name="attn_qkv",
    )(x, gpre.reshape(1, -1).astype(F32), w_in_all.astype(F32), wf, bf, ind, sel)


def _attn_kernel(q_ref, k_ref, ka_ref, v_ref, st_ref, o_ref, g_ref, s_ref, nback_ref, *, s, tq):
    tk = tq
    lane = lax.broadcasted_iota(jnp.int32, (1, LANES), 1)
    head_lanes = [lane < HEAD_DIM, lane >= HEAD_DIM]
    bias_ones = [((lane >= HEAD_DIM) & (lane < HEAD_DIM + 3)).astype(BF16), (lane < 3).astype(BF16)]
    row_ids = lax.broadcasted_iota(jnp.int32, (tq, tk), 0)
    col_ids = lax.broadcasted_iota(jnp.int32, (tq, tk), 1)
    causal = col_ids <= row_ids
    pos = lax.broadcasted_iota(jnp.int32, (1, s), 1)

    nq, nk = st_ref[0, 1, 0], st_ref[0, 2, 0]
    kmax = jnp.max(nk, axis=1, keepdims=True)
    diag_lb = st_ref[0, 3, 0] - DIAG_SLACK * jnp.sqrt(nq * nk)
    g_ref[...] = jnp.sqrt(nq) * jnp.sqrt(kmax) - diag_lb + st_ref[0, 0, 0]

    def prev_block(q0):
        return q0 - tk if isinstance(q0, int) else pl.multiple_of(q0 - tk, tk)

    def load_q(q0):
        qp = q_ref[0, pl.ds(q0, tq), :]
        return [jnp.where(head_lanes[hh], qp, bias_ones[hh]) for hh in range(2)]

    def scores(qh, k0, masked):
        kp = k_ref[0, pl.ds(k0, tk), :]
        ap = ka_ref[0, pl.ds(k0, tk), :]
        out = []
        for hh in range(2):
            kh = jnp.where(head_lanes[hh], kp, ap)
            sc = lax.dot_general(qh[hh], kh, (((1,), (1,)), ((), ())), preferred_element_type=F32)
            out.append(jnp.where(causal, sc, MASK_VALUE) if masked else sc)
        return out

    def softmax_pv(scs, k0s, carry):
        vps = [v_ref[0, pl.ds(k0, tk), :] for k0 in k0s]
        new = []
        for hh in range(2):
            m, acc = carry[hh]
            m_new = m
            for sc in scs[hh]:
                m_new = jnp.maximum(m_new, jnp.max(sc, axis=1, keepdims=True))
            acc = jnp.exp2(m - m_new) * acc
            for sc, vp in zip(scs[hh], vps):
                p = jnp.exp2(sc - m_new).astype(BF16)
                vh = jnp.where(head_lanes[hh], vp, jnp.ones_like(vp))
                acc = acc + jnp.dot(p, vh, preferred_element_type=F32)
            new.append((m_new, acc))
        return tuple(new)

    def init_carry():
        return tuple((jnp.full((tq, 1), MASK_VALUE, F32), jnp.zeros((tq, LANES), F32)) for _ in range(2))

    for qi in range(1, s // tq):
        gmax = jnp.max(g_ref[:, qi * tq:(qi + 1) * tq], axis=1, keepdims=True)
        live = (gmax - st_ref[0, 0, 0] >= -EXP2_UNDERFLOW) & (pos < qi * tq)
        n_live = jnp.max(jnp.sum(live.astype(F32), axis=1, keepdims=True), axis=0, keepdims=True)
        nback_ref[qi] = jnp.ceil(n_live * (1.0 / tk)).astype(jnp.int32)[0, 0]

    def far_blocks(qi, qh):
        def step(kj, carry):
            k0 = pl.multiple_of(kj * tk, tk)
            return softmax_pv([[sc] for sc in scores(qh, k0, False)], [k0], carry)

        return lax.fori_loop(qi - nback_ref[qi], qi - 1, step, init_carry())

    half = tq // 2
    causal_top = (lax.broadcasted_iota(jnp.int32, (half, half), 1) <= lax.broadcasted_iota(jnp.int32, (half, half), 0))
    causal_bot = (lax.broadcasted_iota(jnp.int32, (half, tk), 1)
                  <= lax.broadcasted_iota(jnp.int32, (half, tk), 0) + half)

    def tail_scores(slot, q0, qh):
        for hh, sc in enumerate(scores(qh, prev_block(q0), False)):
            s_ref[slot, hh, 0] = sc
        kp = k_ref[0, pl.ds(q0, tk), :]
        ap = ka_ref[0, pl.ds(q0, tk), :]
        for hh in range(2):
            kh = jnp.where(head_lanes[hh], kp, ap)
            top = lax.dot_general(qh[hh][:half], kh[:half], (((1,), (1,)), ((), ())), preferred_element_type=F32)
            bot = lax.dot_general(qh[hh][half:], kh, (((1,), (1,)), ((), ())), preferred_element_type=F32)
            s_ref[slot, hh, 1, :half, :half] = jnp.where(causal_top, top, MASK_VALUE)
            s_ref[slot, hh, 1, :half, half:] = jnp.full((half, tk - half), MASK_VALUE, F32)
            s_ref[slot, hh, 1, half:, :] = jnp.where(causal_bot, bot, MASK_VALUE)

    def tail_finish(slot, q0, carry):
        v_prev = v_ref[0, pl.ds(prev_block(q0), tk), :]
        v_diag = v_ref[0, pl.ds(q0, tk), :]
        new = []
        for hh in range(2):
            m, acc = carry[hh]
            s_prev = s_ref[slot, hh, 0]
            s_diag = s_ref[slot, hh, 1]
            m_new = jnp.maximum(jnp.maximum(m, jnp.max(s_prev, axis=1, keepdims=True)),
                                jnp.max(s_diag, axis=1, keepdims=True))
            vh_prev = jnp.where(head_lanes[hh], v_prev, jnp.ones_like(v_prev))
            vh_diag = jnp.where(head_lanes[hh], v_diag, jnp.ones_like(v_diag))
            acc = jnp.exp2(m - m_new) * acc + jnp.dot(
                jnp.exp2(s_prev - m_new).astype(BF16), vh_prev, preferred_element_type=F32)
            p_diag = jnp.exp2(s_diag - m_new).astype(BF16)
            acc = acc + jnp.concatenate(
                [jnp.dot(p_diag[:half, :half], vh_diag[:half], preferred_element_type=F32),
                 jnp.dot(p_diag[half:], vh_diag, preferred_element_type=F32)], axis=0)
            new.append((m_new, acc))
        finish(q0, tuple(new))

    def finish(q0, carry):
        acc0, acc1 = carry[0][1], carry[1][1]
        out = jnp.where(head_lanes[0], acc0 / pltpu.roll(acc0, HEAD_DIM, 1), acc1 / pltpu.roll(acc1, HEAD_DIM, 1))
        o_ref[0, pl.ds(q0, tq), :] = out.astype(o_ref.dtype)

    finish(0, softmax_pv([[sc] for sc in scores(load_q(0), 0, True)], [0], init_carry()))

    n_q = s // tq
    last = n_q - 1
    tail_scores(0, tq, load_q(tq))

    def pair(i, _):
        qa = 2 * i + 1
        qb = qa + 1
        qn = jnp.minimum(qa + 2, last)
        a0, b0, n0 = (pl.multiple_of(x * tq, tq) for x in (qa, qb, qn))
        qha, qhb, qhn = load_q(a0), load_q(b0), load_q(n0)
        carry_a = far_blocks(qa, qha)
        carry_b = far_blocks(qb, qhb)
        tail_scores(1, b0, qhb)
        tail_finish(0, a0, carry_a)
        tail_scores(0, n0, qhn)
        tail_finish(1, b0, carry_b)
        return 0

    lax.fori_loop(0, last // 2, pair, 0)
    if last % 2 == 1:
        q0 = last * tq
        tail_finish(0, q0, far_blocks(last, load_q(q0)))


def _attention(q, k, kaug, v, stats, *, tq=512):
    b, s, d = q.shape
    n_groups = d // LANES
    heads_per_group = LANES // HEAD_DIM
    assert heads_per_group == 2 and stats.shape == (b, N_STATS, n_groups, heads_per_group, s)
    kern = functools.partial(_attn_kernel, s=s, tq=tq)
    col_spec = pl.BlockSpec((1, s, LANES), lambda i, j: (i, 0, j))
    return pl.pallas_call(
        kern,
        grid=(b, n_groups),
        in_specs=[col_spec, col_spec, col_spec, col_spec,
                  pl.BlockSpec((1, N_STATS, 1, heads_per_group, s), lambda i, j: (i, 0, j, 0, 0))],
        out_specs=col_spec,
        out_shape=jax.ShapeDtypeStruct((b, s, d), BF16),
        scratch_shapes=[pltpu.VMEM((heads_per_group, s), F32),
                        pltpu.VMEM((2, heads_per_group, 2, tq, tq), F32),
                        pltpu.SMEM((s // tq,), jnp.int32)],
        compiler_params=pltpu.CompilerParams(
            dimension_semantics=("arbitrary", "arbitrary"), vmem_limit_bytes=VMEM_LIMIT_BYTES),
        name="fox_attention",
    )(q, k, kaug, v, stats)


def kernel(x, g_mix_pre, g_mix_post, g_ffn_pre, g_ffn_post, conv_pw1_w, conv_pw1_b, conv_dw_w, conv_dw_b,
           conv_ln_g, conv_ln_b, conv_pw2_w, conv_pw2_b, attn_w_in, attn_b_f, attn_w_o, mlp_w_up, mlp_w_down):
    b, s, d = x.shape
    depth = g_mix_pre.shape[0]
    for i in range(depth):
        j = i // 2
        attn = None
        if i % 2 == 0:
            x = _conv_mixer(x, g_mix_pre[i], conv_pw1_w[j], conv_pw1_b[j], conv_dw_w[j], conv_dw_b[j],
                            conv_ln_g[j], conv_ln_b[j], conv_pw2_w[j], conv_pw2_b[j], g_mix_post[i])
        else:
            q, k, kaug, v, stats = _qkv(x, g_mix_pre[i], attn_w_in, j, attn_b_f[j])
            attn = (_attention(q, k, kaug, v, stats).reshape(b * s, d), attn_w_o[j], g_mix_post[i])
        x = _mlp(x.reshape(b * s, d), g_ffn_pre[i], mlp_w_up, mlp_w_down, i, g_ffn_post[i], attn).reshape(b, s, d)
    return x
```

```python
import functools

import jax
import jax.numpy as jnp
from jax import lax
from jax.experimental import pallas as pl
from jax.experimental.pallas import tpu as pltpu

N_HEADS = 16
HEAD_DIM = 64
CONV_WIDTH = 31
RMS_EPS = 1e-6
LN_EPS = 1e-5
MASK_VALUE = -1e30

LANES = 128
SUBLANES = 8
CONV_ROWS = 256
CONV_COLS = 256
FF_CHUNK = 1024
CONV_HALO = 32
VMEM_LIMIT_BYTES = 56 * 1024 * 1024
EXP2_UNDERFLOW = 150.5
LOG2E = 1.4426950408889634
NORM_SLACK = 1.01
DIAG_SLACK = 2.0 ** -7
N_STATS = 4

F32 = jnp.float32
BF16 = jnp.bfloat16


def _rms(x, g):
    return x * lax.rsqrt(jnp.mean(x * x, axis=-1, keepdims=True) + RMS_EPS) * g


def _const_spec(shape):
    nd = len(shape)
    return pl.BlockSpec(shape, lambda *_: (0,) * nd, pipeline_mode=pl.Buffered(1))


def _conv_mixer_kernel(x_ref, gpre_ref, w1_ref, b1_ref, dww_ref, dwb_ref, lng_ref, lnb_ref,
                       w2_ref, b2_ref, gpost_ref, o_ref, ubuf_ref, ybuf_ref, *, tm, d):
    x = x_ref[0]
    hb = _rms(x, gpre_ref[...]).astype(BF16)

    @pl.when(pl.program_id(1) == 0)
    def _():
        ubuf_ref[0:CONV_HALO, :] = jnp.zeros((CONV_HALO, d), F32)
        ubuf_ref[CONV_HALO + tm:CONV_HALO + tm + SUBLANES, :] = jnp.zeros((SUBLANES, d), F32)

    def conv_chunk(r0, c0):
        acc = jnp.broadcast_to(dwb_ref[:, c0:c0 + LANES], (CONV_ROWS, LANES))
        for r in range(SUBLANES):
            z = None
            for o in range(r if r >= 2 else r + SUBLANES, CONV_WIDTH + 2, SUBLANES):
                a8 = r0 + o - r
                term = ubuf_ref[a8:a8 + CONV_ROWS + SUBLANES, c0:c0 + LANES] * dww_ref[o - 2:o - 1, c0:c0 + LANES]
                z = term if z is None else z + term
            acc = acc + z[r:r + CONV_ROWS]
        ybuf_ref[r0:r0 + CONV_ROWS, c0:c0 + LANES] = acc

    for g0 in range(0, d, CONV_COLS):
        cols = slice(g0, g0 + CONV_COLS)
        gate_cols = slice(d + g0, d + g0 + CONV_COLS)
        a_u = jnp.dot(hb, w1_ref[:, cols], preferred_element_type=F32) + b1_ref[:, cols]
        a_g = jnp.dot(hb, w1_ref[:, gate_cols], preferred_element_type=F32) + b1_ref[:, gate_cols]
        ubuf_ref[CONV_HALO:CONV_HALO + tm, cols] = a_u * jax.nn.sigmoid(a_g)
        for c0 in range(g0, g0 + CONV_COLS, LANES):
            for r0 in range(0, tm, CONV_ROWS):
                conv_chunk(r0, c0)
        ubuf_ref[0:CONV_HALO, cols] = ubuf_ref[tm:tm + CONV_HALO, cols]

    y = ybuf_ref[...]
    mu = jnp.mean(y, axis=-1, keepdims=True)
    yc = y - mu
    var = jnp.mean(yc * yc, axis=-1, keepdims=True)
    yn = yc * lax.rsqrt(var + LN_EPS) * lng_ref[...] + lnb_ref[...]
    act = yn * jax.nn.sigmoid(yn)
    m = jnp.dot(act.astype(BF16), w2_ref[...], preferred_element_type=F32) + b2_ref[...]
    o_ref[0] = x + _rms(m, gpost_ref[...])


def _conv_mixer(x, gpre, w1, b1, dww, dwb, lng, lnb, w2, b2, gpost, *, tm=512):
    b, s, d = x.shape
    row = lambda v: v.reshape(1, -1).astype(F32)
    kern = functools.partial(_conv_mixer_kernel, tm=tm, d=d)
    return pl.pallas_call(
        kern,
        grid=(b, s // tm),
        in_specs=[
            pl.BlockSpec((1, tm, d), lambda i, j: (i, j, 0)),
            _const_spec((1, d)), _const_spec((d, 2 * d)), _const_spec((1, 2 * d)),
            _const_spec((CONV_WIDTH, d)), _const_spec((1, d)), _const_spec((1, d)), _const_spec((1, d)),
            _const_spec((d, d)), _const_spec((1, d)), _const_spec((1, d)),
        ],
        out_specs=pl.BlockSpec((1, tm, d), lambda i, j: (i, j, 0)),
        out_shape=jax.ShapeDtypeStruct(x.shape, F32),
        scratch_shapes=[pltpu.VMEM((CONV_HALO + tm + SUBLANES, d), F32), pltpu.VMEM((tm, d), F32)],
        compiler_params=pltpu.CompilerParams(
            dimension_semantics=("arbitrary", "arbitrary"), vmem_limit_bytes=VMEM_LIMIT_BYTES),
        name="conv_mixer",
    )(x, row(gpre), w1.astype(BF16), row(b1), dww.astype(F32), row(dwb), row(lng), row(lnb),
      w2.astype(BF16), row(b2), row(gpost))


def _mlp_body(x, gpre_ref, wup_ref, wdown_ref, gpost_ref, o_ref):
    hb = _rms(x, gpre_ref[...]).astype(BF16)
    down = None
    for c0 in range(0, wup_ref.shape[1], FF_CHUNK):
        up = jnp.dot(hb, wup_ref[:, c0:c0 + FF_CHUNK].astype(BF16), preferred_element_type=F32)
        r = jnp.maximum(up, 0.0)
        part = jnp.dot((r * r).astype(BF16), wdown_ref[c0:c0 + FF_CHUNK, :].astype(BF16), preferred_element_type=F32)
        down = part if down is None else down + part
    o_ref[...] = x + _rms(down, gpost_ref[...])


def _mlp_kernel(x_ref, gpre_ref, wup_ref, wdown_ref, gpost_ref, o_ref):
    _mlp_body(x_ref[...], gpre_ref, wup_ref, wdown_ref, gpost_ref, o_ref)


def _attn_out_mlp_kernel(x_ref, a_ref, wo_ref, gmix_ref, gpre_ref, wup_ref, wdown_ref, gpost_ref, o_ref):
    m = jnp.dot(a_ref[...], wo_ref[...], preferred_element_type=F32)
    _mlp_body(x_ref[...] + _rms(m, gmix_ref[...]), gpre_ref, wup_ref, wdown_ref, gpost_ref, o_ref)


def _mlp(x2d, gpre, wup_all, wdown_all, layer, gpost, attn=None, *, tm=512):
    t, d = x2d.shape
    row = lambda v: v.reshape(1, -1).astype(F32)
    act_spec = pl.BlockSpec((tm, d), lambda i: (i, 0))
    mlp_specs = [_const_spec((1, d)), _layer_spec(wup_all, layer), _layer_spec(wdown_all, layer), _const_spec((1, d))]
    mlp_args = (row(gpre), wup_all.astype(F32), wdown_all.astype(F32), row(gpost))
    if attn is None:
        kern, in_specs, args = _mlp_kernel, [act_spec] + mlp_specs, (x2d,) + mlp_args
    else:
        a2d, wo, gmix = attn
        kern = _attn_out_mlp_kernel
        in_specs = [act_spec, act_spec, _const_spec((d, d)), _const_spec((1, d))] + mlp_specs
        args = (x2d, a2d, wo.astype(BF16), row(gmix)) + mlp_args
    return pl.pallas_call(
        kern,
        grid=(t // tm,),
        in_specs=in_specs,
        out_specs=act_spec,
        out_shape=jax.ShapeDtypeStruct(x2d.shape, F32),
        compiler_params=pltpu.CompilerParams(
            dimension_semantics=("arbitrary",), vmem_limit_bytes=VMEM_LIMIT_BYTES),
        name="sqrelu_mlp" if attn is None else "attn_out_mlp",
    )(*args)


def _split3_bf16(x):
    hi = x.astype(BF16)
    r1 = x - hi.astype(F32)
    mid = r1.astype(BF16)
    lo = (r1 - mid.astype(F32)).astype(BF16)
    return hi, mid, lo


def _head_sq_norm_bound(xb, ind):
    xf = xb.astype(F32)
    nsq = jnp.dot((xf * xf).astype(BF16), ind, preferred_element_type=F32)
    return nsq * NORM_SLACK


def _qkv_kernel(x_ref, gpre_ref, win_ref, wf_ref, bf_ref, ind_ref, sel_ref, q_ref, k_ref, ka_ref, v_ref, st_ref,
                carry_ref, *, tm, d):
    x = x_ref[0]
    hb = _rms(x, gpre_ref[...]).astype(BF16)
    proj = jnp.dot(hb, win_ref[:, :3 * d].astype(BF16), preferred_element_type=F32)
    qb = (proj[:, :d] * (HEAD_DIM ** -0.5 * LOG2E)).astype(BF16)
    kb = proj[:, d:2 * d].astype(BF16)
    q_ref[0] = qb
    k_ref[0] = kb
    v_ref[0] = proj[:, 2 * d:].astype(BF16)

    def put_stat(t, per_head):
        rows = per_head.T
        for g in range(N_HEADS // 2):
            st_ref[0, t, g] = rows[2 * g:2 * g + 2, :]

    put_stat(1, _head_sq_norm_bound(qb, ind_ref[...]))
    put_stat(2, _head_sq_norm_bound(kb, ind_ref[...]))
    qk = (qb.astype(F32) * kb.astype(F32)).astype(BF16)
    put_stat(3, jnp.dot(qk, ind_ref[...], preferred_element_type=F32))

    f_logit = jnp.dot(hb, wf_ref[...], preferred_element_type=F32) + bf_ref[...]
    log_f = jax.nn.log_sigmoid(f_logit)

    @pl.when(pl.program_id(1) == 0)
    def _():
        carry_ref[...] = jnp.zeros_like(carry_ref)

    rows = lax.broadcasted_iota(jnp.int32, (tm, tm), 0)
    cols = lax.broadcasted_iota(jnp.int32, (tm, tm), 1)
    tri = (rows >= cols).astype(BF16)
    lane = lax.broadcasted_iota(jnp.int32, (1, LANES), 1)

    def pack3(v):
        v_hi, v_mid, v_lo = (t.astype(F32) for t in _split3_bf16(v))
        return jnp.where(lane < N_HEADS, v_hi, jnp.where(
            lane < 2 * N_HEADS, pltpu.roll(v_mid, N_HEADS, 1), pltpu.roll(v_lo, 2 * N_HEADS, 1))).astype(BF16)

    c3 = jnp.dot(tri, pack3(log_f), preferred_element_type=F32)
    csum = c3 + pltpu.roll(c3, LANES - N_HEADS, 1) + pltpu.roll(c3, LANES - 2 * N_HEADS, 1)
    f_cum = csum + carry_ref[...]
    carry_ref[...] = f_cum[tm - 1:tm, :]
    f2 = f_cum * LOG2E
    put_stat(0, f2)
    ka_ref[0] = jnp.dot(pack3(f2), sel_ref[...], preferred_element_type=F32).astype(BF16)


def _layer_spec(stacked, layer):
    return pl.BlockSpec((None,) + stacked.shape[1:], lambda *_: (layer, 0, 0), pipeline_mode=pl.Buffered(1))


def _qkv(x, gpre, w_in_all, layer, b_f, *, tm=512):
    b, s, d = x.shape
    w_in = w_in_all[layer]
    wf = jnp.zeros((d, LANES), F32).at[:, :N_HEADS].set(w_in[:, 3 * d:]).astype(BF16)
    bf = jnp.zeros((1, LANES), F32).at[0, :N_HEADS].set(b_f.astype(F32))
    ind = (jnp.arange(d)[:, None] // HEAD_DIM == jnp.arange(LANES)[None, :]).astype(BF16)
    rows = jnp.arange(LANES)[:, None]
    head, term = rows % N_HEADS, rows // N_HEADS
    dest = (head // 2) * LANES + (1 - head % 2) * HEAD_DIM
    sel = -((jnp.arange(d)[None, :] == dest + term) & (term < 3)).astype(BF16)
    kern = functools.partial(_qkv_kernel, tm=tm, d=d)
    act_spec = pl.BlockSpec((1, tm, d), lambda i, j: (i, j, 0))
    return pl.pallas_call(
        kern,
        grid=(b, s // tm),
        in_specs=[act_spec, _const_spec((1, d)), _layer_spec(w_in_all, layer), _const_spec((d, LANES)),
                  _const_spec((1, LANES)), _const_spec((d, LANES)), _const_spec((LANES, d))],
        out_specs=[act_spec, act_spec, act_spec, act_spec,
                   pl.BlockSpec((1, N_STATS, N_HEADS // 2, 2, tm), lambda i, j: (i, 0, 0, 0, j))],
        out_shape=[jax.ShapeDtypeStruct((b, s, d), BF16)] * 4
        + [jax.ShapeDtypeStruct((b, N_STATS, N_HEADS // 2, 2, s), F32)],
        scratch_shapes=[pltpu.VMEM((1, LANES), F32)],
        compiler_params=pltpu.CompilerParams(
            dimension_semantics=("arbitrary", "arbitrary"), vmem_limit_bytes=VMEM_LIMIT_BYTES),
        name="attn_qkv",
    )(x, gpre.reshape(1, -1).astype(F32), w_in_all.astype(F32), wf, bf, ind, sel)


def _attn_kernel(q_ref, k_ref, ka_ref, v_ref, st_ref, o_ref, g_ref, s_ref, nback_ref, *, s, tq):
    tk = tq
    lane = lax.broadcasted_iota(jnp.int32, (1, LANES), 1)
    head_lanes = [lane < HEAD_DIM, lane >= HEAD_DIM]
    bias_ones = [((lane >= HEAD_DIM) & (lane < HEAD_DIM + 3)).astype(BF16), (lane < 3).astype(BF16)]
    row_ids = lax.broadcasted_iota(jnp.int32, (tq, tk), 0)
    col_ids = lax.broadcasted_iota(jnp.int32, (tq, tk), 1)
    causal = col_ids <= row_ids
    pos = lax.broadcasted_iota(jnp.int32, (1, s), 1)

    nq, nk = st_ref[0, 1, 0], st_ref[0, 2, 0]
    kmax = jnp.max(nk, axis=1, keepdims=True)
    diag_lb = st_ref[0, 3, 0] - DIAG_SLACK * jnp.sqrt(nq * nk)
    g_ref[...] = jnp.sqrt(nq) * jnp.sqrt(kmax) - diag_lb + st_ref[0, 0, 0]

    def prev_block(q0):
        return q0 - tk if isinstance(q0, int) else pl.multiple_of(q0 - tk, tk)

    def load_q(q0):
        qp = q_ref[0, pl.ds(q0, tq), :]
        return [jnp.where(head_lanes[hh], qp, bias_ones[hh]) for hh in range(2)]

    def scores(qh, k0, masked):
        kp = k_ref[0, pl.ds(k0, tk), :]
        ap = ka_ref[0, pl.ds(k0, tk), :]
        out = []
        for hh in range(2):
            kh = jnp.where(head_lanes[hh], kp, ap)
            sc = lax.dot_general(qh[hh], kh, (((1,), (1,)), ((), ())), preferred_element_type=F32)
            out.append(jnp.where(causal, sc, MASK_VALUE) if masked else sc)
        return out

    def softmax_pv(scs, k0s, carry):
        vps = [v_ref[0, pl.ds(k0, tk), :] for k0 in k0s]
        new = []
        for hh in range(2):
            m, acc = carry[hh]
            m_new = m
            for sc in scs[hh]:
                m_new = jnp.maximum(m_new, jnp.max(sc, axis=1, keepdims=True))
            acc = jnp.exp2(m - m_new) * acc
            for sc, vp in zip(scs[hh], vps):
                p = jnp.exp2(sc - m_new).astype(BF16)
                vh = jnp.where(head_lanes[hh], vp, jnp.ones_like(vp))
                acc = acc + jnp.dot(p, vh, preferred_element_type=F32)
            new.append((m_new, acc))
        return tuple(new)

    def init_carry():
        return tuple((jnp.full((tq, 1), MASK_VALUE, F32), jnp.zeros((tq, LANES), F32)) for _ in range(2))

    for qi in range(1, s // tq):
        gmax = jnp.max(g_ref[:, qi * tq:(qi + 1) * tq], axis=1, keepdims=True)
        live = (gmax - st_ref[0, 0, 0] >= -EXP2_UNDERFLOW) & (pos < qi * tq)
        n_live = jnp.max(jnp.sum(live.astype(F32), axis=1, keepdims=True), axis=0, keepdims=True)
        nback_ref[qi] = jnp.ceil(n_live * (1.0 / tk)).astype(jnp.int32)[0, 0]

    def far_blocks(qi, qh):
        def step(kj, carry):
            k0 = pl.multiple_of(kj * tk, tk)
            return softmax_pv([[sc] for sc in scores(qh, k0, False)], [k0], carry)

        return lax.fori_loop(qi - nback_ref[qi], qi - 1, step, init_carry())

    half = tq // 2
    causal_top = (lax.broadcasted_iota(jnp.int32, (half, half), 1) <= lax.broadcasted_iota(jnp.int32, (half, half), 0))
    causal_bot = (lax.broadcasted_iota(jnp.int32, (half, tk), 1)
                  <= lax.broadcasted_iota(jnp.int32, (half, tk), 0) + half)

    def tail_scores(slot, q0, qh):
        for hh, sc in enumerate(scores(qh, prev_block(q0), False)):
            s_ref[slot, hh, 0] = sc
        kp = k_ref[0, pl.ds(q0, tk), :]
        ap = ka_ref[0, pl.ds(q0, tk), :]
        for hh in range(2):
            kh = jnp.where(head_lanes[hh], kp, ap)
            top = lax.dot_general(qh[hh][:half], kh[:half], (((1,), (1,)), ((), ())), preferred_element_type=F32)
            bot = lax.dot_general(qh[hh][half:], kh, (((1,), (1,)), ((), ())), preferred_element_type=F32)
            s_ref[slot, hh, 1, :half, :half] = jnp.where(causal_top, top, MASK_VALUE)
            s_ref[slot, hh, 1, :half, half:] = jnp.full((half, tk - half), MASK_VALUE, F32)
            s_ref[slot, hh, 1, half:, :] = jnp.where(causal_bot, bot, MASK_VALUE)

    def tail_finish(slot, q0, carry):
        v_prev = v_ref[0, pl.ds(prev_block(q0), tk), :]
        v_diag = v_ref[0, pl.ds(q0, tk), :]
        new = []
        for hh in range(2):
            m, acc = carry[hh]
            s_prev = s_ref[slot, hh, 0]
            s_diag = s_ref[slot, hh, 1]
            m_new = jnp.maximum(jnp.maximum(m, jnp.max(s_prev, axis=1, keepdims=True)),
                                jnp.max(s_diag, axis=1, keepdims=True))
            vh_prev = jnp.where(head_lanes[hh], v_prev, jnp.ones_like(v_prev))
            vh_diag = jnp.where(head_lanes[hh], v_diag, jnp.ones_like(v_diag))
            acc = jnp.exp2(m - m_new) * acc + jnp.dot(
                jnp.exp2(s_prev - m_new).astype(BF16), vh_prev, preferred_element_type=F32)
            p_diag = jnp.exp2(s_diag - m_new).astype(BF16)
            acc = acc + jnp.concatenate(
                [jnp.dot(p_diag[:half, :half], vh_diag[:half], preferred_element_type=F32),
                 jnp.dot(p_diag[half:], vh_diag, preferred_element_type=F32)], axis=0)
            new.append((m_new, acc))
        finish(q0, tuple(new))

    def finish(q0, carry):
        acc0, acc1 = carry[0][1], carry[1][1]
        out = jnp.where(head_lanes[0], acc0 / pltpu.roll(acc0, HEAD_DIM, 1), acc1 / pltpu.roll(acc1, HEAD_DIM, 1))
        o_ref[0, pl.ds(q0, tq), :] = out.astype(o_ref.dtype)

    finish(0, softmax_pv([[sc] for sc in scores(load_q(0), 0, True)], [0], init_carry()))

    n_q = s // tq
    last = n_q - 1
    tail_scores(0, tq, load_q(tq))

    def pair(i, _):
        qa = 2 * i + 1
        qb = qa + 1
        qn = jnp.minimum(qa + 2, last)
        a0, b0, n0 = (pl.multiple_of(x * tq, tq) for x in (qa, qb, qn))
        qha, qhb, qhn = load_q(a0), load_q(b0), load_q(n0)
        carry_a = far_blocks(qa, qha)
        carry_b = far_blocks(qb, qhb)
        tail_scores(1, b0, qhb)
        tail_finish(0, a0, carry_a)
        tail_scores(0, n0, qhn)
        tail_finish(1, b0, carry_b)
        return 0

    lax.fori_loop(0, last // 2, pair, 0)
    if last % 2 == 1:
        q0 = last * tq
        tail_finish(0, q0, far_blocks(last, load_q(q0)))


def _attention(q, k, kaug, v, stats, *, tq=512):
    b, s, d = q.shape
    n_groups = d // LANES
    heads_per_group = LANES // HEAD_DIM
    assert heads_per_group == 2 and stats.shape == (b, N_STATS, n_groups, heads_per_group, s)
    kern = functools.partial(_attn_kernel, s=s, tq=tq)
    col_spec = pl.BlockSpec((1, s, LANES), lambda i, j: (i, 0, j))
    return pl.pallas_call(
        kern,
        grid=(b, n_groups),
        in_specs=[col_spec, col_spec, col_spec, col_spec,
                  pl.BlockSpec((1, N_STATS, 1, heads_per_group, s), lambda i, j: (i, 0, j, 0, 0))],
        out_specs=col_spec,
        out_shape=jax.ShapeDtypeStruct((b, s, d), BF16),
        scratch_shapes=[pltpu.VMEM((heads_per_group, s), F32),
                        pltpu.VMEM((2, heads_per_group, 2, tq, tq), F32),
                        pltpu.SMEM((s // tq,), jnp.int32)],
        compiler_params=pltpu.CompilerParams(
            dimension_semantics=("arbitrary", "arbitrary"), vmem_limit_bytes=VMEM_LIMIT_BYTES),
        name="fox_attention",
    )(q, k, kaug, v, stats)


def kernel(x, g_mix_pre, g_mix_post, g_ffn_pre, g_ffn_post, conv_pw1_w, conv_pw1_b, conv_dw_w, conv_dw_b,
           conv_ln_g, conv_ln_b, conv_pw2_w, conv_pw2_b, attn_w_in, attn_b_f, attn_w_o, mlp_w_up, mlp_w_down):
    b, s, d = x.shape
    depth = g_mix_pre.shape[0]
    for i in range(depth):
        j = i // 2
        attn = None
        if i % 2 == 0:
            x = _conv_mixer(x, g_mix_pre[i], conv_pw1_w[j], conv_pw1_b[j], conv_dw_w[j], conv_dw_b[j],
                            conv_ln_g[j], conv_ln_b[j], conv_pw2_w[j], conv_pw2_b[j], g_mix_post[i])
        else:
            q, k, kaug, v, stats = _qkv(x, g_mix_pre[i], attn_w_in, j, attn_b_f[j])
            attn = (_attention(q, k, kaug, v, stats).reshape(b * s, d), attn_w_o[j], g_mix_post[i])
        x = _mlp(x.reshape(b * s, d), g_ffn_pre[i], mlp_w_up, mlp_w_down, i, g_ffn_post[i], attn).reshape(b, s, d)
    return x
```

```python
import functools

import jax
import jax.numpy as jnp
from jax import lax
from jax.experimental import pallas as pl
from jax.experimental.pallas import tpu as pltpu

N_HEADS = 16
HEAD_DIM = 64
CONV_WIDTH = 31
RMS_EPS = 1e-6
LN_EPS = 1e-5
MASK_VALUE = -1e30

LANES = 128
SUBLANES = 8
CONV_ROWS = 512
CONV_COLS = 256
FF_CHUNK = 1024
CONV_HALO = 32
VMEM_LIMIT_BYTES = 56 * 1024 * 1024
EXP2_UNDERFLOW = 150.5
LOG2E = 1.4426950408889634
NORM_SLACK = 1.01
DIAG_SLACK = 2.0 ** -7
N_STATS = 4

F32 = jnp.float32
BF16 = jnp.bfloat16


def _rms(x, g):
    return x * lax.rsqrt(jnp.mean(x * x, axis=-1, keepdims=True) + RMS_EPS) * g


def _const_spec(shape):
    nd = len(shape)
    return pl.BlockSpec(shape, lambda *_: (0,) * nd, pipeline_mode=pl.Buffered(1))


def _conv_mixer_kernel(x_ref, gpre_ref, w1_ref, b1_ref, dww_ref, dwb_ref, lng_ref, lnb_ref,
                       w2_ref, b2_ref, gpost_ref, o_ref, ubuf_ref, ybuf_ref, *, tm, d):
    x = x_ref[0]
    hb = _rms(x, gpre_ref[...]).astype(BF16)

    @pl.when(pl.program_id(1) == 0)
    def _():
        ubuf_ref[0:CONV_HALO, :] = jnp.zeros((CONV_HALO, d), F32)
        ubuf_ref[CONV_HALO + tm:CONV_HALO + tm + SUBLANES, :] = jnp.zeros((SUBLANES, d), F32)

    def conv_chunk(r0, c0):
        acc = jnp.broadcast_to(dwb_ref[:, c0:c0 + LANES], (CONV_ROWS, LANES))
        for r in range(SUBLANES):
            z = None
            for o in range(r if r >= 2 else r + SUBLANES, CONV_WIDTH + 2, SUBLANES):
                a8 = r0 + o - r
                term = ubuf_ref[a8:a8 + CONV_ROWS + SUBLANES, c0:c0 + LANES] * dww_ref[o - 2:o - 1, c0:c0 + LANES]
                z = term if z is None else z + term
            acc = acc + z[r:r + CONV_ROWS]
        ybuf_ref[r0:r0 + CONV_ROWS, c0:c0 + LANES] = acc

    for g0 in range(0, d, CONV_COLS):
        cols = slice(g0, g0 + CONV_COLS)
        gate_cols = slice(d + g0, d + g0 + CONV_COLS)
        a_u = jnp.dot(hb, w1_ref[:, cols], preferred_element_type=F32) + b1_ref[:, cols]
        a_g = jnp.dot(hb, w1_ref[:, gate_cols], preferred_element_type=F32) + b1_ref[:, gate_cols]
        ubuf_ref[CONV_HALO:CONV_HALO + tm, cols] = a_u * jax.nn.sigmoid(a_g)
        for c0 in range(g0, g0 + CONV_COLS, LANES):
            for r0 in range(0, tm, CONV_ROWS):
                conv_chunk(r0, c0)
        ubuf_ref[0:CONV_HALO, cols] = ubuf_ref[tm:tm + CONV_HALO, cols]

    y = ybuf_ref[...]
    mu = jnp.mean(y, axis=-1, keepdims=True)
    yc = y - mu
    var = jnp.mean(yc * yc, axis=-1, keepdims=True)
    yn = yc * lax.rsqrt(var + LN_EPS) * lng_ref[...] + lnb_ref[...]
    act = yn * jax.nn.sigmoid(yn)
    m = jnp.dot(act.astype(BF16), w2_ref[...], preferred_element_type=F32) + b2_ref[...]
    o_ref[0] = x + _rms(m, gpost_ref[...])


def _conv_mixer(x, gpre, w1, b1, dww, dwb, lng, lnb, w2, b2, gpost, *, tm=512):
    b, s, d = x.shape
    row = lambda v: v.reshape(1, -1).astype(F32)
    kern = functools.partial(_conv_mixer_kernel, tm=tm, d=d)
    return pl.pallas_call(
        kern,
        grid=(b, s // tm),
        in_specs=[
            pl.BlockSpec((1, tm, d), lambda i, j: (i, j, 0)),
            _const_spec((1, d)), _const_spec((d, 2 * d)), _const_spec((1, 2 * d)),
            _const_spec((CONV_WIDTH, d)), _const_spec((1, d)), _const_spec((1, d)), _const_spec((1, d)),
            _const_spec((d, d)), _const_spec((1, d)), _const_spec((1, d)),
        ],
        out_specs=pl.BlockSpec((1, tm, d), lambda i, j: (i, j, 0)),
        out_shape=jax.ShapeDtypeStruct(x.shape, F32),
        scratch_shapes=[pltpu.VMEM((CONV_HALO + tm + SUBLANES, d), F32), pltpu.VMEM((tm, d), F32)],
        compiler_params=pltpu.CompilerParams(
            dimension_semantics=("arbitrary", "arbitrary"), vmem_limit_bytes=VMEM_LIMIT_BYTES),
        name="conv_mixer",
    )(x, row(gpre), w1.astype(BF16), row(b1), dww.astype(F32), row(dwb), row(lng), row(lnb),
      w2.astype(BF16), row(b2), row(gpost))


def _mlp_body(x, gpre_ref, wup_ref, wdown_ref, gpost_ref, o_ref):
    hb = _rms(x, gpre_ref[...]).astype(BF16)
    down = None
    for c0 in range(0, wup_ref.shape[1], FF_CHUNK):
        up = jnp.dot(hb, wup_ref[:, c0:c0 + FF_CHUNK].astype(BF16), preferred_element_type=F32)
        r = jnp.maximum(up, 0.0)
        part = jnp.dot((r * r).astype(BF16), wdown_ref[c0:c0 + FF_CHUNK, :].astype(BF16), preferred_element_type=F32)
        down = part if down is None else down + part
    o_ref[...] = x + _rms(down, gpost_ref[...])


def _mlp_kernel(x_ref, gpre_ref, wup_ref, wdown_ref, gpost_ref, o_ref):
    _mlp_body(x_ref[...], gpre_ref, wup_ref, wdown_ref, gpost_ref, o_ref)


def _attn_out_mlp_kernel(x_ref, a_ref, wo_ref, gmix_ref, gpre_ref, wup_ref, wdown_ref, gpost_ref, o_ref):
    m = jnp.dot(a_ref[...], wo_ref[...], preferred_element_type=F32)
    _mlp_body(x_ref[...] + _rms(m, gmix_ref[...]), gpre_ref, wup_ref, wdown_ref, gpost_ref, o_ref)


def _mlp(x2d, gpre, wup_all, wdown_all, layer, gpost, attn=None, *, tm=512):
    t, d = x2d.shape
    row = lambda v: v.reshape(1, -1).astype(F32)
    act_spec = pl.BlockSpec((tm, d), lambda i: (i, 0))
    mlp_specs = [_const_spec((1, d)), _layer_spec(wup_all, layer), _layer_spec(wdown_all, layer), _const_spec((1, d))]
    mlp_args = (row(gpre), wup_all.astype(F32), wdown_all.astype(F32), row(gpost))
    if attn is None:
        kern, in_specs, args = _mlp_kernel, [act_spec] + mlp_specs, (x2d,) + mlp_args
    else:
        a2d, wo, gmix = attn
        kern = _attn_out_mlp_kernel
        in_specs = [act_spec, act_spec, _const_spec((d, d)), _const_spec((1, d))] + mlp_specs
        args = (x2d, a2d, wo.astype(BF16), row(gmix)) + mlp_args
    return pl.pallas_call(
        kern,
        grid=(t // tm,),
        in_specs=in_specs,
        out_specs=act_spec,
        out_shape=jax.ShapeDtypeStruct(x2d.shape, F32),
        compiler_params=pltpu.CompilerParams(
            dimension_semantics=("arbitrary",), vmem_limit_bytes=VMEM_LIMIT_BYTES),
        name="sqrelu_mlp" if attn is None else "attn_out_mlp",
    )(*args)


def _split3_bf16(x):
    hi = x.astype(BF16)
    r1 = x - hi.astype(F32)
    mid = r1.astype(BF16)
    lo = (r1 - mid.astype(F32)).astype(BF16)
    return hi, mid, lo


def _head_sq_norm_bound(xb, ind):
    xf = xb.astype(F32)
    nsq = jnp.dot((xf * xf).astype(BF16), ind, preferred_element_type=F32)
    return nsq * NORM_SLACK


def _qkv_kernel(x_ref, gpre_ref, win_ref, wf_ref, bf_ref, ind_ref, sel_ref, q_ref, k_ref, ka_ref, v_ref, st_ref,
                carry_ref, *, tm, d):
    x = x_ref[0]
    hb = _rms(x, gpre_ref[...]).astype(BF16)
    proj = jnp.dot(hb, win_ref[:, :3 * d].astype(BF16), preferred_element_type=F32)
    qb = (proj[:, :d] * (HEAD_DIM ** -0.5 * LOG2E)).astype(BF16)
    kb = proj[:, d:2 * d].astype(BF16)
    q_ref[0] = qb
    k_ref[0] = kb
    v_ref[0] = proj[:, 2 * d:].astype(BF16)

    def put_stat(t, per_head):
        rows = per_head.T
        for g in range(N_HEADS // 2):
            st_ref[0, t, g] = rows[2 * g:2 * g + 2, :]

    put_stat(1, _head_sq_norm_bound(qb, ind_ref[...]))
    put_stat(2, _head_sq_norm_bound(kb, ind_ref[...]))
    qk = (qb.astype(F32) * kb.astype(F32)).astype(BF16)
    put_stat(3, jnp.dot(qk, ind_ref[...], preferred_element_type=F32))

    f_logit = jnp.dot(hb, wf_ref[...], preferred_element_type=F32) + bf_ref[...]
    log_f = jax.nn.log_sigmoid(f_logit)

    @pl.when(pl.program_id(1) == 0)
    def _():
        carry_ref[...] = jnp.zeros_like(carry_ref)

    rows = lax.broadcasted_iota(jnp.int32, (tm, tm), 0)
    cols = lax.broadcasted_iota(jnp.int32, (tm, tm), 1)
    tri = (rows >= cols).astype(BF16)
    lane = lax.broadcasted_iota(jnp.int32, (1, LANES), 1)

    def pack3(v):
        v_hi, v_mid, v_lo = (t.astype(F32) for t in _split3_bf16(v))
        return jnp.where(lane < N_HEADS, v_hi, jnp.where(
            lane < 2 * N_HEADS, pltpu.roll(v_mid, N_HEADS, 1), pltpu.roll(v_lo, 2 * N_HEADS, 1))).astype(BF16)

    c3 = jnp.dot(tri, pack3(log_f), preferred_element_type=F32)
    csum = c3 + pltpu.roll(c3, LANES - N_HEADS, 1) + pltpu.roll(c3, LANES - 2 * N_HEADS, 1)
    f_cum = csum + carry_ref[...]
    carry_ref[...] = f_cum[tm - 1:tm, :]
    f2 = f_cum * LOG2E
    put_stat(0, f2)
    ka_ref[0] = jnp.dot(pack3(f2), sel_ref[...], preferred_element_type=F32).astype(BF16)


def _layer_spec(stacked, layer):
    return pl.BlockSpec((None,) + stacked.shape[1:], lambda *_: (layer, 0, 0), pipeline_mode=pl.Buffered(1))


def _qkv(x, gpre, w_in_all, layer, b_f, *, tm=512):
    b, s, d = x.shape
    w_in = w_in_all[layer]
    wf = jnp.zeros((d, LANES), F32).at[:, :N_HEADS].set(w_in[:, 3 * d:]).astype(BF16)
    bf = jnp.zeros((1, LANES), F32).at[0, :N_HEADS].set(b_f.astype(F32))
    ind = (jnp.arange(d)[:, None] // HEAD_DIM == jnp.arange(LANES)[None, :]).astype(BF16)
    rows = jnp.arange(LANES)[:, None]
    head, term = rows % N_HEADS, rows // N_HEADS
    dest = (head // 2) * LANES + (1 - head % 2) * HEAD_DIM
    sel = -((jnp.arange(d)[None, :] == dest + term) & (term < 3)).astype(BF16)
    kern = functools.partial(_qkv_kernel, tm=tm, d=d)
    act_spec = pl.BlockSpec((1, tm, d), lambda i, j: (i, j, 0))
    return pl.pallas_call(
        kern,
        grid=(b, s // tm),
        in_specs=[act_spec, _const_spec((1, d)), _layer_spec(w_in_all, layer), _const_spec((d, LANES)),
                  _const_spec((1, LANES)), _const_spec((d, LANES)), _const_spec((LANES, d))],
        out_specs=[act_spec, act_spec, act_spec, act_spec,
                   pl.BlockSpec((1, N_STATS, N_HEADS // 2, 2, tm), lambda i, j: (i, 0, 0, 0, j))],
        out_shape=[jax.ShapeDtypeStruct((b, s, d), BF16)] * 4
        + [jax.ShapeDtypeStruct((b, N_STATS, N_HEADS // 2, 2, s), F32)],
        scratch_shapes=[pltpu.VMEM((1, LANES), F32)],
        compiler_params=pltpu.CompilerParams(
            dimension_semantics=("arbitrary", "arbitrary"), vmem_limit_bytes=VMEM_LIMIT_BYTES),
        name="attn_qkv",
    )(x, gpre.reshape(1, -1).astype(F32), w_in_all.astype(F32), wf, bf, ind, sel)


def _attn_kernel(q_ref, k_ref, ka_ref, v_ref, st_ref, o_ref, g_ref, s_ref, nback_ref, *, s, tq):
    tk = tq
    lane = lax.broadcasted_iota(jnp.int32, (1, LANES), 1)
    head_lanes = [lane < HEAD_DIM, lane >= HEAD_DIM]
    bias_ones = [((lane >= HEAD_DIM) & (lane < HEAD_DIM + 3)).astype(BF16), (lane < 3).astype(BF16)]
    row_ids = lax.broadcasted_iota(jnp.int32, (tq, tk), 0)
    col_ids = lax.broadcasted_iota(jnp.int32, (tq, tk), 1)
    causal = col_ids <= row_ids
    pos = lax.broadcasted_iota(jnp.int32, (1, s), 1)

    nq, nk = st_ref[0, 1, 0], st_ref[0, 2, 0]
    kmax = jnp.max(nk, axis=1, keepdims=True)
    diag_lb = st_ref[0, 3, 0] - DIAG_SLACK * jnp.sqrt(nq * nk)
    g_ref[...] = jnp.sqrt(nq) * jnp.sqrt(kmax) - diag_lb + st_ref[0, 0, 0]

    def prev_block(q0):
        return q0 - tk if isinstance(q0, int) else pl.multiple_of(q0 - tk, tk)

    def load_q(q0):
        qp = q_ref[0, pl.ds(q0, tq), :]
        return [jnp.where(head_lanes[hh], qp, bias_ones[hh]) for hh in range(2)]

    def scores(qh, k0, masked):
        kp = k_ref[0, pl.ds(k0, tk), :]
        ap = ka_ref[0, pl.ds(k0, tk), :]
        out = []
        for hh in range(2):
            kh = jnp.where(head_lanes[hh], kp, ap)
            sc = lax.dot_general(qh[hh], kh, (((1,), (1,)), ((), ())), preferred_element_type=F32)
            out.append(jnp.where(causal, sc, MASK_VALUE) if masked else sc)
        return out

    def softmax_pv(scs, k0s, carry):
        vps = [v_ref[0, pl.ds(k0, tk), :] for k0 in k0s]
        new = []
        for hh in range(2):
            m, acc = carry[hh]
            m_new = m
            for sc in scs[hh]:
                m_new = jnp.maximum(m_new, jnp.max(sc, axis=1, keepdims=True))
            acc = jnp.exp2(m - m_new) * acc
            for sc, vp in zip(scs[hh], vps):
                p = jnp.exp2(sc - m_new).astype(BF16)
                vh = jnp.where(head_lanes[hh], vp, jnp.ones_like(vp))
                acc = acc + jnp.dot(p, vh, preferred_element_type=F32)
            new.append((m_new, acc))
        return tuple(new)

    def init_carry():
        return tuple((jnp.full((tq, 1), MASK_VALUE, F32), jnp.zeros((tq, LANES), F32)) for _ in range(2))

    for qi in range(1, s // tq):
        gmax = jnp.max(g_ref[:, qi * tq:(qi + 1) * tq], axis=1, keepdims=True)
        live = (gmax - st_ref[0, 0, 0] >= -EXP2_UNDERFLOW) & (pos < qi * tq)
        n_live = jnp.max(jnp.sum(live.astype(F32), axis=1, keepdims=True), axis=0, keepdims=True)
        nback_ref[qi] = jnp.ceil(n_live * (1.0 / tk)).astype(jnp.int32)[0, 0]

    def far_blocks(qi, qh):
        def step(kj, carry):
            k0 = pl.multiple_of(kj * tk, tk)
            return softmax_pv([[sc] for sc in scores(qh, k0, False)], [k0], carry)

        return lax.fori_loop(qi - nback_ref[qi], qi - 1, step, init_carry())

    half = tq // 2
    causal_top = (lax.broadcasted_iota(jnp.int32, (half, half), 1) <= lax.broadcasted_iota(jnp.int32, (half, half), 0))
    causal_bot = (lax.broadcasted_iota(jnp.int32, (half, tk), 1)
                  <= lax.broadcasted_iota(jnp.int32, (half, tk), 0) + half)

    def tail_scores(slot, q0, qh):
        for hh, sc in enumerate(scores(qh, prev_block(q0), False)):
            s_ref[slot, hh, 0] = sc
        kp = k_ref[0, pl.ds(q0, tk), :]
        ap = ka_ref[0, pl.ds(q0, tk), :]
        for hh in range(2):
            kh = jnp.where(head_lanes[hh], kp, ap)
            top = lax.dot_general(qh[hh][:half], kh[:half], (((1,), (1,)), ((), ())), preferred_element_type=F32)
            bot = lax.dot_general(qh[hh][half:], kh, (((1,), (1,)), ((), ())), preferred_element_type=F32)
            s_ref[slot, hh, 1, :half, :half] = jnp.where(causal_top, top, MASK_VALUE)
            s_ref[slot, hh, 1, :half, half:] = jnp.full((half, tk - half), MASK_VALUE, F32)
            s_ref[slot, hh, 1, half:, :] = jnp.where(causal_bot, bot, MASK_VALUE)

    def tail_finish(slot, q0, carry):
        v_prev = v_ref[0, pl.ds(prev_block(q0), tk), :]
        v_diag = v_ref[0, pl.ds(q0, tk), :]
        new = []
        for hh in range(2):
            m, acc = carry[hh]
            s_prev = s_ref[slot, hh, 0]
            s_diag = s_ref[slot, hh, 1]
            m_new = jnp.maximum(jnp.maximum(m, jnp.max(s_prev, axis=1, keepdims=True)),
                                jnp.max(s_diag, axis=1, keepdims=True))
            vh_prev = jnp.where(head_lanes[hh], v_prev, jnp.ones_like(v_prev))
            vh_diag = jnp.where(head_lanes[hh], v_diag, jnp.ones_like(v_diag))
            acc = jnp.exp2(m - m_new) * acc + jnp.dot(
                jnp.exp2(s_prev - m_new).astype(BF16), vh_prev, preferred_element_type=F32)
            p_diag = jnp.exp2(s_diag - m_new).astype(BF16)
            acc = acc + jnp.concatenate(
                [jnp.dot(p_diag[:half, :half], vh_diag[:half], preferred_element_type=F32),
                 jnp.dot(p_diag[half:], vh_diag, preferred_element_type=F32)], axis=0)
            new.append((m_new, acc))
        finish(q0, tuple(new))

    def finish(q0, carry):
        acc0, acc1 = carry[0][1], carry[1][1]
        out = jnp.where(head_lanes[0], acc0 / pltpu.roll(acc0, HEAD_DIM, 1), acc1 / pltpu.roll(acc1, HEAD_DIM, 1))
        o_ref[0, pl.ds(q0, tq), :] = out.astype(o_ref.dtype)

    finish(0, softmax_pv([[sc] for sc in scores(load_q(0), 0, True)], [0], init_carry()))

    n_q = s // tq
    last = n_q - 1
    tail_scores(0, tq, load_q(tq))

    def pair(i, _):
        qa = 2 * i + 1
        qb = qa + 1
        qn = jnp.minimum(qa + 2, last)
        a0, b0, n0 = (pl.multiple_of(x * tq, tq) for x in (qa, qb, qn))
        qha, qhb, qhn = load_q(a0), load_q(b0), load_q(n0)
        carry_a = far_blocks(qa, qha)
        carry_b = far_blocks(qb, qhb)
        tail_scores(1, b0, qhb)
        tail_finish(0, a0, carry_a)
        tail_scores(0, n0, qhn)
        tail_finish(1, b0, carry_b)
        return 0

    lax.fori_loop(0, last // 2, pair, 0)
    if last % 2 == 1:
        q0 = last * tq
        tail_finish(0, q0, far_blocks(last, load_q(q0)))


def _attention(q, k, kaug, v, stats, *, tq=512):
    b, s, d = q.shape
    n_groups = d // LANES
    heads_per_group = LANES // HEAD_DIM
    assert heads_per_group == 2 and stats.shape == (b, N_STATS, n_groups, heads_per_group, s)
    kern = functools.partial(_attn_kernel, s=s, tq=tq)
    col_spec = pl.BlockSpec((1, s, LANES), lambda i, j: (i, 0, j))
    return pl.pallas_call(
        kern,
        grid=(b, n_groups),
        in_specs=[col_spec, col_spec, col_spec, col_spec,
                  pl.BlockSpec((1, N_STATS, 1, heads_per_group, s), lambda i, j: (i, 0, j, 0, 0))],
        out_specs=col_spec,
        out_shape=jax.ShapeDtypeStruct((b, s, d), BF16),
        scratch_shapes=[pltpu.VMEM((heads_per_group, s), F32),
                        pltpu.VMEM((2, heads_per_group, 2, tq, tq), F32),
                        pltpu.SMEM((s // tq,), jnp.int32)],
        compiler_params=pltpu.CompilerParams(
            dimension_semantics=("arbitrary", "arbitrary"), vmem_limit_bytes=VMEM_LIMIT_BYTES),
        name="fox_attention",
    )(q, k, kaug, v, stats)


def kernel(x, g_mix_pre, g_mix_post, g_ffn_pre, g_ffn_post, conv_pw1_w, conv_pw1_b, conv_dw_w, conv_dw_b,
           conv_ln_g, conv_ln_b, conv_pw2_w, conv_pw2_b, attn_w_in, attn_b_f, attn_w_o, mlp_w_up, mlp_w_down):
    b, s, d = x.shape
    depth = g_mix_pre.shape[0]
    for i in range(depth):
        j = i // 2
        attn = None
        if i % 2 == 0:
            x = _conv_mixer(x, g_mix_pre[i], conv_pw1_w[j], conv_pw1_b[j], conv_dw_w[j], conv_dw_b[j],
                            conv_ln_g[j], conv_ln_b[j], conv_pw2_w[j], conv_pw2_b[j], g_mix_post[i])
        else:
            q, k, kaug, v, stats = _qkv(x, g_mix_pre[i], attn_w_in, j, attn_b_f[j])
            attn = (_attention(q, k, kaug, v, stats).reshape(b * s, d), attn_w_o[j], g_mix_post[i])
        x = _mlp(x.reshape(b * s, d), g_ffn_pre[i], mlp_w_up, mlp_w_down, i, g_ffn_post[i], attn).reshape(b, s, d)
    return x
```

```python
import functools

import jax
import jax.numpy as jnp
from jax import lax
from jax.experimental import pallas as pl
from jax.experimental.pallas import tpu as pltpu

N_HEADS = 16
HEAD_DIM = 64
CONV_WIDTH = 31
RMS_EPS = 1e-6
LN_EPS = 1e-5
MASK_VALUE = -1e30

LANES = 128
SUBLANES = 8
CONV_ROWS = 512
CONV_COLS = 256
FF_CHUNK = 1024
CONV_HALO = 32
VMEM_LIMIT_BYTES = 56 * 1024 * 1024
EXP2_UNDERFLOW = 150.5
LOG2E = 1.4426950408889634
NORM_SLACK = 1.01
DIAG_SLACK = 2.0 ** -7
N_STATS = 4

F32 = jnp.float32
BF16 = jnp.bfloat16


def _rms(x, g):
    return x * lax.rsqrt(jnp.mean(x * x, axis=-1, keepdims=True) + RMS_EPS) * g


def _const_spec(shape):
    nd = len(shape)
    return pl.BlockSpec(shape, lambda *_: (0,) * nd, pipeline_mode=pl.Buffered(1))


def _conv_mixer_kernel(x_ref, gpre_ref, w1_ref, b1_ref, dww_ref, dwb_ref, lng_ref, lnb_ref,
                       w2_ref, b2_ref, gpost_ref, o_ref, ubuf_ref, ybuf_ref, *, tm, d):
    x = x_ref[0]
    hb = _rms(x, gpre_ref[...]).astype(BF16)

    @pl.when(pl.program_id(1) == 0)
    def _():
        ubuf_ref[0:CONV_HALO, :] = jnp.zeros((CONV_HALO, d), F32)
        ubuf_ref[CONV_HALO + tm:CONV_HALO + tm + SUBLANES, :] = jnp.zeros((SUBLANES, d), F32)

    def conv_chunk(r0, c0):
        acc = jnp.broadcast_to(dwb_ref[:, c0:c0 + LANES], (CONV_ROWS, LANES))
        for r in range(SUBLANES):
            z = None
            for o in range(r if r >= 2 else r + SUBLANES, CONV_WIDTH + 2, SUBLANES):
                a8 = r0 + o - r
                term = ubuf_ref[a8:a8 + CONV_ROWS + SUBLANES, c0:c0 + LANES] * dww_ref[o - 2:o - 1, c0:c0 + LANES]
                z = term if z is None else z + term
            acc = acc + z[r:r + CONV_ROWS]
        ybuf_ref[r0:r0 + CONV_ROWS, c0:c0 + LANES] = acc

    for g0 in range(0, d, CONV_COLS):
        cols = slice(g0, g0 + CONV_COLS)
        gate_cols = slice(d + g0, d + g0 + CONV_COLS)
        a_u = jnp.dot(hb, w1_ref[:, cols], preferred_element_type=F32) + b1_ref[:, cols]
        a_g = jnp.dot(hb, w1_ref[:, gate_cols], preferred_element_type=F32) + b1_ref[:, gate_cols]
        ubuf_ref[CONV_HALO:CONV_HALO + tm, cols] = a_u * jax.nn.sigmoid(a_g)
        for c0 in range(g0, g0 + CONV_COLS, LANES):
            for r0 in range(0, tm, CONV_ROWS):
                conv_chunk(r0, c0)
        ubuf_ref[0:CONV_HALO, cols] = ubuf_ref[tm:tm + CONV_HALO, cols]

    y = ybuf_ref[...]
    mu = jnp.mean(y, axis=-1, keepdims=True)
    yc = y - mu
    var = jnp.mean(yc * yc, axis=-1, keepdims=True)
    yn = yc * lax.rsqrt(var + LN_EPS) * lng_ref[...] + lnb_ref[...]
    act = yn * jax.nn.sigmoid(yn)
    m = jnp.dot(act.astype(BF16), w2_ref[...], preferred_element_type=F32) + b2_ref[...]
    o_ref[0] = x + _rms(m, gpost_ref[...])


def _conv_mixer(x, gpre, w1, b1, dww, dwb, lng, lnb, w2, b2, gpost, *, tm=1024):
    b, s, d = x.shape
    row = lambda v: v.reshape(1, -1).astype(F32)
    kern = functools.partial(_conv_mixer_kernel, tm=tm, d=d)
    return pl.pallas_call(
        kern,
        grid=(b, s // tm),
        in_specs=[
            pl.BlockSpec((1, tm, d), lambda i, j: (i, j, 0)),
            _const_spec((1, d)), _const_spec((d, 2 * d)), _const_spec((1, 2 * d)),
            _const_spec((CONV_WIDTH, d)), _const_spec((1, d)), _const_spec((1, d)), _const_spec((1, d)),
            _const_spec((d, d)), _const_spec((1, d)), _const_spec((1, d)),
        ],
        out_specs=pl.BlockSpec((1, tm, d), lambda i, j: (i, j, 0)),
        out_shape=jax.ShapeDtypeStruct(x.shape, F32),
        scratch_shapes=[pltpu.VMEM((CONV_HALO + tm + SUBLANES, d), F32), pltpu.VMEM((tm, d), F32)],
        compiler_params=pltpu.CompilerParams(
            dimension_semantics=("arbitrary", "arbitrary"), vmem_limit_bytes=VMEM_LIMIT_BYTES),
        name="conv_mixer",
    )(x, row(gpre), w1.astype(BF16), row(b1), dww.astype(F32), row(dwb), row(lng), row(lnb),
      w2.astype(BF16), row(b2), row(gpost))


def _mlp_body(x, gpre_ref, wup_ref, wdown_ref, gpost_ref, o_ref):
    hb = _rms(x, gpre_ref[...]).astype(BF16)
    down = None
    for c0 in range(0, wup_ref.shape[1], FF_CHUNK):
        up = jnp.dot(hb, wup_ref[:, c0:c0 + FF_CHUNK].astype(BF16), preferred_element_type=F32)
        r = jnp.maximum(up, 0.0)
        part = jnp.dot((r * r).astype(BF16), wdown_ref[c0:c0 + FF_CHUNK, :].astype(BF16), preferred_element_type=F32)
        down = part if down is None else down + part
    o_ref[...] = x + _rms(down, gpost_ref[...])


def _mlp_kernel(x_ref, gpre_ref, wup_ref, wdown_ref, gpost_ref, o_ref):
    _mlp_body(x_ref[...], gpre_ref, wup_ref, wdown_ref, gpost_ref, o_ref)


def _attn_out_mlp_kernel(x_ref, a_ref, wo_ref, gmix_ref, gpre_ref, wup_ref, wdown_ref, gpost_ref, o_ref):
    m = jnp.dot(a_ref[...], wo_ref[...], preferred_element_type=F32)
    _mlp_body(x_ref[...] + _rms(m, gmix_ref[...]), gpre_ref, wup_ref, wdown_ref, gpost_ref, o_ref)


def _mlp(x2d, gpre, wup_all, wdown_all, layer, gpost, attn=None, *, tm=512):
    t, d = x2d.shape
    row = lambda v: v.reshape(1, -1).astype(F32)
    act_spec = pl.BlockSpec((tm, d), lambda i: (i, 0))
    mlp_specs = [_const_spec((1, d)), _layer_spec(wup_all, layer), _layer_spec(wdown_all, layer), _const_spec((1, d))]
    mlp_args = (row(gpre), wup_all.astype(F32), wdown_all.astype(F32), row(gpost))
    if attn is None:
        kern, in_specs, args = _mlp_kernel, [act_spec] + mlp_specs, (x2d,) + mlp_args
    else:
        a2d, wo, gmix = attn
        kern = _attn_out_mlp_kernel
        in_specs = [act_spec, act_spec, _const_spec((d, d)), _const_spec((1, d))] + mlp_specs
        args = (x2d, a2d, wo.astype(BF16), row(gmix)) + mlp_args
    return pl.pallas_call(
        kern,
        grid=(t // tm,),
        in_specs=in_specs,
        out_specs=act_spec,
        out_shape=jax.ShapeDtypeStruct(x2d.shape, F32),
        compiler_params=pltpu.CompilerParams(
            dimension_semantics=("arbitrary",), vmem_limit_bytes=VMEM_LIMIT_BYTES),
        name="sqrelu_mlp" if attn is None else "attn_out_mlp",
    )(*args)


def _split3_bf16(x):
    hi = x.astype(BF16)
    r1 = x - hi.astype(F32)
    mid = r1.astype(BF16)
    lo = (r1 - mid.astype(F32)).astype(BF16)
    return hi, mid, lo


def _head_sq_norm_bound(xb, ind):
    xf = xb.astype(F32)
    nsq = jnp.dot((xf * xf).astype(BF16), ind, preferred_element_type=F32)
    return nsq * NORM_SLACK


def _qkv_kernel(x_ref, gpre_ref, win_ref, wf_ref, bf_ref, ind_ref, sel_ref, q_ref, k_ref, ka_ref, v_ref, st_ref,
                carry_ref, *, tm, d):
    x = x_ref[0]
    hb = _rms(x, gpre_ref[...]).astype(BF16)
    proj = jnp.dot(hb, win_ref[:, :3 * d].astype(BF16), preferred_element_type=F32)
    qb = (proj[:, :d] * (HEAD_DIM ** -0.5 * LOG2E)).astype(BF16)
    kb = proj[:, d:2 * d].astype(BF16)
    q_ref[0] = qb
    k_ref[0] = kb
    v_ref[0] = proj[:, 2 * d:].astype(BF16)

    def put_stat(t, per_head):
        rows = per_head.T
        for g in range(N_HEADS // 2):
            st_ref[0, t, g] = rows[2 * g:2 * g + 2, :]

    put_stat(1, _head_sq_norm_bound(qb, ind_ref[...]))
    put_stat(2, _head_sq_norm_bound(kb, ind_ref[...]))
    qk = (qb.astype(F32) * kb.astype(F32)).astype(BF16)
    put_stat(3, jnp.dot(qk, ind_ref[...], preferred_element_type=F32))

    f_logit = jnp.dot(hb, wf_ref[...], preferred_element_type=F32) + bf_ref[...]
    log_f = jax.nn.log_sigmoid(f_logit)

    @pl.when(pl.program_id(1) == 0)
    def _():
        carry_ref[...] = jnp.zeros_like(carry_ref)

    rows = lax.broadcasted_iota(jnp.int32, (tm, tm), 0)
    cols = lax.broadcasted_iota(jnp.int32, (tm, tm), 1)
    tri = (rows >= cols).astype(BF16)
    lane = lax.broadcasted_iota(jnp.int32, (1, LANES), 1)

    def pack3(v):
        v_hi, v_mid, v_lo = (t.astype(F32) for t in _split3_bf16(v))
        return jnp.where(lane < N_HEADS, v_hi, jnp.where(
            lane < 2 * N_HEADS, pltpu.roll(v_mid, N_HEADS, 1), pltpu.roll(v_lo, 2 * N_HEADS, 1))).astype(BF16)

    c3 = jnp.dot(tri, pack3(log_f), preferred_element_type=F32)
    csum = c3 + pltpu.roll(c3, LANES - N_HEADS, 1) + pltpu.roll(c3, LANES - 2 * N_HEADS, 1)
    f_cum = csum + carry_ref[...]
    carry_ref[...] = f_cum[tm - 1:tm, :]
    f2 = f_cum * LOG2E
    put_stat(0, f2)
    ka_ref[0] = jnp.dot(pack3(f2), sel_ref[...], preferred_element_type=F32).astype(BF16)


def _layer_spec(stacked, layer):
    return pl.BlockSpec((None,) + stacked.shape[1:], lambda *_: (layer, 0, 0), pipeline_mode=pl.Buffered(1))


def _qkv(x, gpre, w_in_all, layer, b_f, *, tm=512):
    b, s, d = x.shape
    w_in = w_in_all[layer]
    wf = jnp.zeros((d, LANES), F32).at[:, :N_HEADS].set(w_in[:, 3 * d:]).astype(BF16)
    bf = jnp.zeros((1, LANES), F32).at[0, :N_HEADS].set(b_f.astype(F32))
    ind = (jnp.arange(d)[:, None] // HEAD_DIM == jnp.arange(LANES)[None, :]).astype(BF16)
    rows = jnp.arange(LANES)[:, None]
    head, term = rows % N_HEADS, rows // N_HEADS
    dest = (head // 2) * LANES + (1 - head % 2) * HEAD_DIM
    sel = -((jnp.arange(d)[None, :] == dest + term) & (term < 3)).astype(BF16)
    kern = functools.partial(_qkv_kernel, tm=tm, d=d)
    act_spec = pl.BlockSpec((1, tm, d), lambda i, j: (i, j, 0))
    return pl.pallas_call(
        kern,
        grid=(b, s // tm),
        in_specs=[act_spec, _const_spec((1, d)), _layer_spec(w_in_all, layer), _const_spec((d, LANES)),
                  _const_spec((1, LANES)), _const_spec((d, LANES)), _const_spec((LANES, d))],
        out_specs=[act_spec, act_spec, act_spec, act_spec,
                   pl.BlockSpec((1, N_STATS, N_HEADS // 2, 2, tm), lambda i, j: (i, 0, 0, 0, j))],
        out_shape=[jax.ShapeDtypeStruct((b, s, d), BF16)] * 4
        + [jax.ShapeDtypeStruct((b, N_STATS, N_HEADS // 2, 2, s), F32)],
        scratch_shapes=[pltpu.VMEM((1, LANES), F32)],
        compiler_params=pltpu.CompilerParams(
            dimension_semantics=("arbitrary", "arbitrary"), vmem_limit_bytes=VMEM_LIMIT_BYTES),
        name="attn_qkv",
    )(x, gpre.reshape(1, -1).astype(F32), w_in_all.astype(F32), wf, bf, ind, sel)


def _attn_kernel(q_ref, k_ref, ka_ref, v_ref, st_ref, o_ref, g_ref, s_ref, nback_ref, *, s, tq):
    tk = tq
    lane = lax.broadcasted_iota(jnp.int32, (1, LANES), 1)
    head_lanes = [lane < HEAD_DIM, lane >= HEAD_DIM]
    bias_ones = [((lane >= HEAD_DIM) & (lane < HEAD_DIM + 3)).astype(BF16), (lane < 3).astype(BF16)]
    row_ids = lax.broadcasted_iota(jnp.int32, (tq, tk), 0)
    col_ids = lax.broadcasted_iota(jnp.int32, (tq, tk), 1)
    causal = col_ids <= row_ids
    pos = lax.broadcasted_iota(jnp.int32, (1, s), 1)

    nq, nk = st_ref[0, 1, 0], st_ref[0, 2, 0]
    kmax = jnp.max(nk, axis=1, keepdims=True)
    diag_lb = st_ref[0, 3, 0] - DIAG_SLACK * jnp.sqrt(nq * nk)
    g_ref[...] = jnp.sqrt(nq) * jnp.sqrt(kmax) - diag_lb + st_ref[0, 0, 0]

    def prev_block(q0):
        return q0 - tk if isinstance(q0, int) else pl.multiple_of(q0 - tk, tk)

    def load_q(q0):
        qp = q_ref[0, pl.ds(q0, tq), :]
        return [jnp.where(head_lanes[hh], qp, bias_ones[hh]) for hh in range(2)]

    def scores(qh, k0, masked):
        kp = k_ref[0, pl.ds(k0, tk), :]
        ap = ka_ref[0, pl.ds(k0, tk), :]
        out = []
        for hh in range(2):
            kh = jnp.where(head_lanes[hh], kp, ap)
            sc = lax.dot_general(qh[hh], kh, (((1,), (1,)), ((), ())), preferred_element_type=F32)
            out.append(jnp.where(causal, sc, MASK_VALUE) if masked else sc)
        return out

    def softmax_pv(scs, k0s, carry):
        vps = [v_ref[0, pl.ds(k0, tk), :] for k0 in k0s]
        new = []
        for hh in range(2):
            m, acc = carry[hh]
            m_new = m
            for sc in scs[hh]:
                m_new = jnp.maximum(m_new, jnp.max(sc, axis=1, keepdims=True))
            acc = jnp.exp2(m - m_new) * acc
            for sc, vp in zip(scs[hh], vps):
                p = jnp.exp2(sc - m_new).astype(BF16)
                vh = jnp.where(head_lanes[hh], vp, jnp.ones_like(vp))
                acc = acc + jnp.dot(p, vh, preferred_element_type=F32)
            new.append((m_new, acc))
        return tuple(new)

    def init_carry():
        return tuple((jnp.full((tq, 1), MASK_VALUE, F32), jnp.zeros((tq, LANES), F32)) for _ in range(2))

    for qi in range(1, s // tq):
        gmax = jnp.max(g_ref[:, qi * tq:(qi + 1) * tq], axis=1, keepdims=True)
        live = (gmax - st_ref[0, 0, 0] >= -EXP2_UNDERFLOW) & (pos < qi * tq)
        n_live = jnp.max(jnp.sum(live.astype(F32), axis=1, keepdims=True), axis=0, keepdims=True)
        nback_ref[qi] = jnp.ceil(n_live * (1.0 / tk)).astype(jnp.int32)[0, 0]

    def far_blocks(qi, qh):
        def step(kj, carry):
            k0 = pl.multiple_of(kj * tk, tk)
            return softmax_pv([[sc] for sc in scores(qh, k0, False)], [k0], carry)

        return lax.fori_loop(qi - nback_ref[qi], qi - 1, step, init_carry())

    half = tq // 2
    causal_top = (lax.broadcasted_iota(jnp.int32, (half, half), 1) <= lax.broadcasted_iota(jnp.int32, (half, half), 0))
    causal_bot = (lax.broadcasted_iota(jnp.int32, (half, tk), 1)
                  <= lax.broadcasted_iota(jnp.int32, (half, tk), 0) + half)

    def tail_scores(slot, q0, qh):
        for hh, sc in enumerate(scores(qh, prev_block(q0), False)):
            s_ref[slot, hh, 0] = sc
        kp = k_ref[0, pl.ds(q0, tk), :]
        ap = ka_ref[0, pl.ds(q0, tk), :]
        for hh in range(2):
            kh = jnp.where(head_lanes[hh], kp, ap)
            top = lax.dot_general(qh[hh][:half], kh[:half], (((1,), (1,)), ((), ())), preferred_element_type=F32)
            bot = lax.dot_general(qh[hh][half:], kh, (((1,), (1,)), ((), ())), preferred_element_type=F32)
            s_ref[slot, hh, 1, :half, :half] = jnp.where(causal_top, top, MASK_VALUE)
            s_ref[slot, hh, 1, :half, half:] = jnp.full((half, tk - half), MASK_VALUE, F32)
            s_ref[slot, hh, 1, half:, :] = jnp.where(causal_bot, bot, MASK_VALUE)

    def tail_finish(slot, q0, carry):
        v_prev = v_ref[0, pl.ds(prev_block(q0), tk), :]
        v_diag = v_ref[0, pl.ds(q0, tk), :]
        new = []
        for hh in range(2):
            m, acc = carry[hh]
            s_prev = s_ref[slot, hh, 0]
            s_diag = s_ref[slot, hh, 1]
            m_new = jnp.maximum(jnp.maximum(m, jnp.max(s_prev, axis=1, keepdims=True)),
                                jnp.max(s_diag, axis=1, keepdims=True))
            vh_prev = jnp.where(head_lanes[hh], v_prev, jnp.ones_like(v_prev))
            vh_diag = jnp.where(head_lanes[hh], v_diag, jnp.ones_like(v_diag))
            acc = jnp.exp2(m - m_new) * acc + jnp.dot(
                jnp.exp2(s_prev - m_new).astype(BF16), vh_prev, preferred_element_type=F32)
            p_diag = jnp.exp2(s_diag - m_new).astype(BF16)
            acc = acc + jnp.concatenate(
                [jnp.dot(p_diag[:half, :half], vh_diag[:half], preferred_element_type=F32),
                 jnp.dot(p_diag[half:], vh_diag, preferred_element_type=F32)], axis=0)
            new.append((m_new, acc))
        finish(q0, tuple(new))

    def finish(q0, carry):
        acc0, acc1 = carry[0][1], carry[1][1]
        out = jnp.where(head_lanes[0], acc0 / pltpu.roll(acc0, HEAD_DIM, 1), acc1 / pltpu.roll(acc1, HEAD_DIM, 1))
        o_ref[0, pl.ds(q0, tq), :] = out.astype(o_ref.dtype)

    finish(0, softmax_pv([[sc] for sc in scores(load_q(0), 0, True)], [0], init_carry()))

    n_q = s // tq
    last = n_q - 1
    tail_scores(0, tq, load_q(tq))

    def pair(i, _):
        qa = 2 * i + 1
        qb = qa + 1
        qn = jnp.minimum(qa + 2, last)
        a0, b0, n0 = (pl.multiple_of(x * tq, tq) for x in (qa, qb, qn))
        qha, qhb, qhn = load_q(a0), load_q(b0), load_q(n0)
        carry_a = far_blocks(qa, qha)
        carry_b = far_blocks(qb, qhb)
        tail_scores(1, b0, qhb)
        tail_finish(0, a0, carry_a)
        tail_scores(0, n0, qhn)
        tail_finish(1, b0, carry_b)
        return 0

    lax.fori_loop(0, last // 2, pair, 0)
    if last % 2 == 1:
        q0 = last * tq
        tail_finish(0, q0, far_blocks(last, load_q(q0)))


def _attention(q, k, kaug, v, stats, *, tq=512):
    b, s, d = q.shape
    n_groups = d // LANES
    heads_per_group = LANES // HEAD_DIM
    assert heads_per_group == 2 and stats.shape == (b, N_STATS, n_groups, heads_per_group, s)
    kern = functools.partial(_attn_kernel, s=s, tq=tq)
    col_spec = pl.BlockSpec((1, s, LANES), lambda i, j: (i, 0, j))
    return pl.pallas_call(
        kern,
        grid=(b, n_groups),
        in_specs=[col_spec, col_spec, col_spec, col_spec,
                  pl.BlockSpec((1, N_STATS, 1, heads_per_group, s), lambda i, j: (i, 0, j, 0, 0))],
        out_specs=col_spec,
        out_shape=jax.ShapeDtypeStruct((b, s, d), BF16),
        scratch_shapes=[pltpu.VMEM((heads_per_group, s), F32),
                        pltpu.VMEM((2, heads_per_group, 2, tq, tq), F32),
                        pltpu.SMEM((s // tq,), jnp.int32)],
        compiler_params=pltpu.CompilerParams(
            dimension_semantics=("arbitrary", "arbitrary"), vmem_limit_bytes=VMEM_LIMIT_BYTES),
        name="fox_attention",
    )(q, k, kaug, v, stats)


def kernel(x, g_mix_pre, g_mix_post, g_ffn_pre, g_ffn_post, conv_pw1_w, conv_pw1_b, conv_dw_w, conv_dw_b,
           conv_ln_g, conv_ln_b, conv_pw2_w, conv_pw2_b, attn_w_in, attn_b_f, attn_w_o, mlp_w_up, mlp_w_down):
    b, s, d = x.shape
    depth = g_mix_pre.shape[0]
    for i in range(depth):
        j = i // 2
        attn = None
        if i % 2 == 0:
            x = _conv_mixer(x, g_mix_pre[i], conv_pw1_w[j], conv_pw1_b[j], conv_dw_w[j], conv_dw_b[j],
                            conv_ln_g[j], conv_ln_b[j], conv_pw2_w[j], conv_pw2_b[j], g_mix_post[i])
        else:
            q, k, kaug, v, stats = _qkv(x, g_mix_pre[i], attn_w_in, j, attn_b_f[j])
            attn = (_attention(q, k, kaug, v, stats).reshape(b * s, d), attn_w_o[j], g_mix_post[i])
        x = _mlp(x.reshape(b * s, d), g_ffn_pre[i], mlp_w_up, mlp_w_down, i, g_ffn_post[i], attn).reshape(b, s, d)
    return x
```
